```python
import math
import jax, jax.numpy as jnp
from jax import lax
import numpy as np

D_MODEL = 1024
BATCH = 4
SEQ = 8192
DEPTH = 2

N_META = 16
D_MIX = D_MODEL
N_MIXERS = 4
W_GROUP = D_MIX // N_MIXERS
S5_GROUP = 16
S5_NGROUPS = W_GROUP // S5_GROUP
S5_STATE = 64
SC_WIDTH = 3
HG_HEADS = 4
HG_HEAD_DIM = W_GROUP // HG_HEADS
HG_CHUNK = 64
LRU_HEADS = 4
LRU_HEAD_DIM = W_GROUP // LRU_HEADS
LRU_CONV = 4
LRU_C = 8.0
D_FF = -(-8 * D_MODEL // (3 * 256)) * 256
ALPHA = (2 * DEPTH) ** 0.25
BETA = (8 * DEPTH) ** -0.25
EPS = 1e-5
N_SPLITS = 10
N_IN = W_GROUP * N_SPLITS

kernel_name = "hybrid_hymba_s5_conv_hgrn2_rglru"


def layer_norm(x, g, b):
    xf = x.astype(jnp.float32)
    mu = jnp.mean(xf, axis=-1, keepdims=True)
    var = jnp.mean(jnp.square(xf - mu), axis=-1, keepdims=True)
    y = (xf - mu) * lax.rsqrt(var + EPS) * g.astype(jnp.float32) + b.astype(jnp.float32)
    return y.astype(x.dtype)


def causal_depthwise_conv(x, w):
    k = w.shape[0]
    return lax.conv_general_dilated(
        x, w[:, None, :].astype(x.dtype), window_strides=(1,),
        padding=[(k - 1, 0)], dimension_numbers=("NWC", "WIO", "NWC"),
        feature_group_count=x.shape[-1])


def linear_scan(a, b):
    def combine(left, right):
        a_l, b_l = left
        a_r, b_r = right
        return a_l * a_r, a_r * b_l + b_r
    return lax.associative_scan(combine, (a, b), axis=1)[1]


def s5_mixer(u, lam_re, lam_im, b_re, b_im, c_re, c_im, d_skip, log_dt, glu_w, glu_b):
    f32 = jnp.float32
    bsz, n, _ = u.shape
    uf = u.astype(f32)
    lam = lax.complex(lam_re.astype(f32), lam_im.astype(f32))
    dt = jnp.exp(log_dt.astype(f32))[:, None]
    lam_bar = jnp.exp(lam * dt)
    bmat = lax.complex(b_re.astype(f32), b_im.astype(f32))
    b_bar = ((lam_bar - 1.0) / lam)[..., None] * bmat
    cmat = lax.complex(c_re.astype(f32), c_im.astype(f32))
    ug = uf.reshape(bsz, n, S5_NGROUPS, S5_GROUP)
    bu = jnp.einsum('gph,blgh->blgp', b_bar, ug)
    states = linear_scan(jnp.broadcast_to(lam_bar, bu.shape), bu)
    y = jnp.einsum('ghp,blgp->blgh', cmat, states).real.reshape(bsz, n, W_GROUP)
    y = y + d_skip.astype(f32) * uf
    g = jax.nn.gelu(y)
    return g * jax.nn.sigmoid(g @ glu_w.astype(f32) + glu_b.astype(f32))


def short_conv_mixer(h, gate_b, gate_c, conv_w):
    f32 = jnp.float32
    hc = gate_c.astype(f32) * h.astype(f32)
    return gate_b.astype(f32) * causal_depthwise_conv(hc, conv_w.astype(f32))


def hgrn2_mixer(q_in, f_in, i_in, g_in, lb, gnorm):
    f32 = jnp.float32
    bsz, n, _ = q_in.shape
    q = jax.nn.silu(q_in.astype(f32)) * (HG_HEAD_DIM ** -0.5)
    z = f_in.astype(f32)
    lbf = lb.astype(f32)
    log_f = jnp.logaddexp(jnp.log(lbf), jnp.log1p(-lbf) + jax.nn.log_sigmoid(z))
    k = (1.0 - lbf) * jax.nn.sigmoid(-z)
    v = i_in.astype(f32)
    pad = (-n) % HG_CHUNK
    nc = (n + pad) // HG_CHUNK

    def to_chunks(t):
        t = jnp.pad(t, ((0, 0), (pad, 0), (0, 0)))
        t = t.reshape(bsz, nc, HG_CHUNK, HG_HEADS, HG_HEAD_DIM)
        return t.transpose(1, 0, 3, 2, 4)

    mask = jnp.tril(jnp.ones((HG_CHUNK, HG_CHUNK), dtype=bool))[:, :, None]

    def step(state, inp):
        qc, kc, vc, gc = inp
        b = jnp.cumsum(gc, axis=-2)
        b_last = b[..., -1:, :]
        inter = jnp.einsum('bhtd,bhde->bhte', qc * jnp.exp(b), state)
        rel = jnp.where(mask, b[..., :, None, :] - b[..., None, :, :], -jnp.inf)
        scores = jnp.einsum('bhtd,bhsd,bhtsd->bhts', qc, kc, jnp.exp(rel))
        intra = jnp.einsum('bhts,bhse->bhte', scores, vc)
        new_state = (jnp.exp(b_last)[..., 0, :, None] * state
                     + jnp.einsum('bhsd,bhse->bhde', kc * jnp.exp(b_last - b), vc))
        return new_state, inter + intra

    state0 = jnp.zeros((bsz, HG_HEADS, HG_HEAD_DIM, HG_HEAD_DIM), f32)
    _, o = lax.scan(step, state0, (to_chunks(q), to_chunks(k), to_chunks(v), to_chunks(log_f)))
    o = o.transpose(1, 0, 3, 2, 4).reshape(bsz, nc * HG_CHUNK, HG_HEADS, HG_HEAD_DIM)[:, pad:]
    o = o * lax.rsqrt(jnp.mean(jnp.square(o), axis=-1, keepdims=True) + EPS)
    o = o * gnorm.astype(f32).reshape(HG_HEADS, HG_HEAD_DIM)
    return o.reshape(bsz, n, W_GROUP) * jax.nn.silu(g_in.astype(f32))


def rglru_mixer(xb, yb, conv_w, conv_b, wa, ba, wx, bx, a_param):
    f32 = jnp.float32
    bsz, n, _ = xb.shape
    xc = causal_depthwise_conv(xb.astype(f32), conv_w.astype(f32)) + conv_b.astype(f32)
    xh = xc.reshape(bsz, n, LRU_HEADS, LRU_HEAD_DIM)
    gate_a = jax.nn.sigmoid(jnp.einsum('blhi,hij->blhj', xh, wa.astype(f32)).reshape(bsz, n, W_GROUP) + ba.astype(f32))
    gate_x = jax.nn.sigmoid(jnp.einsum('blhi,hij->blhj', xh, wx.astype(f32)).reshape(bsz, n, W_GROUP) + bx.astype(f32))
    log_a = -LRU_C * gate_a * jax.nn.softplus(-a_param.astype(f32))
    mult = jnp.sqrt(-jnp.expm1(2.0 * log_a))
    h = linear_scan(jnp.exp(log_a), xc * gate_x * mult)
    return h * jax.nn.gelu(yb.astype(f32))


def setup_inputs(seed: int = 0) -> dict:
    key = jax.random.key(seed)
    ks = jax.random.split(key, 32)
    f32 = jnp.float32
    nrm = lambda k, s, sc: jax.random.normal(k, s, f32) * sc
    lam_im = jnp.pi * jnp.arange(S5_STATE, dtype=f32)
    lru_u = jax.random.uniform(ks[22], (DEPTH, W_GROUP), f32, 0.9, 0.999)
    return {
        "x": nrm(ks[0], (BATCH, SEQ, D_MODEL), 1.0),
        "meta_tokens": nrm(ks[1], (N_META, D_MODEL), 1.0),
        "hg_lb_raw": nrm(ks[2], (DEPTH, W_GROUP), 0.5),
        "w_in": nrm(ks[3], (DEPTH, D_MODEL, N_IN), D_MODEL ** -0.5),
        "w_out": nrm(ks[4], (DEPTH, D_MIX, D_MODEL), BETA * D_MIX ** -0.5),
        "s5_lam_re": -0.5 + nrm(ks[5], (DEPTH, S5_NGROUPS, S5_STATE), 0.01),
        "s5_lam_im": lam_im + nrm(ks[6], (DEPTH, S5_NGROUPS, S5_STATE), 0.01),
        "s5_b_re": nrm(ks[7], (DEPTH, S5_NGROUPS, S5_STATE, S5_GROUP), (2 * S5_GROUP) ** -0.5),
        "s5_b_im": nrm(ks[8], (DEPTH, S5_NGROUPS, S5_STATE, S5_GROUP), (2 * S5_GROUP) ** -0.5),
        "s5_c_re": nrm(ks[9], (DEPTH, S5_NGROUPS, S5_GROUP, S5_STATE), (2 * S5_STATE) ** -0.5),
        "s5_c_im": nrm(ks[10], (DEPTH, S5_NGROUPS, S5_GROUP, S5_STATE), (2 * S5_STATE) ** -0.5),
        "s5_d": nrm(ks[11], (DEPTH, W_GROUP), 1.0),
        "s5_log_dt": jax.random.uniform(ks[12], (DEPTH, S5_NGROUPS), f32, math.log(1e-3), math.log(1e-1)),
        "s5_glu_w": nrm(ks[13], (DEPTH, W_GROUP, W_GROUP), W_GROUP ** -0.5),
        "s5_glu_b": nrm(ks[14], (DEPTH, W_GROUP), 0.01),
        "sc_conv_w": nrm(ks[15], (DEPTH, SC_WIDTH, W_GROUP), SC_WIDTH ** -0.5),
        "hg_gnorm": 1.0 + nrm(ks[16], (DEPTH, W_GROUP), 0.01),
        "lru_conv_w": nrm(ks[17], (DEPTH, LRU_CONV, W_GROUP), LRU_CONV ** -0.5),
        "lru_conv_b": nrm(ks[18], (DEPTH, W_GROUP), 0.01),
        "lru_wa": nrm(ks[19], (DEPTH, LRU_HEADS, LRU_HEAD_DIM, LRU_HEAD_DIM), LRU_HEAD_DIM ** -0.5),
        "lru_ba": nrm(ks[20], (DEPTH, W_GROUP), 0.01),
        "lru_wx": nrm(ks[21], (DEPTH, LRU_HEADS, LRU_HEAD_DIM, LRU_HEAD_DIM), LRU_HEAD_DIM ** -0.5),
        "lru_bx": nrm(ks[23], (DEPTH, W_GROUP), 0.01),
        "lru_a_param": jnp.log(lru_u) - jnp.log1p(-lru_u),
        "ln1_g": 1.0 + nrm(ks[24], (DEPTH, D_MODEL), 0.01),
        "ln1_b": nrm(ks[25], (DEPTH, D_MODEL), 0.01),
        "w_ffn_in": nrm(ks[26], (DEPTH, D_MODEL, 2 * D_FF), D_MODEL ** -0.5),
        "w_ffn_out": nrm(ks[27], (DEPTH, D_FF, D_MODEL), BETA * D_FF ** -0.5),
        "ln2_g": 1.0 + nrm(ks[28], (DEPTH, D_MODEL), 0.01),
        "ln2_b": nrm(ks[29], (DEPTH, D_MODEL), 0.01),
    }


def reference(x, meta_tokens, hg_lb_raw, w_in, w_out, s5_lam_re, s5_lam_im, s5_b_re, s5_b_im,
              s5_c_re, s5_c_im, s5_d, s5_log_dt, s5_glu_w, s5_glu_b, sc_conv_w, hg_gnorm,
              lru_conv_w, lru_conv_b, lru_wa, lru_ba, lru_wx, lru_bx, lru_a_param,
              ln1_g, ln1_b, w_ffn_in, w_ffn_out, ln2_g, ln2_b):
    f32 = jnp.float32
    bsz = x.shape[0]
    meta = jnp.broadcast_to(meta_tokens.astype(x.dtype)[None], (bsz, N_META, D_MODEL))
    h = jnp.concatenate([meta, x], axis=1)
    lb_all = jnp.cumsum(jax.nn.softmax(hg_lb_raw.astype(f32), axis=0), axis=0)
    lb_all = lb_all - lb_all[0:1]
    for l in range(DEPTH):
        proj = h @ w_in[l]
        (s5_u, sc_h, sc_b, sc_c, hg_q, hg_f, hg_i, hg_g, lru_x, lru_y) = jnp.split(proj, N_SPLITS, axis=-1)
        y_a = s5_mixer(s5_u, s5_lam_re[l], s5_lam_im[l], s5_b_re[l], s5_b_im[l], s5_c_re[l],
                       s5_c_im[l], s5_d[l], s5_log_dt[l], s5_glu_w[l], s5_glu_b[l])
        y_b = short_conv_mixer(sc_h, sc_b, sc_c, sc_conv_w[l])
        y_c = hgrn2_mixer(hg_q, hg_f, hg_i, hg_g, lb_all[l], hg_gnorm[l])
        y_d = rglru_mixer(lru_x, lru_y, lru_conv_w[l], lru_conv_b[l], lru_wa[l], lru_ba[l],
                          lru_wx[l], lru_bx[l], lru_a_param[l])
        mix = jnp.concatenate([y_a, y_b, y_c, y_d], axis=-1).astype(h.dtype) @ w_out[l]
        h = layer_norm(ALPHA * h + mix, ln1_g[l], ln1_b[l])
        gate, up = jnp.split(h @ w_ffn_in[l], 2, axis=-1)
        ffn = (jax.nn.silu(gate) * up) @ w_ffn_out[l]
        h = layer_norm(ALPHA * h + ffn, ln2_g[l], ln2_b[l])
    return h[:, N_META:]
```

```python
import functools
import math

import jax
import jax.numpy as jnp
from jax import lax
from jax.experimental import pallas as pl
from jax.experimental.pallas import tpu as pltpu

F32 = jnp.float32
BF16 = jnp.bfloat16

D_MODEL = 1024
DEPTH = 2
N_META = 16
W_GROUP = 256
N_IN = 10 * W_GROUP
S5_GROUP = 16
S5_NGROUPS = 16
S5_STATE = 64
S5_W = S5_NGROUPS * S5_STATE
HG_HEADS = 4
HG_HEAD_DIM = 64
LRU_HEADS = 4
LRU_C = 8.0
D_FF = 2816
ALPHA = (2 * DEPTH) ** 0.25
EPS = 1e-5

SUBLANES = 8
LANES = 128
TIME_BLOCK = 640
SEG = TIME_BLOCK // SUBLANES
HG_CHUNK = 64
HG_LEVELS = 6
FFN_SPLIT = 2
VMEM_LIMIT_BYTES = 56 * 1024 * 1024

(V_S5_D, V_GLU_B, V_SC_W0, V_SC_W1, V_SC_W2, V_HG_LOGLB, V_HG_LOG1MLB, V_HG_1MLB, V_HG_GNORM,
 V_LRU_W0, V_LRU_W1, V_LRU_W2, V_LRU_W3, V_LRU_CB, V_LRU_CA, V_LRU_BA, V_LRU_BX) = range(17)
N_VEC = 24


def _sigmoid(x):
    return 1.0 / (1.0 + jnp.exp(-x))


def _gelu_tanh(x):
    c = math.sqrt(2.0 / math.pi)
    return x * (0.5 * (1.0 + jnp.tanh(c * (x + 0.044715 * (x * x * x)))))


def _dot(a, b):
    return jnp.dot(a, b, preferred_element_type=F32)


def _dot_nt(a, b):
    return lax.dot_general(a, b, (((1,), (1,)), ((), ())), preferred_element_type=F32)


def _dot_tn(a, b):
    return lax.dot_general(a, b, (((0,), (0,)), ((), ())), preferred_element_type=F32)


def _s5_mixer(segin_ref, vec_ref, bmat_ref, cmat_ref, lam_ref, ptab_ref, glu_w_ref,
              uperm_ref, bu_ref, xbf_ref, e_ref, cin_ref, state_ref, segout_ref):
    n = S5_W

    def gather(r, c):
        rows = pl.ds(pl.multiple_of(r * SUBLANES, SUBLANES), SUBLANES)
        for j in range(W_GROUP // LANES):
            uperm_ref[rows, j * LANES:(j + 1) * LANES] = segin_ref[j, pl.ds(r, SUBLANES, stride=SEG), :]
        return c
    lax.fori_loop(0, SEG, gather, 0)

    bu_ref[...] = _dot(uperm_ref[...].astype(BF16), bmat_ref[...])

    lam_r = jnp.broadcast_to(lam_ref[0:1, 0:n], (SUBLANES, n))
    lam_i = jnp.broadcast_to(lam_ref[0:1, n:2 * n], (SUBLANES, n))

    def scan(r, carry):
        xr, xi = carry
        rows = pl.ds(pl.multiple_of(r * SUBLANES, SUBLANES), SUBLANES)
        nr = lam_r * xr - lam_i * xi + bu_ref[rows, 0:n]
        ni = lam_r * xi + lam_i * xr + bu_ref[rows, n:2 * n]
        bu_ref[rows, 0:n] = nr
        bu_ref[rows, n:2 * n] = ni
        return nr, ni
    zero = jnp.zeros((SUBLANES, n), F32)
    er, ei = lax.fori_loop(0, SEG, scan, (zero, zero))
    e_ref[:, 0:n] = er
    e_ref[:, n:2 * n] = ei

    pr, pi_ = lam_ref[1:2, 0:n], lam_ref[1:2, n:2 * n]
    cr, ci = state_ref[0:1, 0:n], state_ref[0:1, n:2 * n]
    for s in range(SUBLANES):
        cin_ref[s:s + 1, 0:n] = cr
        cin_ref[s:s + 1, n:2 * n] = ci
        sr, si = e_ref[s:s + 1, 0:n], e_ref[s:s + 1, n:2 * n]
        cr, ci = sr + (pr * cr - pi_ * ci), si + (pr * ci + pi_ * cr)
    state_ref[0:1, 0:n] = cr
    state_ref[0:1, n:2 * n] = ci

    cin_r = cin_ref[:, 0:n]
    cin_i = cin_ref[:, n:2 * n]

    def fix(rr, c):
        parts_r, parts_i = [], []
        for j in range(2):
            r = rr * 2 + j
            rows = pl.ds(pl.multiple_of(r * SUBLANES, SUBLANES), SUBLANES)
            qr = jnp.broadcast_to(ptab_ref[pl.ds(r, 1), 0:n], (SUBLANES, n))
            qi = jnp.broadcast_to(ptab_ref[pl.ds(r, 1), n:2 * n], (SUBLANES, n))
            parts_r.append(bu_ref[rows, 0:n] + (qr * cin_r - qi * cin_i))
            parts_i.append(bu_ref[rows, n:2 * n] + (qr * cin_i + qi * cin_r))
        rows2 = pl.ds(pl.multiple_of(rr * 2 * SUBLANES, 2 * SUBLANES), 2 * SUBLANES)
        xbf_ref[rows2, 0:n] = jnp.concatenate(parts_r, axis=0).astype(BF16)
        xbf_ref[rows2, n:2 * n] = jnp.concatenate(parts_i, axis=0).astype(BF16)
        return c
    lax.fori_loop(0, SEG // 2, fix, 0)

    y = _dot(xbf_ref[...], cmat_ref[...])
    y = y + vec_ref[V_S5_D:V_S5_D + 1, :] * uperm_ref[...]
    g = _gelu_tanh(y)
    gate = _sigmoid(_dot(g.astype(BF16), glu_w_ref[...]) + vec_ref[V_GLU_B:V_GLU_B + 1, :])
    uperm_ref[...] = g * gate

    def scatter(r, c):
        rows = pl.ds(pl.multiple_of(r * SUBLANES, SUBLANES), SUBLANES)
        for j in range(W_GROUP // LANES):
            segout_ref[j, pl.ds(r, SUBLANES, stride=SEG), :] = uperm_ref[rows, j * LANES:(j + 1) * LANES]
        return c
    lax.fori_loop(0, SEG, scatter, 0)


def _short_conv_mixer(proj_ref, vec_ref, buf_ref, out_ref):
    tb = TIME_BLOCK
    h = proj_ref[:, W_GROUP:2 * W_GROUP]
    gb = proj_ref[:, 2 * W_GROUP:3 * W_GROUP]
    gc = proj_ref[:, 3 * W_GROUP:4 * W_GROUP]
    buf_ref[SUBLANES:SUBLANES + tb, :] = gc * h
    w0 = vec_ref[V_SC_W0:V_SC_W0 + 1, :]
    w1 = vec_ref[V_SC_W1:V_SC_W1 + 1, :]
    w2 = vec_ref[V_SC_W2:V_SC_W2 + 1, :]
    conv = (w0 * buf_ref[SUBLANES - 2:SUBLANES - 2 + tb, :]
            + w1 * buf_ref[SUBLANES - 1:SUBLANES - 1 + tb, :]
            + w2 * buf_ref[SUBLANES:SUBLANES + tb, :])
    out_ref[:, W_GROUP:2 * W_GROUP] = (gb * conv).astype(BF16)
    buf_ref[0:SUBLANES, :] = buf_ref[tb:tb + SUBLANES, :]


def _sibling(x, n):
    c, w = x.shape
    if n < SUBLANES:
        x3 = x.reshape(c // SUBLANES, SUBLANES, w)
        fwd = pltpu.roll(x3, n, 1)
        if 2 * n == SUBLANES:
            return fwd.reshape(c, w)
        bwd = pltpu.roll(x3, SUBLANES - n, 1)
        row = lax.broadcasted_iota(jnp.int32, x3.shape, 1)
        return jnp.where((row & n) != 0, fwd, bwd).reshape(c, w)
    x4 = x.reshape(c // (2 * n), 2, n, w)
    return jnp.concatenate([x4[:, 1:2], x4[:, 0:1]], axis=1).reshape(c, w)


def _hgrn2_mixer(proj_ref, vec_ref, headmask_ref, bdmask_ref, ones_ref, lv_ref, state_ref, out_ref):
    c = HG_CHUNK
    loglb = vec_ref[V_HG_LOGLB:V_HG_LOGLB + 1, :]
    log1mlb = vec_ref[V_HG_LOG1MLB:V_HG_LOG1MLB + 1, :]
    onemlb = vec_ref[V_HG_1MLB:V_HG_1MLB + 1, :]
    gnorm = vec_ref[V_HG_GNORM:V_HG_GNORM + 1, :]

    def stack_heads(x_bf):
        return jnp.concatenate([x_bf] * HG_HEADS, axis=0) * headmask_ref[...]

    def chunk(ci, carry):
        rows = pl.ds(pl.multiple_of(ci * c, c), c)
        q_in = proj_ref[rows, 4 * W_GROUP:5 * W_GROUP]
        z = proj_ref[rows, 5 * W_GROUP:6 * W_GROUP]
        v = proj_ref[rows, 6 * W_GROUP:7 * W_GROUP]
        g_in = proj_ref[rows, 7 * W_GROUP:8 * W_GROUP]

        q = q_in * _sigmoid(q_in) * (HG_HEAD_DIM ** -0.5)
        ez = jnp.exp(-jnp.abs(z))
        softp = jnp.log1p(ez)
        log_sig = jnp.minimum(z, 0.0) - softp
        cpl = log1mlb + log_sig
        m = jnp.maximum(loglb, cpl)
        g = m + jnp.log(jnp.exp(loglb - m) + jnp.exp(cpl - m))
        k = onemlb * _sigmoid(-z)

        lv = lv_ref[...]
        row = lax.broadcasted_iota(jnp.int32, (c, W_GROUP), 0)
        v_bd = stack_heads(v.astype(BF16))

        scores = jnp.where(lv == HG_LEVELS, _dot_nt(q.astype(BF16), stack_heads(k.astype(BF16))), 0.0)
        cin = g
        sfx = jnp.zeros_like(g)
        for level in range(HG_LEVELS):
            n = 1 << level
            qt = (q * jnp.exp(cin)).astype(BF16)
            kt = (k * jnp.exp(sfx)).astype(BF16)
            s_n = _dot_nt(qt, stack_heads(kt))
            scores = jnp.where(lv == level, s_n, scores)
            sib = _sibling(cin + sfx, n)
            right = (row & n) != 0
            cin = cin + jnp.where(right, sib, 0.0)
            sfx = sfx + jnp.where(right, 0.0, sib)

        st = state_ref[...]
        inter = _dot_nt((q * jnp.exp(cin)).astype(BF16), st.astype(BF16))
        intra = _dot(scores.astype(BF16), v_bd)
        o = inter + intra

        decay = jnp.exp(cin[c - 1:c, :])
        khat = (k * jnp.exp(sfx)).astype(BF16)
        upd = _dot_tn(v.astype(BF16), khat)
        state_ref[...] = st * decay + upd * bdmask_ref[...]

        ms = _dot((o * o).astype(BF16), ones_ref[...])
        o = o * lax.rsqrt(ms + EPS) * gnorm
        out_ref[rows, 2 * W_GROUP:3 * W_GROUP] = (o * (g_in * _sigmoid(g_in))).astype(BF16)
        return carry
    lax.fori_loop(0, TIME_BLOCK // c, chunk, 0)


def _rglru_mixer(proj_ref, segin_ref, vec_ref, wab_ref, xbuf_ref, a_ref, b_ref, h_ref, p_ref,
                 e_ref, cin_ref, state_ref, segout_ref):
    tb = TIME_BLOCK
    xbuf_ref[SUBLANES:SUBLANES + tb, :] = proj_ref[:, 8 * W_GROUP:9 * W_GROUP]
    xc = vec_ref[V_LRU_CB:V_LRU_CB + 1, :] + (
        vec_ref[V_LRU_W0:V_LRU_W0 + 1, :] * xbuf_ref[SUBLANES - 3:SUBLANES - 3 + tb, :]
        + vec_ref[V_LRU_W1:V_LRU_W1 + 1, :] * xbuf_ref[SUBLANES - 2:SUBLANES - 2 + tb, :]
        + vec_ref[V_LRU_W2:V_LRU_W2 + 1, :] * xbuf_ref[SUBLANES - 1:SUBLANES - 1 + tb, :]
        + vec_ref[V_LRU_W3:V_LRU_W3 + 1, :] * xbuf_ref[SUBLANES:SUBLANES + tb, :])
    xbuf_ref[0:SUBLANES, :] = xbuf_ref[tb:tb + SUBLANES, :]

    gates = _dot(xc.astype(BF16), wab_ref[...])
    gate_a = _sigmoid(gates[:, 0:W_GROUP] + vec_ref[V_LRU_BA:V_LRU_BA + 1, :])
    gate_x = _sigmoid(gates[:, W_GROUP:2 * W_GROUP] + vec_ref[V_LRU_BX:V_LRU_BX + 1, :])
    log_a = vec_ref[V_LRU_CA:V_LRU_CA + 1, :] * gate_a
    a = jnp.exp(log_a)
    b = xc * gate_x * jnp.sqrt(-jnp.tanh(log_a) * (a * a + 1.0))
    nslab = W_GROUP // LANES
    for j in range(nslab):
        a_ref[j] = a[:, j * LANES:(j + 1) * LANES]
        b_ref[j] = b[:, j * LANES:(j + 1) * LANES]

    def seg_tile(ref, first, r):
        return jnp.concatenate(
            [ref[first + j, pl.ds(r, SUBLANES, stride=SEG), :] for j in range(nslab)], axis=1)

    def scan(r, carry):
        h, p = carry
        a = seg_tile(a_ref, 0, r)
        h = a * h + seg_tile(b_ref, 0, r)
        p = a * p
        rows = pl.ds(pl.multiple_of(r * SUBLANES, SUBLANES), SUBLANES)
        h_ref[rows, :] = h
        p_ref[rows, :] = p
        return h, p
    e, ptot = lax.fori_loop(0, SEG, scan, (jnp.zeros((SUBLANES, W_GROUP), F32),
                                           jnp.ones((SUBLANES, W_GROUP), F32)))
    e_ref[0:SUBLANES, :] = e
    e_ref[SUBLANES:2 * SUBLANES, :] = ptot
    cur = state_ref[0:1, :]
    for s in range(SUBLANES):
        cin_ref[s:s + 1, :] = cur
        cur = e_ref[s:s + 1, :] + e_ref[SUBLANES + s:SUBLANES + s + 1, :] * cur
    state_ref[0:1, :] = cur
    cin = cin_ref[...]

    def fix(r, c):
        rows = pl.ds(pl.multiple_of(r * SUBLANES, SUBLANES), SUBLANES)
        h = h_ref[rows, :] + p_ref[rows, :] * cin
        res = h * _gelu_tanh(seg_tile(segin_ref, nslab, r))
        for j in range(nslab):
            segout_ref[nslab + j, pl.ds(r, SUBLANES, stride=SEG), :] = res[:, j * LANES:(j + 1) * LANES]
        return c
    lax.fori_loop(0, SEG, fix, 0)


def _mixer_kernel(h_ref, w_in_ref, vec_ref, bmat_ref, cmat_ref, lam_ref, ptab_ref, glu_w_ref,
                  wab_ref, headmask_ref, bdmask_ref, ones_ref, lv_ref,
                  out_ref,
                  proj_ref, segin_ref, segout_ref, uperm_ref, bu_ref, xbf_ref, s5_e_ref, s5_cin_ref, s5_state_ref,
                  sc_buf_ref, hg_state_ref, lru_xbuf_ref, lru_a_ref, lru_b_ref, lru_h_ref, lru_p_ref,
                  lru_e_ref, lru_cin_ref, lru_state_ref):
    @pl.when(pl.program_id(1) == 0)
    def _():
        s5_state_ref[...] = jnp.zeros_like(s5_state_ref)
        sc_buf_ref[0:SUBLANES, :] = jnp.zeros((SUBLANES, W_GROUP), F32)
        hg_state_ref[...] = jnp.zeros_like(hg_state_ref)
        lru_xbuf_ref[0:SUBLANES, :] = jnp.zeros((SUBLANES, W_GROUP), F32)
        lru_state_ref[...] = jnp.zeros_like(lru_state_ref)

    proj_ref[...] = _dot(h_ref[...].astype(BF16), w_in_ref[...])
    nslab = W_GROUP // LANES
    for j in range(nslab):
        segin_ref[j] = proj_ref[:, j * LANES:(j + 1) * LANES]
        segin_ref[nslab + j] = proj_ref[:, 9 * W_GROUP + j * LANES:9 * W_GROUP + (j + 1) * LANES]
    _s5_mixer(segin_ref, vec_ref, bmat_ref, cmat_ref, lam_ref, ptab_ref, glu_w_ref,
              uperm_ref, bu_ref, xbf_ref, s5_e_ref, s5_cin_ref, s5_state_ref, segout_ref)
    _short_conv_mixer(proj_ref, vec_ref, sc_buf_ref, out_ref)
    _hgrn2_mixer(proj_ref, vec_ref, headmask_ref, bdmask_ref, ones_ref, lv_ref, hg_state_ref, out_ref)
    _rglru_mixer(proj_ref, segin_ref, vec_ref, wab_ref, lru_xbuf_ref, lru_a_ref, lru_b_ref, lru_h_ref,
                 lru_p_ref, lru_e_ref, lru_cin_ref, lru_state_ref, segout_ref)
    for j in range(nslab):
        out_ref[:, j * LANES:(j + 1) * LANES] = segout_ref[j].astype(BF16)
        out_ref[:, 3 * W_GROUP + j * LANES:3 * W_GROUP + (j + 1) * LANES] = segout_ref[nslab + j].astype(BF16)


def _const_spec(shape):
    return pl.BlockSpec(shape, lambda *_: (0,) * len(shape), pipeline_mode=pl.Buffered(1))


def _mixer_call(h, w_in, vec, bmat, cmat, lam, ptab, glu_w, wab, headmask, bdmask, ones, lv):
    bsz, lp, _ = h.shape
    tb = TIME_BLOCK
    consts = (w_in, vec, bmat, cmat, lam, ptab, glu_w, wab, headmask, bdmask, ones, lv)
    scratch = [
        pltpu.VMEM((tb, N_IN), F32),
        pltpu.VMEM((2 * W_GROUP // LANES, tb, LANES), F32),
        pltpu.VMEM((2 * W_GROUP // LANES, tb, LANES), F32),
        pltpu.VMEM((tb, W_GROUP), F32),
        pltpu.VMEM((tb, 2 * S5_W), F32),
        pltpu.VMEM((tb, 2 * S5_W), BF16),
        pltpu.VMEM((SUBLANES, 2 * S5_W), F32),
        pltpu.VMEM((SUBLANES, 2 * S5_W), F32),
        pltpu.VMEM((SUBLANES, 2 * S5_W), F32),
        pltpu.VMEM((tb + SUBLANES, W_GROUP), F32),
        pltpu.VMEM((W_GROUP, W_GROUP), F32),
        pltpu.VMEM((tb + SUBLANES, W_GROUP), F32),
        pltpu.VMEM((W_GROUP // LANES, tb, LANES), F32),
        pltpu.VMEM((W_GROUP // LANES, tb, LANES), F32),
        pltpu.VMEM((tb, W_GROUP), F32),
        pltpu.VMEM((tb, W_GROUP), F32),
        pltpu.VMEM((2 * SUBLANES, W_GROUP), F32),
        pltpu.VMEM((SUBLANES, W_GROUP), F32),
        pltpu.VMEM((SUBLANES, W_GROUP), F32),
    ]
    return pl.pallas_call(
        _mixer_kernel,
        grid=(bsz, lp // tb),
        in_specs=[pl.BlockSpec((None, tb, D_MODEL), lambda b, t: (b, t, 0))]
                 + [_const_spec(c.shape) for c in consts],
        out_specs=pl.BlockSpec((None, tb, 4 * W_GROUP), lambda b, t: (b, t, 0)),
        out_shape=jax.ShapeDtypeStruct((bsz, lp, 4 * W_GROUP), BF16),
        scratch_shapes=scratch,
        compiler_params=pltpu.CompilerParams(
            dimension_semantics=("arbitrary", "arbitrary"), vmem_limit_bytes=VMEM_LIMIT_BYTES),
    )(h, *consts)


def _layer_norm(x, g, b):
    mu = jnp.mean(x, axis=-1, keepdims=True)
    xc = x - mu
    var = jnp.mean(xc * xc, axis=-1, keepdims=True)
    return xc * lax.rsqrt(var + EPS) * g + b


def _ffn_kernel(mix_ref, h_ref, w_out_ref, ln_ref, w_ffn_in_ref, w_ffn_out_ref, out_ref):
    a = ALPHA * h_ref[...] + _dot(mix_ref[...], w_out_ref[...])
    hn = _layer_norm(a, ln_ref[0:1, :], ln_ref[1:2, :])
    hb = hn.astype(BF16)
    part = D_FF // FFN_SPLIT
    ffn = None
    for j in range(FFN_SPLIT):
        gate = _dot(hb, w_ffn_in_ref[:, j * part:(j + 1) * part])
        up = _dot(hb, w_ffn_in_ref[:, D_FF + j * part:D_FF + (j + 1) * part])
        act = (gate * _sigmoid(gate) * up).astype(BF16)
        contrib = _dot(act, w_ffn_out_ref[j * part:(j + 1) * part, :])
        ffn = contrib if ffn is None else ffn + contrib
    out_ref[...] = _layer_norm(ALPHA * hn + ffn, ln_ref[2:3, :], ln_ref[3:4, :])


def _ffn_call(mix, h, w_out, ln, w_ffn_in, w_ffn_out):
    bsz, lp, _ = h.shape
    tb = TIME_BLOCK
    consts = (w_out, ln, w_ffn_in, w_ffn_out)
    return pl.pallas_call(
        _ffn_kernel,
        grid=(bsz, lp // tb),
        in_specs=[pl.BlockSpec((None, tb, 4 * W_GROUP), lambda b, t: (b, t, 0)),
                  pl.BlockSpec((None, tb, D_MODEL), lambda b, t: (b, t, 0))]
                 + [_const_spec(c.shape) for c in consts],
        out_specs=pl.BlockSpec((None, tb, D_MODEL), lambda b, t: (b, t, 0)),
        out_shape=jax.ShapeDtypeStruct((bsz, lp, D_MODEL), F32),
        compiler_params=pltpu.CompilerParams(
            dimension_semantics=("arbitrary", "arbitrary"), vmem_limit_bytes=VMEM_LIMIT_BYTES),
    )(mix, h, *consts)


def _block_diag(blocks):
    n, a, b = blocks.shape
    eye = jnp.eye(n, dtype=blocks.dtype)
    return jnp.einsum('nab,nm->namb', blocks, eye).reshape(n * a, n * b)


def _hgrn2_level_map():
    t = jnp.arange(HG_CHUNK, dtype=jnp.int32)[:, None]
    s = jnp.arange(HG_CHUNK, dtype=jnp.int32)[None, :]
    x = t ^ s
    lv = jnp.full((HG_CHUNK, HG_CHUNK), -1, jnp.int32)
    for level in range(HG_LEVELS):
        n = 1 << level
        lv = jnp.where((x >= n) & (x < 2 * n) & (t > s), level, lv)
    lv = jnp.where(t == s, HG_LEVELS, lv)
    return jnp.tile(lv, (1, HG_HEADS))


def _prepare_layer(l, lb, w_in, s5_lam_re, s5_lam_im, s5_b_re, s5_b_im, s5_c_re, s5_c_im, s5_d,
                   s5_log_dt, s5_glu_w, s5_glu_b, sc_conv_w, hg_gnorm, lru_conv_w, lru_conv_b,
                   lru_wa, lru_ba, lru_wx, lru_bx, lru_a_param):
    lam = lax.complex(s5_lam_re[l].astype(F32), s5_lam_im[l].astype(F32))
    dt = jnp.exp(s5_log_dt[l].astype(F32))[:, None]
    lam_dt = lam * dt
    lam_bar = jnp.exp(lam_dt)
    b_bar = ((lam_bar - 1.0) / lam)[..., None] * lax.complex(s5_b_re[l].astype(F32), s5_b_im[l].astype(F32))
    b_t = jnp.transpose(b_bar, (0, 2, 1))
    bmat = jnp.concatenate([_block_diag(b_t.real), _block_diag(b_t.imag)], axis=1)
    c_t = jnp.transpose(lax.complex(s5_c_re[l].astype(F32), s5_c_im[l].astype(F32)), (0, 2, 1))
    cmat = jnp.concatenate([_block_diag(c_t.real), -_block_diag(c_t.imag)], axis=0)

    def flat(zc):
        return jnp.concatenate([zc.real.reshape(-1), zc.imag.reshape(-1)])
    lam_rows = jnp.stack([flat(lam_bar), flat(jnp.exp(lam_dt * SEG))])
    steps = jnp.arange(1, SEG + 1, dtype=F32)[:, None, None]
    pw = jnp.exp(lam_dt[None] * steps)
    ptab = jnp.concatenate([pw.real.reshape(SEG, -1), pw.imag.reshape(SEG, -1)], axis=1)

    one_m_lb = 1.0 - lb
    rows = [None] * N_VEC
    rows[V_S5_D] = s5_d[l]
    rows[V_GLU_B] = s5_glu_b[l]
    rows[V_SC_W0], rows[V_SC_W1], rows[V_SC_W2] = sc_conv_w[l, 0], sc_conv_w[l, 1], sc_conv_w[l, 2]
    rows[V_HG_LOGLB] = jnp.maximum(jnp.log(lb), -1e30)
    rows[V_HG_LOG1MLB] = jnp.log1p(-lb)
    rows[V_HG_1MLB] = one_m_lb
    rows[V_HG_GNORM] = hg_gnorm[l]
    for i, v in enumerate((V_LRU_W0, V_LRU_W1, V_LRU_W2, V_LRU_W3)):
        rows[v] = lru_conv_w[l, i]
    rows[V_LRU_CB] = lru_conv_b[l]
    rows[V_LRU_CA] = -LRU_C * jax.nn.softplus(-lru_a_param[l].astype(F32))
    rows[V_LRU_BA] = lru_ba[l]
    rows[V_LRU_BX] = lru_bx[l]
    zero = jnp.zeros((W_GROUP,), F32)
    vec = jnp.stack([zero if r is None else r.astype(F32) for r in rows])
    wab = jnp.concatenate([_block_diag(lru_wa[l].astype(F32)), _block_diag(lru_wx[l].astype(F32))], axis=1)
    return (w_in[l].astype(BF16), vec, bmat.astype(BF16), cmat.astype(BF16), lam_rows, ptab,
            s5_glu_w[l].astype(BF16), wab.astype(BF16))


def kernel(x, meta_tokens, hg_lb_raw, w_in, w_out, s5_lam_re, s5_lam_im, s5_b_re, s5_b_im, s5_c_re, s5_c_im, s5_d, s5_log_dt, s5_glu_w, s5_glu_b, sc_conv_w, hg_gnorm, lru_conv_w, lru_conv_b, lru_wa, lru_ba, lru_wx, lru_bx, lru_a_param, ln1_g, ln1_b, w_ffn_in, w_ffn_out, ln2_g, ln2_b):
    bsz, seq, d = x.shape
    assert d == D_MODEL
    n_tok = N_META + seq
    lp = -(-n_tok // TIME_BLOCK) * TIME_BLOCK
    meta = jnp.broadcast_to(meta_tokens.astype(x.dtype)[None], (bsz, N_META, d))
    h = jnp.concatenate([meta, x, jnp.zeros((bsz, lp - n_tok, d), x.dtype)], axis=1)

    lb_all = jnp.cumsum(jax.nn.softmax(hg_lb_raw.astype(F32), axis=0), axis=0)
    lb_all = lb_all - lb_all[0:1]

    head = jnp.arange(W_GROUP) // HG_HEAD_DIM
    same_head = head[:, None] == head[None, :]
    bdmask = same_head.astype(F32)
    headmask = same_head.astype(BF16)
    ones = (same_head.astype(F32) / HG_HEAD_DIM).astype(BF16)
    lv = _hgrn2_level_map()

    for l in range(DEPTH):
        prep = _prepare_layer(l, lb_all[l], w_in, s5_lam_re, s5_lam_im, s5_b_re, s5_b_im, s5_c_re, s5_c_im,
                              s5_d, s5_log_dt, s5_glu_w, s5_glu_b, sc_conv_w, hg_gnorm, lru_conv_w,
                              lru_conv_b, lru_wa, lru_ba, lru_wx, lru_bx, lru_a_param)
        mix = _mixer_call(h, *prep, headmask, bdmask, ones, lv)
        ln = jnp.stack([ln1_g[l], ln1_b[l], ln2_g[l], ln2_b[l]]).astype(F32)
        h = _ffn_call(mix, h, w_out[l].astype(BF16), ln, w_ffn_in[l].astype(BF16), w_ffn_out[l].astype(BF16))
    return h[:, N_META:n_tok]
```

```python
import functools
import math

import jax
import jax.numpy as jnp
from jax import lax
from jax.experimental import pallas as pl
from jax.experimental.pallas import tpu as pltpu

F32 = jnp.float32
BF16 = jnp.bfloat16

D_MODEL = 1024
DEPTH = 2
N_META = 16
W_GROUP = 256
N_IN = 10 * W_GROUP
S5_GROUP = 16
S5_NGROUPS = 16
S5_STATE = 64
S5_W = S5_NGROUPS * S5_STATE
HG_HEADS = 4
HG_HEAD_DIM = 64
LRU_HEADS = 4
LRU_C = 8.0
D_FF = 2816
ALPHA = (2 * DEPTH) ** 0.25
EPS = 1e-5

SUBLANES = 8
LANES = 128
TIME_BLOCK = 640
SEG = TIME_BLOCK // SUBLANES
HG_CHUNK = 64
HG_LEVELS = 6
FFN_SPLIT = 2
VMEM_LIMIT_BYTES = 56 * 1024 * 1024

(V_S5_D, V_GLU_B, V_SC_W0, V_SC_W1, V_SC_W2, V_HG_LOGLB, V_HG_LOG1MLB, V_HG_1MLB, V_HG_GNORM,
 V_LRU_W0, V_LRU_W1, V_LRU_W2, V_LRU_W3, V_LRU_CB, V_LRU_CA, V_LRU_BA, V_LRU_BX) = range(17)
N_VEC = 24


def _sigmoid(x):
    return 0.5 + 0.5 * jnp.tanh(0.5 * x)


def _silu(x):
    h = 0.5 * x
    return h + h * jnp.tanh(h)


def _gelu_tanh(x):
    c = math.sqrt(2.0 / math.pi)
    return x * (0.5 * (1.0 + jnp.tanh(c * (x + 0.044715 * (x * x * x)))))


def _dot(a, b):
    return jnp.dot(a, b, preferred_element_type=F32)


def _dot_nt(a, b):
    return lax.dot_general(a, b, (((1,), (1,)), ((), ())), preferred_element_type=F32)


def _dot_tn(a, b):
    return lax.dot_general(a, b, (((0,), (0,)), ((), ())), preferred_element_type=F32)


def _s5_mixer(segin_ref, vec_ref, bmat_ref, cmat_ref, lam_ref, ptab_ref, glu_w_ref,
              uperm_ref, bu_ref, xbf_ref, e_ref, cin_ref, state_ref, segout_ref):
    n = S5_W

    def gather(r, c):
        rows = pl.ds(pl.multiple_of(r * SUBLANES, SUBLANES), SUBLANES)
        for j in range(W_GROUP // LANES):
            uperm_ref[rows, j * LANES:(j + 1) * LANES] = segin_ref[j, pl.ds(r, SUBLANES, stride=SEG), :]
        return c
    lax.fori_loop(0, SEG, gather, 0)

    bu_ref[...] = _dot(uperm_ref[...].astype(BF16), bmat_ref[...])

    lam_r = jnp.broadcast_to(lam_ref[0:1, 0:n], (SUBLANES, n))
    lam_i = jnp.broadcast_to(lam_ref[0:1, n:2 * n], (SUBLANES, n))

    def scan(r, carry):
        xr, xi = carry
        rows = pl.ds(pl.multiple_of(r * SUBLANES, SUBLANES), SUBLANES)
        nr = lam_r * xr - lam_i * xi + bu_ref[rows, 0:n]
        ni = lam_r * xi + lam_i * xr + bu_ref[rows, n:2 * n]
        bu_ref[rows, 0:n] = nr
        bu_ref[rows, n:2 * n] = ni
        return nr, ni
    zero = jnp.zeros((SUBLANES, n), F32)
    er, ei = lax.fori_loop(0, SEG, scan, (zero, zero))
    e_ref[:, 0:n] = er
    e_ref[:, n:2 * n] = ei

    pr, pi_ = lam_ref[1:2, 0:n], lam_ref[1:2, n:2 * n]
    cr, ci = state_ref[0:1, 0:n], state_ref[0:1, n:2 * n]
    for s in range(SUBLANES):
        cin_ref[s:s + 1, 0:n] = cr
        cin_ref[s:s + 1, n:2 * n] = ci
        sr, si = e_ref[s:s + 1, 0:n], e_ref[s:s + 1, n:2 * n]
        cr, ci = sr + (pr * cr - pi_ * ci), si + (pr * ci + pi_ * cr)
    state_ref[0:1, 0:n] = cr
    state_ref[0:1, n:2 * n] = ci

    cin_r = cin_ref[:, 0:n]
    cin_i = cin_ref[:, n:2 * n]

    def fix(rr, c):
        parts_r, parts_i = [], []
        for j in range(2):
            r = rr * 2 + j
            rows = pl.ds(pl.multiple_of(r * SUBLANES, SUBLANES), SUBLANES)
            qr = jnp.broadcast_to(ptab_ref[pl.ds(r, 1), 0:n], (SUBLANES, n))
            qi = jnp.broadcast_to(ptab_ref[pl.ds(r, 1), n:2 * n], (SUBLANES, n))
            parts_r.append(bu_ref[rows, 0:n] + (qr * cin_r - qi * cin_i))
            parts_i.append(bu_ref[rows, n:2 * n] + (qr * cin_i + qi * cin_r))
        rows2 = pl.ds(pl.multiple_of(rr * 2 * SUBLANES, 2 * SUBLANES), 2 * SUBLANES)
        xbf_ref[rows2, 0:n] = jnp.concatenate(parts_r, axis=0).astype(BF16)
        xbf_ref[rows2, n:2 * n] = jnp.concatenate(parts_i, axis=0).astype(BF16)
        return c
    lax.fori_loop(0, SEG // 2, fix, 0)

    y = _dot(xbf_ref[...], cmat_ref[...])
    y = y + vec_ref[V_S5_D:V_S5_D + 1, :] * uperm_ref[...]
    g = _gelu_tanh(y)
    gate = _sigmoid(_dot(g.astype(BF16), glu_w_ref[...]) + vec_ref[V_GLU_B:V_GLU_B + 1, :])
    uperm_ref[...] = g * gate

    def scatter(r, c):
        rows = pl.ds(pl.multiple_of(r * SUBLANES, SUBLANES), SUBLANES)
        for j in range(W_GROUP // LANES):
            segout_ref[j, pl.ds(r, SUBLANES, stride=SEG), :] = uperm_ref[rows, j * LANES:(j + 1) * LANES]
        return c
    lax.fori_loop(0, SEG, scatter, 0)


def _short_conv_mixer(proj_ref, vec_ref, buf_ref, out_ref):
    tb = TIME_BLOCK
    h = proj_ref[:, W_GROUP:2 * W_GROUP]
    gb = proj_ref[:, 2 * W_GROUP:3 * W_GROUP]
    gc = proj_ref[:, 3 * W_GROUP:4 * W_GROUP]
    buf_ref[SUBLANES:SUBLANES + tb, :] = gc * h
    w0 = vec_ref[V_SC_W0:V_SC_W0 + 1, :]
    w1 = vec_ref[V_SC_W1:V_SC_W1 + 1, :]
    w2 = vec_ref[V_SC_W2:V_SC_W2 + 1, :]
    conv = (w0 * buf_ref[SUBLANES - 2:SUBLANES - 2 + tb, :]
            + w1 * buf_ref[SUBLANES - 1:SUBLANES - 1 + tb, :]
            + w2 * buf_ref[SUBLANES:SUBLANES + tb, :])
    out_ref[:, W_GROUP:2 * W_GROUP] = (gb * conv).astype(BF16)
    buf_ref[0:SUBLANES, :] = buf_ref[tb:tb + SUBLANES, :]


def _sibling(x, n):
    c, w = x.shape
    if n < SUBLANES:
        x3 = x.reshape(c // SUBLANES, SUBLANES, w)
        fwd = pltpu.roll(x3, n, 1)
        if 2 * n == SUBLANES:
            return fwd.reshape(c, w)
        bwd = pltpu.roll(x3, SUBLANES - n, 1)
        row = lax.broadcasted_iota(jnp.int32, x3.shape, 1)
        return jnp.where((row & n) != 0, fwd, bwd).reshape(c, w)
    x4 = x.reshape(c // (2 * n), 2, n, w)
    return jnp.concatenate([x4[:, 1:2], x4[:, 0:1]], axis=1).reshape(c, w)


def _hgrn2_mixer(proj_ref, vec_ref, headmask_ref, bdmask_ref, ones_ref, lv_ref, qt_ref, kt_ref, cin_ref,
                 state_ref, out_ref):
    c = HG_CHUNK
    loglb = vec_ref[V_HG_LOGLB:V_HG_LOGLB + 1, :]
    log1mlb = vec_ref[V_HG_LOG1MLB:V_HG_LOG1MLB + 1, :]
    onemlb = vec_ref[V_HG_1MLB:V_HG_1MLB + 1, :]
    gnorm = vec_ref[V_HG_GNORM:V_HG_GNORM + 1, :]

    def stack_heads(x_bf):
        return jnp.concatenate([x_bf] * HG_HEADS, axis=0) * headmask_ref[...]

    q_in = proj_ref[:, 4 * W_GROUP:5 * W_GROUP]
    z = proj_ref[:, 5 * W_GROUP:6 * W_GROUP]
    q = _silu(q_in) * (HG_HEAD_DIM ** -0.5)
    log_sig = jnp.minimum(z, 0.0) - jnp.log(1.0 + jnp.exp(-jnp.abs(z)))
    cpl = log1mlb + log_sig
    g = jnp.maximum(loglb, cpl) + jnp.log(1.0 + jnp.exp(-jnp.abs(loglb - cpl)))
    k = onemlb * _sigmoid(-z)
    qt_ref[HG_LEVELS + 1] = q.astype(BF16)
    kt_ref[HG_LEVELS + 1] = k.astype(BF16)
    row = lax.broadcasted_iota(jnp.int32, (TIME_BLOCK, W_GROUP), 0)
    cin = g
    sfx = jnp.zeros_like(g)
    for level in range(HG_LEVELS):
        n = 1 << level
        qt_ref[level] = (q * jnp.exp(cin)).astype(BF16)
        kt_ref[level] = (k * jnp.exp(sfx)).astype(BF16)
        sib = _sibling(cin + sfx, n)
        right = (row & n) != 0
        cin = cin + jnp.where(right, sib, 0.0)
        sfx = sfx + jnp.where(right, 0.0, sib)
    qt_ref[HG_LEVELS] = (q * jnp.exp(cin)).astype(BF16)
    kt_ref[HG_LEVELS] = (k * jnp.exp(sfx)).astype(BF16)
    cin_ref[...] = cin

    lv = lv_ref[...]
    for ci in range(TIME_BLOCK // c):
        rows = slice(ci * c, (ci + 1) * c)
        v = proj_ref[rows, 6 * W_GROUP:7 * W_GROUP].astype(BF16)
        g_in = proj_ref[rows, 7 * W_GROUP:8 * W_GROUP]
        scores = jnp.where(lv == HG_LEVELS,
                           _dot_nt(qt_ref[HG_LEVELS + 1, rows, :], stack_heads(kt_ref[HG_LEVELS + 1, rows, :])),
                           0.0)
        for level in range(HG_LEVELS):
            s_n = _dot_nt(qt_ref[level, rows, :], stack_heads(kt_ref[level, rows, :]))
            scores = jnp.where(lv == level, s_n, scores)

        st = state_ref[...]
        inter = _dot_nt(qt_ref[HG_LEVELS, rows, :], st.astype(BF16))
        intra = _dot(scores.astype(BF16), stack_heads(v))
        o = inter + intra

        decay = jnp.exp(cin_ref[(ci + 1) * c - 1:(ci + 1) * c, :])
        upd = _dot_tn(v, kt_ref[HG_LEVELS, rows, :])
        state_ref[...] = st * decay + upd * bdmask_ref[...]

        ms = _dot((o * o).astype(BF16), ones_ref[...])
        o = o * lax.rsqrt(ms + EPS) * gnorm
        out_ref[rows, 2 * W_GROUP:3 * W_GROUP] = (o * _silu(g_in)).astype(BF16)


def _rglru_mixer(proj_ref, segin_ref, vec_ref, wab_ref, xbuf_ref, a_ref, b_ref, h_ref, p_ref,
                 e_ref, cin_ref, state_ref, segout_ref):
    tb = TIME_BLOCK
    xbuf_ref[SUBLANES:SUBLANES + tb, :] = proj_ref[:, 8 * W_GROUP:9 * W_GROUP]
    xc = vec_ref[V_LRU_CB:V_LRU_CB + 1, :] + (
        vec_ref[V_LRU_W0:V_LRU_W0 + 1, :] * xbuf_ref[SUBLANES - 3:SUBLANES - 3 + tb, :]
        + vec_ref[V_LRU_W1:V_LRU_W1 + 1, :] * xbuf_ref[SUBLANES - 2:SUBLANES - 2 + tb, :]
        + vec_ref[V_LRU_W2:V_LRU_W2 + 1, :] * xbuf_ref[SUBLANES - 1:SUBLANES - 1 + tb, :]
        + vec_ref[V_LRU_W3:V_LRU_W3 + 1, :] * xbuf_ref[SUBLANES:SUBLANES + tb, :])
    xbuf_ref[0:SUBLANES, :] = xbuf_ref[tb:tb + SUBLANES, :]

    gates = _dot(xc.astype(BF16), wab_ref[...])
    gate_a = _sigmoid(gates[:, 0:W_GROUP] + vec_ref[V_LRU_BA:V_LRU_BA + 1, :])
    gate_x = _sigmoid(gates[:, W_GROUP:2 * W_GROUP] + vec_ref[V_LRU_BX:V_LRU_BX + 1, :])
    log_a = vec_ref[V_LRU_CA:V_LRU_CA + 1, :] * gate_a
    a = jnp.exp(log_a)
    b = xc * gate_x * jnp.sqrt(-jnp.tanh(log_a) * (a * a + 1.0))
    nslab = W_GROUP // LANES
    for j in range(nslab):
        a_ref[j] = a[:, j * LANES:(j + 1) * LANES]
        b_ref[j] = b[:, j * LANES:(j + 1) * LANES]

    def seg_tile(ref, first, r):
        return jnp.concatenate(
            [ref[first + j, pl.ds(r, SUBLANES, stride=SEG), :] for j in range(nslab)], axis=1)

    def scan(r, carry):
        h, p = carry
        a = seg_tile(a_ref, 0, r)
        h = a * h + seg_tile(b_ref, 0, r)
        p = a * p
        rows = pl.ds(pl.multiple_of(r * SUBLANES, SUBLANES), SUBLANES)
        h_ref[rows, :] = h
        p_ref[rows, :] = p
        return h, p
    e, ptot = lax.fori_loop(0, SEG, scan, (jnp.zeros((SUBLANES, W_GROUP), F32),
                                           jnp.ones((SUBLANES, W_GROUP), F32)))
    e_ref[0:SUBLANES, :] = e
    e_ref[SUBLANES:2 * SUBLANES, :] = ptot
    cur = state_ref[0:1, :]
    for s in range(SUBLANES):
        cin_ref[s:s + 1, :] = cur
        cur = e_ref[s:s + 1, :] + e_ref[SUBLANES + s:SUBLANES + s + 1, :] * cur
    state_ref[0:1, :] = cur
    cin = cin_ref[...]

    def fix(r, c):
        rows = pl.ds(pl.multiple_of(r * SUBLANES, SUBLANES), SUBLANES)
        h = h_ref[rows, :] + p_ref[rows, :] * cin
        res = h * _gelu_tanh(seg_tile(segin_ref, nslab, r))
        for j in range(nslab):
            segout_ref[nslab + j, pl.ds(r, SUBLANES, stride=SEG), :] = res[:, j * LANES:(j + 1) * LANES]
        return c
    lax.fori_loop(0, SEG, fix, 0)


def _mixer_kernel(h_ref, w_in_ref, vec_ref, bmat_ref, cmat_ref, lam_ref, ptab_ref, glu_w_ref,
                  wab_ref, headmask_ref, bdmask_ref, ones_ref, lv_ref,
                  out_ref,
                  proj_ref, segin_ref, segout_ref, uperm_ref, bu_ref, xbf_ref, s5_e_ref, s5_cin_ref, s5_state_ref,
                  sc_buf_ref, hg_qt_ref, hg_kt_ref, hg_cin_ref, hg_state_ref,
                  lru_xbuf_ref, lru_a_ref, lru_b_ref, lru_h_ref, lru_p_ref,
                  lru_e_ref, lru_cin_ref, lru_state_ref):
    @pl.when(pl.program_id(1) == 0)
    def _():
        s5_state_ref[...] = jnp.zeros_like(s5_state_ref)
        sc_buf_ref[0:SUBLANES, :] = jnp.zeros((SUBLANES, W_GROUP), F32)
        hg_state_ref[...] = jnp.zeros_like(hg_state_ref)
        lru_xbuf_ref[0:SUBLANES, :] = jnp.zeros((SUBLANES, W_GROUP), F32)
        lru_state_ref[...] = jnp.zeros_like(lru_state_ref)

    proj_ref[...] = _dot(h_ref[...].astype(BF16), w_in_ref[...])
    nslab = W_GROUP // LANES
    for j in range(nslab):
        segin_ref[j] = proj_ref[:, j * LANES:(j + 1) * LANES]
        segin_ref[nslab + j] = proj_ref[:, 9 * W_GROUP + j * LANES:9 * W_GROUP + (j + 1) * LANES]
    _s5_mixer(segin_ref, vec_ref, bmat_ref, cmat_ref, lam_ref, ptab_ref, glu_w_ref,
              uperm_ref, bu_ref, xbf_ref, s5_e_ref, s5_cin_ref, s5_state_ref, segout_ref)
    _short_conv_mixer(proj_ref, vec_ref, sc_buf_ref, out_ref)
    _hgrn2_mixer(proj_ref, vec_ref, headmask_ref, bdmask_ref, ones_ref, lv_ref, hg_qt_ref, hg_kt_ref,
                 hg_cin_ref, hg_state_ref, out_ref)
    _rglru_mixer(proj_ref, segin_ref, vec_ref, wab_ref, lru_xbuf_ref, lru_a_ref, lru_b_ref, lru_h_ref,
                 lru_p_ref, lru_e_ref, lru_cin_ref, lru_state_ref, segout_ref)
    for j in range(nslab):
        out_ref[:, j * LANES:(j + 1) * LANES] = segout_ref[j].astype(BF16)
        out_ref[:, 3 * W_GROUP + j * LANES:3 * W_GROUP + (j + 1) * LANES] = segout_ref[nslab + j].astype(BF16)


def _const_spec(shape):
    return pl.BlockSpec(shape, lambda *_: (0,) * len(shape), pipeline_mode=pl.Buffered(1))


def _mixer_call(h, w_in, vec, bmat, cmat, lam, ptab, glu_w, wab, headmask, bdmask, ones, lv):
    bsz, lp, _ = h.shape
    tb = TIME_BLOCK
    consts = (w_in, vec, bmat, cmat, lam, ptab, glu_w, wab, headmask, bdmask, ones, lv)
    scratch = [
        pltpu.VMEM((tb, N_IN), F32),
        pltpu.VMEM((2 * W_GROUP // LANES, tb, LANES), F32),
        pltpu.VMEM((2 * W_GROUP // LANES, tb, LANES), F32),
        pltpu.VMEM((tb, W_GROUP), F32),
        pltpu.VMEM((tb, 2 * S5_W), F32),
        pltpu.VMEM((tb, 2 * S5_W), BF16),
        pltpu.VMEM((SUBLANES, 2 * S5_W), F32),
        pltpu.VMEM((SUBLANES, 2 * S5_W), F32),
        pltpu.VMEM((SUBLANES, 2 * S5_W), F32),
        pltpu.VMEM((tb + SUBLANES, W_GROUP), F32),
        pltpu.VMEM((HG_LEVELS + 2, tb, W_GROUP), BF16),
        pltpu.VMEM((HG_LEVELS + 2, tb, W_GROUP), BF16),
        pltpu.VMEM((tb, W_GROUP), F32),
        pltpu.VMEM((W_GROUP, W_GROUP), F32),
        pltpu.VMEM((tb + SUBLANES, W_GROUP), F32),
        pltpu.VMEM((W_GROUP // LANES, tb, LANES), F32),
        pltpu.VMEM((W_GROUP // LANES, tb, LANES), F32),
        pltpu.VMEM((tb, W_GROUP), F32),
        pltpu.VMEM((tb, W_GROUP), F32),
        pltpu.VMEM((2 * SUBLANES, W_GROUP), F32),
        pltpu.VMEM((SUBLANES, W_GROUP), F32),
        pltpu.VMEM((SUBLANES, W_GROUP), F32),
    ]
    return pl.pallas_call(
        _mixer_kernel,
        grid=(bsz, lp // tb),
        in_specs=[pl.BlockSpec((None, tb, D_MODEL), lambda b, t: (b, t, 0))]
                 + [_const_spec(c.shape) for c in consts],
        out_specs=pl.BlockSpec((None, tb, 4 * W_GROUP), lambda b, t: (b, t, 0)),
        out_shape=jax.ShapeDtypeStruct((bsz, lp, 4 * W_GROUP), BF16),
        scratch_shapes=scratch,
        compiler_params=pltpu.CompilerParams(
            dimension_semantics=("arbitrary", "arbitrary"), vmem_limit_bytes=VMEM_LIMIT_BYTES),
    )(h, *consts)


def _layer_norm(x, g, b):
    mu = jnp.mean(x, axis=-1, keepdims=True)
    xc = x - mu
    var = jnp.mean(xc * xc, axis=-1, keepdims=True)
    return xc * lax.rsqrt(var + EPS) * g + b


def _ffn_kernel(mix_ref, h_ref, w_out_ref, ln_ref, w_ffn_in_ref, w_ffn_out_ref, out_ref):
    a = ALPHA * h_ref[...] + _dot(mix_ref[...], w_out_ref[...])
    hn = _layer_norm(a, ln_ref[0:1, :], ln_ref[1:2, :])
    hb = hn.astype(BF16)
    part = D_FF // FFN_SPLIT
    ffn = None
    for j in range(FFN_SPLIT):
        gate = _dot(hb, w_ffn_in_ref[:, j * part:(j + 1) * part])
        up = _dot(hb, w_ffn_in_ref[:, D_FF + j * part:D_FF + (j + 1) * part])
        act = (_silu(gate) * up).astype(BF16)
        contrib = _dot(act, w_ffn_out_ref[j * part:(j + 1) * part, :])
        ffn = contrib if ffn is None else ffn + contrib
    out_ref[...] = _layer_norm(ALPHA * hn + ffn, ln_ref[2:3, :], ln_ref[3:4, :])


def _ffn_call(mix, h, w_out, ln, w_ffn_in, w_ffn_out):
    bsz, lp, _ = h.shape
    tb = TIME_BLOCK
    consts = (w_out, ln, w_ffn_in, w_ffn_out)
    return pl.pallas_call(
        _ffn_kernel,
        grid=(bsz, lp // tb),
        in_specs=[pl.BlockSpec((None, tb, 4 * W_GROUP), lambda b, t: (b, t, 0)),
                  pl.BlockSpec((None, tb, D_MODEL), lambda b, t: (b, t, 0))]
                 + [_const_spec(c.shape) for c in consts],
        out_specs=pl.BlockSpec((None, tb, D_MODEL), lambda b, t: (b, t, 0)),
        out_shape=jax.ShapeDtypeStruct((bsz, lp, D_MODEL), F32),
        compiler_params=pltpu.CompilerParams(
            dimension_semantics=("arbitrary", "arbitrary"), vmem_limit_bytes=VMEM_LIMIT_BYTES),
    )(mix, h, *consts)


def _block_diag(blocks):
    n, a, b = blocks.shape
    eye = jnp.eye(n, dtype=blocks.dtype)
    return jnp.einsum('nab,nm->namb', blocks, eye).reshape(n * a, n * b)


def _hgrn2_level_map():
    t = jnp.arange(HG_CHUNK, dtype=jnp.int32)[:, None]
    s = jnp.arange(HG_CHUNK, dtype=jnp.int32)[None, :]
    x = t ^ s
    lv = jnp.full((HG_CHUNK, HG_CHUNK), -1, jnp.int32)
    for level in range(HG_LEVELS):
        n = 1 << level
        lv = jnp.where((x >= n) & (x < 2 * n) & (t > s), level, lv)
    lv = jnp.where(t == s, HG_LEVELS, lv)
    return jnp.tile(lv, (1, HG_HEADS))


def _prepare_layer(l, lb, w_in, s5_lam_re, s5_lam_im, s5_b_re, s5_b_im, s5_c_re, s5_c_im, s5_d,
                   s5_log_dt, s5_glu_w, s5_glu_b, sc_conv_w, hg_gnorm, lru_conv_w, lru_conv_b,
                   lru_wa, lru_ba, lru_wx, lru_bx, lru_a_param):
    lam_r, lam_i = s5_lam_re[l].astype(F32), s5_lam_im[l].astype(F32)
    dt = jnp.exp(s5_log_dt[l].astype(F32))[:, None]
    arg_r, arg_i = lam_r * dt, lam_i * dt

    def lam_bar_pow(k):
        mag = jnp.exp(arg_r * k)
        return mag * jnp.cos(arg_i * k), mag * jnp.sin(arg_i * k)
    bar_r, bar_i = lam_bar_pow(1.0)
    num_r, num_i = bar_r - 1.0, bar_i
    den = lam_r * lam_r + lam_i * lam_i
    coef_r = ((num_r * lam_r + num_i * lam_i) / den)[..., None]
    coef_i = ((num_i * lam_r - num_r * lam_i) / den)[..., None]
    b_r, b_i = s5_b_re[l].astype(F32), s5_b_im[l].astype(F32)
    bbar_r = jnp.transpose(coef_r * b_r - coef_i * b_i, (0, 2, 1))
    bbar_i = jnp.transpose(coef_r * b_i + coef_i * b_r, (0, 2, 1))
    bmat = jnp.concatenate([_block_diag(bbar_r), _block_diag(bbar_i)], axis=1)
    c_r = jnp.transpose(s5_c_re[l].astype(F32), (0, 2, 1))
    c_i = jnp.transpose(s5_c_im[l].astype(F32), (0, 2, 1))
    cmat = jnp.concatenate([_block_diag(c_r), -_block_diag(c_i)], axis=0)

    seg_r, seg_i = lam_bar_pow(float(SEG))
    lam_rows = jnp.stack([jnp.concatenate([bar_r.reshape(-1), bar_i.reshape(-1)]),
                          jnp.concatenate([seg_r.reshape(-1), seg_i.reshape(-1)])])
    steps = jnp.arange(1, SEG + 1, dtype=F32)[:, None, None]
    pw_r, pw_i = lam_bar_pow(steps)
    ptab = jnp.concatenate([pw_r.reshape(SEG, -1), pw_i.reshape(SEG, -1)], axis=1)

    one_m_lb = 1.0 - lb
    rows = [None] * N_VEC
    rows[V_S5_D] = s5_d[l]
    rows[V_GLU_B] = s5_glu_b[l]
    rows[V_SC_W0], rows[V_SC_W1], rows[V_SC_W2] = sc_conv_w[l, 0], sc_conv_w[l, 1], sc_conv_w[l, 2]
    rows[V_HG_LOGLB] = jnp.maximum(jnp.log(lb), -1e30)
    rows[V_HG_LOG1MLB] = jnp.log1p(-lb)
    rows[V_HG_1MLB] = one_m_lb
    rows[V_HG_GNORM] = hg_gnorm[l]
    for i, v in enumerate((V_LRU_W0, V_LRU_W1, V_LRU_W2, V_LRU_W3)):
        rows[v] = lru_conv_w[l, i]
    rows[V_LRU_CB] = lru_conv_b[l]
    rows[V_LRU_CA] = -LRU_C * jax.nn.softplus(-lru_a_param[l].astype(F32))
    rows[V_LRU_BA] = lru_ba[l]
    rows[V_LRU_BX] = lru_bx[l]
    zero = jnp.zeros((W_GROUP,), F32)
    vec = jnp.stack([zero if r is None else r.astype(F32) for r in rows])
    wab = jnp.concatenate([_block_diag(lru_wa[l].astype(F32)), _block_diag(lru_wx[l].astype(F32))], axis=1)
    return (w_in[l].astype(BF16), vec, bmat.astype(BF16), cmat.astype(BF16), lam_rows, ptab,
            s5_glu_w[l].astype(BF16), wab.astype(BF16))


def kernel(x, meta_tokens, hg_lb_raw, w_in, w_out, s5_lam_re, s5_lam_im, s5_b_re, s5_b_im, s5_c_re, s5_c_im, s5_d, s5_log_dt, s5_glu_w, s5_glu_b, sc_conv_w, hg_gnorm, lru_conv_w, lru_conv_b, lru_wa, lru_ba, lru_wx, lru_bx, lru_a_param, ln1_g, ln1_b, w_ffn_in, w_ffn_out, ln2_g, ln2_b):
    bsz, seq, d = x.shape
    assert d == D_MODEL
    n_tok = N_META + seq
    lp = -(-n_tok // TIME_BLOCK) * TIME_BLOCK
    meta = jnp.broadcast_to(meta_tokens.astype(x.dtype)[None], (bsz, N_META, d))
    h = jnp.concatenate([meta, x, jnp.zeros((bsz, lp - n_tok, d), x.dtype)], axis=1)

    lb_all = jnp.cumsum(jax.nn.softmax(hg_lb_raw.astype(F32), axis=0), axis=0)
    lb_all = lb_all - lb_all[0:1]

    head = jnp.arange(W_GROUP) // HG_HEAD_DIM
    same_head = head[:, None] == head[None, :]
    bdmask = same_head.astype(F32)
    headmask = same_head.astype(BF16)
    ones = (same_head.astype(F32) / HG_HEAD_DIM).astype(BF16)
    lv = _hgrn2_level_map()

    for l in range(DEPTH):
        prep = _prepare_layer(l, lb_all[l], w_in, s5_lam_re, s5_lam_im, s5_b_re, s5_b_im, s5_c_re, s5_c_im,
                              s5_d, s5_log_dt, s5_glu_w, s5_glu_b, sc_conv_w, hg_gnorm, lru_conv_w,
                              lru_conv_b, lru_wa, lru_ba, lru_wx, lru_bx, lru_a_param)
        mix = _mixer_call(h, *prep, headmask, bdmask, ones, lv)
        ln = jnp.stack([ln1_g[l], ln1_b[l], ln2_g[l], ln2_b[l]]).astype(F32)
        h = _ffn_call(mix, h, w_out[l].astype(BF16), ln, w_ffn_in[l].astype(BF16), w_ffn_out[l].astype(BF16))
    return h[:, N_META:n_tok]
```

```python
import functools
import math

import jax
import jax.numpy as jnp
from jax import lax
from jax.experimental import pallas as pl
from jax.experimental.pallas import tpu as pltpu

F32 = jnp.float32
BF16 = jnp.bfloat16

D_MODEL = 1024
DEPTH = 2
N_META = 16
W_GROUP = 256
N_IN = 10 * W_GROUP
S5_GROUP = 16
S5_NGROUPS = 16
S5_STATE = 64
S5_W = S5_NGROUPS * S5_STATE
HG_HEADS = 4
HG_HEAD_DIM = 64
LRU_HEADS = 4
LRU_C = 8.0
D_FF = 2816
ALPHA = (2 * DEPTH) ** 0.25
EPS = 1e-5

SUBLANES = 8
LANES = 128
NSLAB = W_GROUP // LANES
TIME_BLOCK = 320
SEG = TIME_BLOCK // SUBLANES
HG_CHUNK = 64
HG_LEVELS = 6
FFN_SPLIT = 2
VMEM_LIMIT_BYTES = 60 * 1024 * 1024

(V_S5_D, V_GLU_B, V_SC_W0, V_SC_W1, V_SC_W2, V_HG_LOGLB, V_HG_LOG1MLB, V_HG_1MLB, V_HG_GNORM,
 V_LRU_W0, V_LRU_W1, V_LRU_W2, V_LRU_W3, V_LRU_CB, V_LRU_CA, V_LRU_BA, V_LRU_BX) = range(17)
N_VEC = 24


def _sigmoid(x):
    return 0.5 + 0.5 * jnp.tanh(0.5 * x)


def _silu(x):
    h = 0.5 * x
    return h + h * jnp.tanh(h)


def _gelu_tanh(x):
    c = math.sqrt(2.0 / math.pi)
    return x * (0.5 * (1.0 + jnp.tanh(c * (x + 0.044715 * (x * x * x)))))


def _dot(a, b):
    return jnp.dot(a, b, preferred_element_type=F32)


def _dot_nt(a, b):
    return lax.dot_general(a, b, (((1,), (1,)), ((), ())), preferred_element_type=F32)


def _dot_tn(a, b):
    return lax.dot_general(a, b, (((0,), (0,)), ((), ())), preferred_element_type=F32)


def _tile(r):
    return slice(r * SUBLANES, (r + 1) * SUBLANES)


def _slab(j, first=0):
    return slice(first + j * LANES, first + (j + 1) * LANES)


def _seg_rows(r):
    return pl.ds(r, SUBLANES, stride=SEG)


def _s5_mixer(segin_ref, vec_ref, bmat_ref, cmat_ref, lam_ref, ptab_ref, glu_w_ref,
              uperm_ref, bu_ref, xbf_ref, e_ref, cin_ref, state_ref, segout_ref):
    n = S5_W

    for r in range(SEG):
        for j in range(NSLAB):
            uperm_ref[_tile(r), _slab(j)] = segin_ref[j, _seg_rows(r), :]

    bu_ref[...] = _dot(uperm_ref[...].astype(BF16), bmat_ref[...])

    lam_r = jnp.broadcast_to(lam_ref[0:1, 0:n], (SUBLANES, n))
    lam_i = jnp.broadcast_to(lam_ref[0:1, n:2 * n], (SUBLANES, n))

    xr = xi = None
    for r in range(SEG):
        br, bi = bu_ref[_tile(r), 0:n], bu_ref[_tile(r), n:2 * n]
        if r > 0:
            xr, xi = lam_r * xr - lam_i * xi + br, lam_r * xi + lam_i * xr + bi
            bu_ref[_tile(r), 0:n] = xr
            bu_ref[_tile(r), n:2 * n] = xi
        else:
            xr, xi = br, bi
    e_ref[:, 0:n] = xr
    e_ref[:, n:2 * n] = xi

    pr, pi_ = lam_ref[1:2, 0:n], lam_ref[1:2, n:2 * n]
    cr, ci = state_ref[0:1, 0:n], state_ref[0:1, n:2 * n]
    for s in range(SUBLANES):
        cin_ref[s:s + 1, 0:n] = cr
        cin_ref[s:s + 1, n:2 * n] = ci
        sr, si = e_ref[s:s + 1, 0:n], e_ref[s:s + 1, n:2 * n]
        cr, ci = sr + (pr * cr - pi_ * ci), si + (pr * ci + pi_ * cr)
    state_ref[0:1, 0:n] = cr
    state_ref[0:1, n:2 * n] = ci

    cin_r = cin_ref[:, 0:n]
    cin_i = cin_ref[:, n:2 * n]

    for rr in range(SEG // 2):
        parts_r, parts_i = [], []
        for r in (2 * rr, 2 * rr + 1):
            qr = jnp.broadcast_to(ptab_ref[r:r + 1, 0:n], (SUBLANES, n))
            qi = jnp.broadcast_to(ptab_ref[r:r + 1, n:2 * n], (SUBLANES, n))
            parts_r.append(bu_ref[_tile(r), 0:n] + (qr * cin_r - qi * cin_i))
            parts_i.append(bu_ref[_tile(r), n:2 * n] + (qr * cin_i + qi * cin_r))
        rows2 = slice(rr * 2 * SUBLANES, (rr + 1) * 2 * SUBLANES)
        xbf_ref[rows2, 0:n] = jnp.concatenate(parts_r, axis=0).astype(BF16)
        xbf_ref[rows2, n:2 * n] = jnp.concatenate(parts_i, axis=0).astype(BF16)

    y = _dot(xbf_ref[...], cmat_ref[...])
    y = y + vec_ref[V_S5_D:V_S5_D + 1, :] * uperm_ref[...]
    g = _gelu_tanh(y)
    gate = _sigmoid(_dot(g.astype(BF16), glu_w_ref[...]) + vec_ref[V_GLU_B:V_GLU_B + 1, :])
    uperm_ref[...] = g * gate

    for r in range(SEG):
        for j in range(NSLAB):
            segout_ref[j, _seg_rows(r), :] = uperm_ref[_tile(r), _slab(j)]


def _short_conv_mixer(p, vec_ref, buf_ref, mix_ref):
    tb = TIME_BLOCK
    h, gb, gc = p[:, 0:W_GROUP], p[:, W_GROUP:2 * W_GROUP], p[:, 2 * W_GROUP:3 * W_GROUP]
    buf_ref[SUBLANES:SUBLANES + tb, :] = gc * h
    w0 = vec_ref[V_SC_W0:V_SC_W0 + 1, :]
    w1 = vec_ref[V_SC_W1:V_SC_W1 + 1, :]
    w2 = vec_ref[V_SC_W2:V_SC_W2 + 1, :]
    conv = (w0 * buf_ref[SUBLANES - 2:SUBLANES - 2 + tb, :]
            + w1 * buf_ref[SUBLANES - 1:SUBLANES - 1 + tb, :]
            + w2 * buf_ref[SUBLANES:SUBLANES + tb, :])
    mix_ref[:, W_GROUP:2 * W_GROUP] = (gb * conv).astype(BF16)
    buf_ref[0:SUBLANES, :] = buf_ref[tb:tb + SUBLANES, :]


def _sibling(x, n):
    c, w = x.shape
    if n < SUBLANES:
        x3 = x.reshape(c // SUBLANES, SUBLANES, w)
        fwd = pltpu.roll(x3, n, 1)
        if 2 * n == SUBLANES:
            return fwd.reshape(c, w)
        bwd = pltpu.roll(x3, SUBLANES - n, 1)
        row = lax.broadcasted_iota(jnp.int32, x3.shape, 1)
        return jnp.where((row & n) != 0, fwd, bwd).reshape(c, w)
    x4 = x.reshape(c // (2 * n), 2, n, w)
    return jnp.concatenate([x4[:, 1:2], x4[:, 0:1]], axis=1).reshape(c, w)


def _hgrn2_mixer(p, vec_ref, headmask_ref, bdmask_ref, ones_ref, lv_ref, qt_ref, kt_ref, cin_ref,
                 state_ref, mix_ref):
    c = HG_CHUNK
    loglb = vec_ref[V_HG_LOGLB:V_HG_LOGLB + 1, :]
    log1mlb = vec_ref[V_HG_LOG1MLB:V_HG_LOG1MLB + 1, :]
    onemlb = vec_ref[V_HG_1MLB:V_HG_1MLB + 1, :]
    gnorm = vec_ref[V_HG_GNORM:V_HG_GNORM + 1, :]

    def stack_heads(x_bf):
        return jnp.concatenate([x_bf] * HG_HEADS, axis=0) * headmask_ref[...]

    q_in = p[:, 0:W_GROUP]
    z = p[:, W_GROUP:2 * W_GROUP]
    q = _silu(q_in) * (HG_HEAD_DIM ** -0.5)
    log_sig = jnp.minimum(z, 0.0) - jnp.log(1.0 + jnp.exp(-jnp.abs(z)))
    cpl = log1mlb + log_sig
    g = jnp.maximum(loglb, cpl) + jnp.log(1.0 + jnp.exp(-jnp.abs(loglb - cpl)))
    k = onemlb * _sigmoid(-z)
    qt_ref[HG_LEVELS + 1] = q.astype(BF16)
    kt_ref[HG_LEVELS + 1] = k.astype(BF16)
    row = lax.broadcasted_iota(jnp.int32, (TIME_BLOCK, W_GROUP), 0)
    cin = g
    sfx = jnp.zeros_like(g)
    for level in range(HG_LEVELS):
        n = 1 << level
        qt_ref[level] = (q * jnp.exp(cin)).astype(BF16)
        kt_ref[level] = (k * jnp.exp(sfx)).astype(BF16)
        sib = _sibling(cin + sfx, n)
        right = (row & n) != 0
        cin = cin + jnp.where(right, sib, 0.0)
        sfx = sfx + jnp.where(right, 0.0, sib)
    qt_ref[HG_LEVELS] = (q * jnp.exp(cin)).astype(BF16)
    kt_ref[HG_LEVELS] = (k * jnp.exp(sfx)).astype(BF16)
    cin_ref[...] = cin

    lv = lv_ref[...]
    for ci in range(TIME_BLOCK // c):
        rows = slice(ci * c, (ci + 1) * c)
        v = p[rows, 2 * W_GROUP:3 * W_GROUP].astype(BF16)
        g_in = p[rows, 3 * W_GROUP:4 * W_GROUP]
        scores = jnp.where(lv == HG_LEVELS,
                           _dot_nt(qt_ref[HG_LEVELS + 1, rows, :], stack_heads(kt_ref[HG_LEVELS + 1, rows, :])),
                           0.0)
        for level in range(HG_LEVELS):
            s_n = _dot_nt(qt_ref[level, rows, :], stack_heads(kt_ref[level, rows, :]))
            scores = jnp.where(lv == level, s_n, scores)

        st = state_ref[...]
        inter = _dot_nt(qt_ref[HG_LEVELS, rows, :], st.astype(BF16))
        intra = _dot(scores.astype(BF16), stack_heads(v))
        o = inter + intra

        decay = jnp.exp(cin_ref[(ci + 1) * c - 1:(ci + 1) * c, :])
        upd = _dot_tn(v, kt_ref[HG_LEVELS, rows, :])
        state_ref[...] = st * decay + upd * bdmask_ref[...]

        ms = _dot((o * o).astype(BF16), ones_ref[...])
        o = o * lax.rsqrt(ms + EPS) * gnorm
        mix_ref[rows, 2 * W_GROUP:3 * W_GROUP] = (o * _silu(g_in)).astype(BF16)


def _rglru_mixer(x, segin_ref, vec_ref, wab_ref, xbuf_ref, a_ref, b_ref, h_ref, p_ref,
                 e_ref, cin_ref, state_ref, segout_ref):
    tb = TIME_BLOCK
    xbuf_ref[SUBLANES:SUBLANES + tb, :] = x
    xc = vec_ref[V_LRU_CB:V_LRU_CB + 1, :] + (
        vec_ref[V_LRU_W0:V_LRU_W0 + 1, :] * xbuf_ref[SUBLANES - 3:SUBLANES - 3 + tb, :]
        + vec_ref[V_LRU_W1:V_LRU_W1 + 1, :] * xbuf_ref[SUBLANES - 2:SUBLANES - 2 + tb, :]
        + vec_ref[V_LRU_W2:V_LRU_W2 + 1, :] * xbuf_ref[SUBLANES - 1:SUBLANES - 1 + tb, :]
        + vec_ref[V_LRU_W3:V_LRU_W3 + 1, :] * xbuf_ref[SUBLANES:SUBLANES + tb, :])
    xbuf_ref[0:SUBLANES, :] = xbuf_ref[tb:tb + SUBLANES, :]

    gates = _dot(xc.astype(BF16), wab_ref[...])
    gate_a = _sigmoid(gates[:, 0:W_GROUP] + vec_ref[V_LRU_BA:V_LRU_BA + 1, :])
    gate_x = _sigmoid(gates[:, W_GROUP:2 * W_GROUP] + vec_ref[V_LRU_BX:V_LRU_BX + 1, :])
    log_a = vec_ref[V_LRU_CA:V_LRU_CA + 1, :] * gate_a
    a = jnp.exp(log_a)
    b = xc * gate_x * jnp.sqrt(-jnp.tanh(log_a) * (a * a + 1.0))
    for j in range(NSLAB):
        a_ref[j] = a[:, _slab(j)]
        b_ref[j] = b[:, _slab(j)]

    def seg_tile(ref, first, r):
        return jnp.concatenate([ref[first + j, _seg_rows(r), :] for j in range(NSLAB)], axis=1)

    h = p = None
    for r in range(SEG):
        a_t, b_t = seg_tile(a_ref, 0, r), seg_tile(b_ref, 0, r)
        h, p = (b_t, a_t) if r == 0 else (a_t * h + b_t, a_t * p)
        h_ref[_tile(r), :] = h
        p_ref[_tile(r), :] = p
    e_ref[0:SUBLANES, :] = h
    e_ref[SUBLANES:2 * SUBLANES, :] = p
    cur = state_ref[0:1, :]
    for s in range(SUBLANES):
        cin_ref[s:s + 1, :] = cur
        cur = e_ref[s:s + 1, :] + e_ref[SUBLANES + s:SUBLANES + s + 1, :] * cur
    state_ref[0:1, :] = cur
    cin = cin_ref[...]

    for r in range(SEG):
        res = (h_ref[_tile(r), :] + p_ref[_tile(r), :] * cin) * _gelu_tanh(seg_tile(segin_ref, NSLAB, r))
        for j in range(NSLAB):
            segout_ref[NSLAB + j, _seg_rows(r), :] = res[:, _slab(j)]


def _layer_norm(x, g, b):
    mu = jnp.mean(x, axis=-1, keepdims=True)
    xc = x - mu
    var = jnp.mean(xc * xc, axis=-1, keepdims=True)
    return xc * lax.rsqrt(var + EPS) * g + b


def _channel_mixer(mix, h, w_out_ref, ln_ref, w_ffn_in_ref, w_ffn_out_ref):
    a = ALPHA * h + _dot(mix, w_out_ref[...])
    hn = _layer_norm(a, ln_ref[0:1, :], ln_ref[1:2, :])
    hb = hn.astype(BF16)
    part = D_FF // FFN_SPLIT
    ffn = None
    for j in range(FFN_SPLIT):
        gate = _dot(hb, w_ffn_in_ref[:, j * part:(j + 1) * part])
        up = _dot(hb, w_ffn_in_ref[:, D_FF + j * part:D_FF + (j + 1) * part])
        act = (_silu(gate) * up).astype(BF16)
        contrib = _dot(act, w_ffn_out_ref[j * part:(j + 1) * part, :])
        ffn = contrib if ffn is None else ffn + contrib
    return _layer_norm(ALPHA * hn + ffn, ln_ref[2:3, :], ln_ref[3:4, :])


def _layer_kernel(nblk,
                  h_ref, w_in_ref, vec_ref, bmat_ref, cmat_ref, lam_ref, ptab_ref, glu_w_ref,
                  wab_ref, headmask_ref, bdmask_ref, ones_ref, lv_ref,
                  w_out_ref, ln_ref, w_ffn_in_ref, w_ffn_out_ref,
                  out_ref,
                  hprev_ref, mix_ref, segin_ref, segout_ref, uperm_ref, bu_ref, xbf_ref,
                  s5_e_ref, s5_cin_ref, s5_state_ref,
                  sc_buf_ref, hg_qt_ref, hg_kt_ref, hg_cin_ref, hg_state_ref,
                  lru_xbuf_ref, lru_a_ref, lru_b_ref, lru_h_ref, lru_p_ref,
                  lru_e_ref, lru_cin_ref, lru_state_ref):
    step = pl.program_id(0)

    @pl.when(step == 0)
    def _():
        hprev_ref[...] = jnp.zeros_like(hprev_ref)
        mix_ref[...] = jnp.zeros_like(mix_ref)

    @pl.when(step % nblk == 0)
    def _():
        s5_state_ref[...] = jnp.zeros_like(s5_state_ref)
        sc_buf_ref[0:SUBLANES, :] = jnp.zeros((SUBLANES, W_GROUP), F32)
        hg_state_ref[...] = jnp.zeros_like(hg_state_ref)
        lru_xbuf_ref[0:SUBLANES, :] = jnp.zeros((SUBLANES, W_GROUP), F32)
        lru_state_ref[...] = jnp.zeros_like(lru_state_ref)

    out_ref[...] = _channel_mixer(mix_ref[...], hprev_ref[...], w_out_ref, ln_ref, w_ffn_in_ref, w_ffn_out_ref)

    h = h_ref[...]
    hprev_ref[...] = h
    hb = h.astype(BF16)
    p_s5 = _dot(hb, w_in_ref[:, 0:W_GROUP])
    p_lru = _dot(hb, w_in_ref[:, 8 * W_GROUP:10 * W_GROUP])
    for j in range(NSLAB):
        segin_ref[j] = p_s5[:, _slab(j)]
        segin_ref[NSLAB + j] = p_lru[:, _slab(j, W_GROUP)]
    _s5_mixer(segin_ref, vec_ref, bmat_ref, cmat_ref, lam_ref, ptab_ref, glu_w_ref,
              uperm_ref, bu_ref, xbf_ref, s5_e_ref, s5_cin_ref, s5_state_ref, segout_ref)
    _short_conv_mixer(_dot(hb, w_in_ref[:, W_GROUP:4 * W_GROUP]), vec_ref, sc_buf_ref, mix_ref)
    _hgrn2_mixer(_dot(hb, w_in_ref[:, 4 * W_GROUP:8 * W_GROUP]), vec_ref, headmask_ref, bdmask_ref, ones_ref,
                 lv_ref, hg_qt_ref, hg_kt_ref, hg_cin_ref, hg_state_ref, mix_ref)
    _rglru_mixer(p_lru[:, 0:W_GROUP], segin_ref, vec_ref, wab_ref, lru_xbuf_ref, lru_a_ref, lru_b_ref,
                 lru_h_ref, lru_p_ref, lru_e_ref, lru_cin_ref, lru_state_ref, segout_ref)
    for j in range(NSLAB):
        mix_ref[:, _slab(j)] = segout_ref[j].astype(BF16)
        mix_ref[:, _slab(j, 3 * W_GROUP)] = segout_ref[NSLAB + j].astype(BF16)


def _const_spec(shape):
    return pl.BlockSpec(shape, lambda *_: (0,) * len(shape), pipeline_mode=pl.Buffered(1))


def _layer_call(h, consts):
    rows, _ = h.shape
    tb = TIME_BLOCK
    nsteps = rows // tb
    nblk = consts[-1]
    consts = consts[:-1]
    scratch = [
        pltpu.VMEM((tb, D_MODEL), F32),
        pltpu.VMEM((tb, 4 * W_GROUP), BF16),
        pltpu.VMEM((2 * NSLAB, tb, LANES), F32),
        pltpu.VMEM((2 * NSLAB, tb, LANES), F32),
        pltpu.VMEM((tb, W_GROUP), F32),
        pltpu.VMEM((tb, 2 * S5_W), F32),
        pltpu.VMEM((tb, 2 * S5_W), BF16),
        pltpu.VMEM((SUBLANES, 2 * S5_W), F32),
        pltpu.VMEM((SUBLANES, 2 * S5_W), F32),
        pltpu.VMEM((SUBLANES, 2 * S5_W), F32),
        pltpu.VMEM((tb + SUBLANES, W_GROUP), F32),
        pltpu.VMEM((HG_LEVELS + 2, tb, W_GROUP), BF16),
        pltpu.VMEM((HG_LEVELS + 2, tb, W_GROUP), BF16),
        pltpu.VMEM((tb, W_GROUP), F32),
        pltpu.VMEM((W_GROUP, W_GROUP), F32),
        pltpu.VMEM((tb + SUBLANES, W_GROUP), F32),
        pltpu.VMEM((NSLAB, tb, LANES), F32),
        pltpu.VMEM((NSLAB, tb, LANES), F32),
        pltpu.VMEM((tb, W_GROUP), F32),
        pltpu.VMEM((tb, W_GROUP), F32),
        pltpu.VMEM((2 * SUBLANES, W_GROUP), F32),
        pltpu.VMEM((SUBLANES, W_GROUP), F32),
        pltpu.VMEM((SUBLANES, W_GROUP), F32),
    ]
    return pl.pallas_call(
        functools.partial(_layer_kernel, nblk),
        grid=(nsteps + 1,),
        in_specs=[pl.BlockSpec((tb, D_MODEL), lambda n: (jnp.minimum(n, nsteps - 1), 0))]
                 + [_const_spec(c.shape) for c in consts],
        out_specs=pl.BlockSpec((tb, D_MODEL), lambda n: (jnp.maximum(n - 1, 0), 0)),
        out_shape=jax.ShapeDtypeStruct((rows, D_MODEL), F32),
        scratch_shapes=scratch,
        compiler_params=pltpu.CompilerParams(
            dimension_semantics=("arbitrary",), vmem_limit_bytes=VMEM_LIMIT_BYTES),
    )(h, *consts)


def _block_diag(blocks):
    n, a, b = blocks.shape
    eye = jnp.eye(n, dtype=blocks.dtype)
    return jnp.einsum('nab,nm->namb', blocks, eye).reshape(n * a, n * b)


def _hgrn2_level_map():
    t = jnp.arange(HG_CHUNK, dtype=jnp.int32)[:, None]
    s = jnp.arange(HG_CHUNK, dtype=jnp.int32)[None, :]
    x = t ^ s
    lv = jnp.full((HG_CHUNK, HG_CHUNK), -1, jnp.int32)
    for level in range(HG_LEVELS):
        n = 1 << level
        lv = jnp.where((x >= n) & (x < 2 * n) & (t > s), level, lv)
    lv = jnp.where(t == s, HG_LEVELS, lv)
    return jnp.tile(lv, (1, HG_HEADS))


def _prepare_layer(l, lb, w_in, s5_lam_re, s5_lam_im, s5_b_re, s5_b_im, s5_c_re, s5_c_im, s5_d,
                   s5_log_dt, s5_glu_w, s5_glu_b, sc_conv_w, hg_gnorm, lru_conv_w, lru_conv_b,
                   lru_wa, lru_ba, lru_wx, lru_bx, lru_a_param):
    lam_r, lam_i = s5_lam_re[l].astype(F32), s5_lam_im[l].astype(F32)
    dt = jnp.exp(s5_log_dt[l].astype(F32))[:, None]
    arg_r, arg_i = lam_r * dt, lam_i * dt

    def lam_bar_pow(k):
        mag = jnp.exp(arg_r * k)
        return mag * jnp.cos(arg_i * k), mag * jnp.sin(arg_i * k)
    bar_r, bar_i = lam_bar_pow(1.0)
    num_r, num_i = bar_r - 1.0, bar_i
    den = lam_r * lam_r + lam_i * lam_i
    coef_r = ((num_r * lam_r + num_i * lam_i) / den)[..., None]
    coef_i = ((num_i * lam_r - num_r * lam_i) / den)[..., None]
    b_r, b_i = s5_b_re[l].astype(F32), s5_b_im[l].astype(F32)
    bbar_r = jnp.transpose(coef_r * b_r - coef_i * b_i, (0, 2, 1))
    bbar_i = jnp.transpose(coef_r * b_i + coef_i * b_r, (0, 2, 1))
    bmat = jnp.concatenate([_block_diag(bbar_r), _block_diag(bbar_i)], axis=1)
    c_r = jnp.transpose(s5_c_re[l].astype(F32), (0, 2, 1))
    c_i = jnp.transpose(s5_c_im[l].astype(F32), (0, 2, 1))
    cmat = jnp.concatenate([_block_diag(c_r), -_block_diag(c_i)], axis=0)

    seg_r, seg_i = lam_bar_pow(float(SEG))
    lam_rows = jnp.stack([jnp.concatenate([bar_r.reshape(-1), bar_i.reshape(-1)]),
                          jnp.concatenate([seg_r.reshape(-1), seg_i.reshape(-1)])])
    steps = jnp.arange(1, SEG + 1, dtype=F32)[:, None, None]
    pw_r, pw_i = lam_bar_pow(steps)
    ptab = jnp.concatenate([pw_r.reshape(SEG, -1), pw_i.reshape(SEG, -1)], axis=1)

    one_m_lb = 1.0 - lb
    rows = [None] * N_VEC
    rows[V_S5_D] = s5_d[l]
    rows[V_GLU_B] = s5_glu_b[l]
    rows[V_SC_W0], rows[V_SC_W1], rows[V_SC_W2] = sc_conv_w[l, 0], sc_conv_w[l, 1], sc_conv_w[l, 2]
    rows[V_HG_LOGLB] = jnp.maximum(jnp.log(lb), -1e30)
    rows[V_HG_LOG1MLB] = jnp.log1p(-lb)
    rows[V_HG_1MLB] = one_m_lb
    rows[V_HG_GNORM] = hg_gnorm[l]
    for i, v in enumerate((V_LRU_W0, V_LRU_W1, V_LRU_W2, V_LRU_W3)):
        rows[v] = lru_conv_w[l, i]
    rows[V_LRU_CB] = lru_conv_b[l]
    rows[V_LRU_CA] = -LRU_C * jax.nn.softplus(-lru_a_param[l].astype(F32))
    rows[V_LRU_BA] = lru_ba[l]
    rows[V_LRU_BX] = lru_bx[l]
    zero = jnp.zeros((W_GROUP,), F32)
    vec = jnp.stack([zero if r is None else r.astype(F32) for r in rows])
    wab = jnp.concatenate([_block_diag(lru_wa[l].astype(F32)), _block_diag(lru_wx[l].astype(F32))], axis=1)
    return (w_in[l].astype(BF16), vec, bmat.astype(BF16), cmat.astype(BF16), lam_rows, ptab,
            s5_glu_w[l].astype(BF16), wab.astype(BF16))


def kernel(x, meta_tokens, hg_lb_raw, w_in, w_out, s5_lam_re, s5_lam_im, s5_b_re, s5_b_im, s5_c_re, s5_c_im, s5_d, s5_log_dt, s5_glu_w, s5_glu_b, sc_conv_w, hg_gnorm, lru_conv_w, lru_conv_b, lru_wa, lru_ba, lru_wx, lru_bx, lru_a_param, ln1_g, ln1_b, w_ffn_in, w_ffn_out, ln2_g, ln2_b):
    bsz, seq, d = x.shape
    assert d == D_MODEL
    n_tok = N_META + seq
    nblk = -(-n_tok // TIME_BLOCK)
    lp = nblk * TIME_BLOCK
    meta = jnp.broadcast_to(meta_tokens.astype(x.dtype)[None], (bsz, N_META, d))
    h = jnp.concatenate([meta, x, jnp.zeros((bsz, lp - n_tok, d), x.dtype)], axis=1).reshape(bsz * lp, d)

    lb_all = jnp.cumsum(jax.nn.softmax(hg_lb_raw.astype(F32), axis=0), axis=0)
    lb_all = lb_all - lb_all[0:1]

    head = jnp.arange(W_GROUP) // HG_HEAD_DIM
    same_head = head[:, None] == head[None, :]
    bdmask = same_head.astype(F32)
    headmask = same_head.astype(BF16)
    ones = (same_head.astype(F32) / HG_HEAD_DIM).astype(BF16)
    lv = _hgrn2_level_map()

    for l in range(DEPTH):
        prep = _prepare_layer(l, lb_all[l], w_in, s5_lam_re, s5_lam_im, s5_b_re, s5_b_im, s5_c_re, s5_c_im,
                              s5_d, s5_log_dt, s5_glu_w, s5_glu_b, sc_conv_w, hg_gnorm, lru_conv_w,
                              lru_conv_b, lru_wa, lru_ba, lru_wx, lru_bx, lru_a_param)
        ln = jnp.stack([ln1_g[l], ln1_b[l], ln2_g[l], ln2_b[l]]).astype(F32)
        consts = prep + (headmask, bdmask, ones, lv, w_out[l].astype(BF16), ln,
                         w_ffn_in[l].astype(BF16), w_ffn_out[l].astype(BF16), nblk)
        h = _layer_call(h, consts)
    return h.reshape(bsz, lp, d)[:, N_META:n_tok]
```

```python
import functools
import math
import types

import jax
import jax.numpy as jnp
from jax import lax
from jax.experimental import pallas as pl
from jax.experimental.pallas import tpu as pltpu

F32 = jnp.float32
BF16 = jnp.bfloat16

D_MODEL = 1024
DEPTH = 2
N_META = 16
W_GROUP = 256
N_IN = 10 * W_GROUP
S5_GROUP = 16
S5_NGROUPS = 16
S5_STATE = 64
S5_W = S5_NGROUPS * S5_STATE
HG_HEADS = 4
HG_HEAD_DIM = 64
LRU_HEADS = 4
LRU_C = 8.0
D_FF = 2816
ALPHA = (2 * DEPTH) ** 0.25
EPS = 1e-5

SUBLANES = 8
LANES = 128
MXU_N = 256
NSLAB = W_GROUP // LANES
TIME_BLOCK = 320
SEG = TIME_BLOCK // SUBLANES
SCAN_PIECE = 8
HG_CHUNK = 64
HG_LEVELS = 6
VMEM_LIMIT_BYTES = 60 * 1024 * 1024

(V_S5_D, V_GLU_B, V_SC_W0, V_SC_W1, V_SC_W2, V_HG_LOGLB, V_HG_LOG1MLB, V_HG_1MLB, V_HG_GNORM,
 V_LRU_W0, V_LRU_W1, V_LRU_W2, V_LRU_W3, V_LRU_CB, V_LRU_CA, V_LRU_BA, V_LRU_BX) = range(17)
N_VEC = 24


def _sigmoid(x):
    return 0.5 + 0.5 * jnp.tanh(0.5 * x)


def _silu(x):
    h = 0.5 * x
    return h + h * jnp.tanh(h)


def _gelu_tanh(x):
    c = math.sqrt(2.0 / math.pi)
    return x * (0.5 * (1.0 + jnp.tanh(c * (x + 0.044715 * (x * x * x)))))


def _dot(a, b):
    return jnp.dot(a, b, preferred_element_type=F32)


def _dot_nt(a, b):
    return lax.dot_general(a, b, (((1,), (1,)), ((), ())), preferred_element_type=F32)


def _dot_tn(a, b):
    return lax.dot_general(a, b, (((0,), (0,)), ((), ())), preferred_element_type=F32)


def _tile(r):
    return slice(r * SUBLANES, (r + 1) * SUBLANES)


def _slab(j, first=0):
    return slice(first + j * LANES, first + (j + 1) * LANES)


def _cols(j, first=0):
    return slice(first + j * MXU_N, first + (j + 1) * MXU_N)


def _seg_rows(r):
    return pl.ds(r, SUBLANES, stride=SEG)


def _vrow(vec_ref, i):
    return vec_ref[i:i + 1, :]


def _layer_norm(x, g, b):
    mu = jnp.mean(x, axis=-1, keepdims=True)
    xc = x - mu
    var = jnp.mean(xc * xc, axis=-1, keepdims=True)
    return xc * lax.rsqrt(var + EPS) * g + b


def _sibling(x, n):
    c, w = x.shape
    if n < SUBLANES:
        x3 = x.reshape(c // SUBLANES, SUBLANES, w)
        fwd = pltpu.roll(x3, n, 1)
        if 2 * n == SUBLANES:
            return fwd.reshape(c, w)
        bwd = pltpu.roll(x3, SUBLANES - n, 1)
        row = lax.broadcasted_iota(jnp.int32, x3.shape, 1)
        return jnp.where((row & n) != 0, fwd, bwd).reshape(c, w)
    x4 = x.reshape(c // (2 * n), 2, n, w)
    return jnp.concatenate([x4[:, 1:2], x4[:, 0:1]], axis=1).reshape(c, w)


class _Streams:
    def __init__(self):
        self.lanes = {}

    def add(self, lane, name, cost, fn, deps=()):
        self.lanes.setdefault(lane, []).append((name, cost, tuple(deps), fn))

    def emit(self):
        total = {k: sum(p[1] for p in v) for k, v in self.lanes.items()}
        pos = dict.fromkeys(self.lanes, 0)
        spent = dict.fromkeys(self.lanes, 0)
        done = set()

        def ready(k):
            return pos[k] < len(self.lanes[k]) and all(d in done for d in self.lanes[k][pos[k]][2])
        while any(pos[k] < len(v) for k, v in self.lanes.items()):
            order = sorted(self.lanes, key=lambda k: spent[k] / total[k] * LANE_LAG.get(k, 1.0))
            pick = next((k for k in order if ready(k)), None)
            assert pick is not None, "piece dependencies cannot be met"
            name, cost, _, fn = self.lanes[pick][pos[pick]]
            fn()
            done.add(name)
            pos[pick] += 1
            spent[pick] += cost


MXU, PROJ, VPU, S5, HG, LRU = "mxu", "proj", "misc", "s5", "hg", "lru"
LANE_LAG = {MXU: 0.8, PROJ: 0.3}


def _out_proj_pieces(st, r):
    for j in range(D_MODEL // MXU_N):
        def out_proj(j=j):
            r.res[:, _cols(j)] = ALPHA * r.hprev[:, _cols(j)] + _dot(r.mix[...], r.w_out[:, _cols(j)])
        st.add(MXU, f"out_proj{j}", 320, out_proj)


def _ln1_piece(st, r):
    def ln1():
        hn = _layer_norm(r.res[...], r.ln[0:1, :], r.ln[1:2, :])
        r.hn[...] = hn
        r.hnb[...] = hn.astype(BF16)
    st.add(VPU, "ln1", 800, ln1, [f"out_proj{D_MODEL // MXU_N - 1}"])


def _ffn_pieces(st, r):
    ncol = D_MODEL // MXU_N
    nff = D_FF // MXU_N
    for j in range(nff):
        def ffn_in(j=j):
            hb = r.hnb[...]
            gate = _dot(hb, r.w_ffn_in[:, _cols(j)])
            up = _dot(hb, r.w_ffn_in[:, _cols(j, D_FF)])
            r.act[:, _cols(j)] = (_silu(gate) * up).astype(BF16)
        st.add(MXU, f"ffn_in{j}", 640, ffn_in, ["ln1"])

    for j in range(ncol):
        def ffn_out(j=j):
            r.res[:, _cols(j)] = ALPHA * r.hn[:, _cols(j)] + _dot(r.act[...], r.w_ffn_out[:, _cols(j)])
        st.add(MXU, f"ffn_out{j}", 880, ffn_out, [f"ffn_in{nff - 1}"])


def _channel_tail(st, r):
    def ln2():
        r.out[...] = _layer_norm(r.res[...], r.ln[2:3, :], r.ln[3:4, :])
    st.add(VPU, "ln2", 800, ln2, [f"ffn_out{D_MODEL // MXU_N - 1}"])


def _in_proj_pieces(st, r):
    def cast():
        r.hb[...] = r.h[...].astype(BF16)
    st.add(VPU, "cast", 100, cast)

    def proj(j):
        return _dot(r.hb[...], r.w_in[:, _cols(j)])

    def p_s5():
        p = proj(0)
        for j in range(NSLAB):
            r.segin[j] = p[:, _slab(j)]
    st.add(PROJ, "p_s5", 320, p_s5, ["cast"])
    return proj


def _in_proj_rest(st, r, proj):
    for j in range(4):
        def p_hg(j=j):
            r.hgp[:, _cols(j)] = proj(4 + j)
        st.add(PROJ, f"p_hg{j}", 320, p_hg)

    def p_lru_x():
        r.lru_xbuf[SUBLANES:SUBLANES + TIME_BLOCK, :] = proj(8)
    st.add(PROJ, "p_lru_x", 320, p_lru_x)

    def p_lru_y():
        p = proj(9)
        for j in range(NSLAB):
            r.segin[NSLAB + j] = p[:, _slab(j)]
    st.add(PROJ, "p_lru_y", 320, p_lru_y)
    for j in range(3):
        def p_sc(j=j):
            r.scp[:, _cols(j)] = proj(1 + j)
        st.add(PROJ, f"p_sc{j}", 320, p_sc)


def _s5_head(st, r):
    def gather():
        for t in range(SEG):
            for j in range(NSLAB):
                r.uperm[_tile(t), _slab(j)] = r.segin[j, _seg_rows(t), :]
        r.bu[...] = _dot(r.uperm[...].astype(BF16), r.bmat[...])
    st.add(S5, "s5_gather", 300, gather, ["p_s5"])


def _s5_pieces(st, r):
    n = S5_W
    env = types.SimpleNamespace()

    def scan(first):
        lam_r = jnp.broadcast_to(r.lam[0:1, 0:n], (SUBLANES, n))
        lam_i = jnp.broadcast_to(r.lam[0:1, n:2 * n], (SUBLANES, n))
        for t in range(first, first + SCAN_PIECE):
            br, bi = r.bu[_tile(t), 0:n], r.bu[_tile(t), n:2 * n]
            if t == 0:
                env.xr, env.xi = br, bi
            else:
                env.xr, env.xi = (lam_r * env.xr - lam_i * env.xi + br,
                                  lam_r * env.xi + lam_i * env.xr + bi)
                r.bu[_tile(t), 0:n] = env.xr
                r.bu[_tile(t), n:2 * n] = env.xi
    for first in range(0, SEG, SCAN_PIECE):
        st.add(S5, f"s5_scan{first}", 22 * SCAN_PIECE, functools.partial(scan, first))

    def carry():
        r.s5_e[:, 0:n] = env.xr
        r.s5_e[:, n:2 * n] = env.xi
        pr, pi_ = r.lam[1:2, 0:n], r.lam[1:2, n:2 * n]
        cr, ci = r.s5_state[0:1, 0:n], r.s5_state[0:1, n:2 * n]
        for s in range(SUBLANES):
            r.s5_cin[s:s + 1, 0:n] = cr
            r.s5_cin[s:s + 1, n:2 * n] = ci
            sr, si = r.s5_e[s:s + 1, 0:n], r.s5_e[s:s + 1, n:2 * n]
            cr, ci = sr + (pr * cr - pi_ * ci), si + (pr * ci + pi_ * cr)
        r.s5_state[0:1, 0:n] = cr
        r.s5_state[0:1, n:2 * n] = ci
    st.add(S5, "s5_carry", 100, carry)

    def fix(first):
        cin_r = r.s5_cin[:, 0:n]
        cin_i = r.s5_cin[:, n:2 * n]
        for tt in range(first // 2, (first + SCAN_PIECE) // 2):
            parts_r, parts_i = [], []
            for t in (2 * tt, 2 * tt + 1):
                qr = jnp.broadcast_to(r.ptab[t:t + 1, 0:n], (SUBLANES, n))
                qi = jnp.broadcast_to(r.ptab[t:t + 1, n:2 * n], (SUBLANES, n))
                parts_r.append(r.bu[_tile(t), 0:n] + (qr * cin_r - qi * cin_i))
                parts_i.append(r.bu[_tile(t), n:2 * n] + (qr * cin_i + qi * cin_r))
            rows2 = slice(tt * 2 * SUBLANES, (tt + 1) * 2 * SUBLANES)
            r.xbf[rows2, 0:n] = jnp.concatenate(parts_r, axis=0).astype(BF16)
            r.xbf[rows2, n:2 * n] = jnp.concatenate(parts_i, axis=0).astype(BF16)
    for first in range(0, SEG, SCAN_PIECE):
        st.add(S5, f"s5_fix{first}", 32 * SCAN_PIECE, functools.partial(fix, first))

    def post():
        y = _dot(r.xbf[...], r.cmat[...])
        y = y + _vrow(r.vec, V_S5_D) * r.uperm[...]
        g = _gelu_tanh(y)
        gate = _sigmoid(_dot(g.astype(BF16), r.glu_w[...]) + _vrow(r.vec, V_GLU_B))
        r.uperm[...] = g * gate
        for t in range(SEG):
            for j in range(NSLAB):
                r.segout[j, _seg_rows(t), :] = r.uperm[_tile(t), _slab(j)]
    st.add(S5, "s5_post", 500, post)


def _short_conv_pieces(st, r):
    def conv():
        tb = TIME_BLOCK
        h, gb, gc = r.scp[:, 0:W_GROUP], r.scp[:, W_GROUP:2 * W_GROUP], r.scp[:, 2 * W_GROUP:3 * W_GROUP]
        r.sc_buf[SUBLANES:SUBLANES + tb, :] = gc * h
        out = (_vrow(r.vec, V_SC_W0) * r.sc_buf[SUBLANES - 2:SUBLANES - 2 + tb, :]
               + _vrow(r.vec, V_SC_W1) * r.sc_buf[SUBLANES - 1:SUBLANES - 1 + tb, :]
               + _vrow(r.vec, V_SC_W2) * r.sc_buf[SUBLANES:SUBLANES + tb, :])
        r.mix[:, W_GROUP:2 * W_GROUP] = (gb * out).astype(BF16)
        r.sc_buf[0:SUBLANES, :] = r.sc_buf[tb:tb + SUBLANES, :]
    st.add(VPU, "sc", 400, conv, ["p_sc2", f"out_proj{D_MODEL // MXU_N - 1}"])


def _hgrn2_pieces(st, r):
    c = HG_CHUNK
    env = types.SimpleNamespace()

    def stack_heads(x_bf):
        return jnp.concatenate([x_bf] * HG_HEADS, axis=0) * r.headmask[...]

    def gates():
        q_in = r.hgp[:, 0:W_GROUP]
        z = r.hgp[:, W_GROUP:2 * W_GROUP]
        q = _silu(q_in) * (HG_HEAD_DIM ** -0.5)
        log_sig = jnp.minimum(z, 0.0) - jnp.log(1.0 + jnp.exp(-jnp.abs(z)))
        cpl = _vrow(r.vec, V_HG_LOG1MLB) + log_sig
        loglb = _vrow(r.vec, V_HG_LOGLB)
        g = jnp.maximum(loglb, cpl) + jnp.log(1.0 + jnp.exp(-jnp.abs(loglb - cpl)))
        k = _vrow(r.vec, V_HG_1MLB) * _sigmoid(-z)
        r.hg_qt[HG_LEVELS + 1] = q.astype(BF16)
        r.hg_kt[HG_LEVELS + 1] = k.astype(BF16)
        env.q, env.k, env.cin, env.sfx = q, k, g, jnp.zeros_like(g)
    st.add(HG, "hg_gates", 600, gates, ["p_hg1"])

    def level(lvl):
        r.hg_qt[lvl] = (env.q * jnp.exp(env.cin)).astype(BF16)
        r.hg_kt[lvl] = (env.k * jnp.exp(env.sfx)).astype(BF16)
        if lvl == HG_LEVELS:
            r.hg_cin[...] = env.cin
            return
        n = 1 << lvl
        row = lax.broadcasted_iota(jnp.int32, (TIME_BLOCK, W_GROUP), 0)
        sib = _sibling(env.cin + env.sfx, n)
        right = (row & n) != 0
        env.cin = env.cin + jnp.where(right, sib, 0.0)
        env.sfx = env.sfx + jnp.where(right, 0.0, sib)
    for lvl in range(HG_LEVELS + 1):
        st.add(HG, f"hg_level{lvl}", 350, functools.partial(level, lvl))

    nchunk = TIME_BLOCK // c

    def chunk_rows(ci):
        return slice(ci * c, (ci + 1) * c)

    def intra(ci):
        rows = chunk_rows(ci)
        lv = r.lv[...]
        v = r.hgp[rows, 2 * W_GROUP:3 * W_GROUP].astype(BF16)
        scores = None
        for lvl in (HG_LEVELS + 1,) + tuple(range(HG_LEVELS)):
            s_n = _dot_nt(stack_heads(r.hg_kt[lvl, rows, :]), r.hg_qt[lvl, rows, :])
            code = HG_LEVELS if lvl == HG_LEVELS + 1 else lvl
            scores = jnp.where(lv == code, s_n, 0.0 if scores is None else scores)
        r.hg_o[rows, :] = _dot_tn(scores.astype(BF16), stack_heads(v))
        r.hg_upd[ci] = _dot_tn(v, r.hg_kt[HG_LEVELS, rows, :]) * r.bdmask[...]
    for ci in range(nchunk):
        st.add(HG, f"hg_intra{ci}", 400, functools.partial(intra, ci), ["p_hg3"])

    def inter(ci):
        rows = chunk_rows(ci)
        state = r.hg_state[...]
        r.hg_o[rows, :] = r.hg_o[rows, :] + _dot_nt(r.hg_qt[HG_LEVELS, rows, :], state.astype(BF16))
        decay = jnp.exp(r.hg_cin[(ci + 1) * c - 1:(ci + 1) * c, :])
        r.hg_state[...] = state * decay + r.hg_upd[ci]
    for ci in range(nchunk):
        st.add(HG, f"hg_inter{ci}", 150, functools.partial(inter, ci))

    def norm():
        o = r.hg_o[...]
        ms = _dot((o * o).astype(BF16), r.ones[...])
        o = o * lax.rsqrt(ms + EPS) * _vrow(r.vec, V_HG_GNORM)
        g_in = r.hgp[:, 3 * W_GROUP:4 * W_GROUP]
        r.mix[:, 2 * W_GROUP:3 * W_GROUP] = (o * _silu(g_in)).astype(BF16)
    st.add(HG, "hg_norm", 300, norm, [f"out_proj{D_MODEL // MXU_N - 1}"])


def _rglru_pieces(st, r):
    tb = TIME_BLOCK
    env = types.SimpleNamespace()

    def seg_tile(ref, first, t):
        return jnp.concatenate([ref[first + j, _seg_rows(t), :] for j in range(NSLAB)], axis=1)

    def pre():
        xc = _vrow(r.vec, V_LRU_CB) + (
            _vrow(r.vec, V_LRU_W0) * r.lru_xbuf[SUBLANES - 3:SUBLANES - 3 + tb, :]
            + _vrow(r.vec, V_LRU_W1) * r.lru_xbuf[SUBLANES - 2:SUBLANES - 2 + tb, :]
            + _vrow(r.vec, V_LRU_W2) * r.lru_xbuf[SUBLANES - 1:SUBLANES - 1 + tb, :]
            + _vrow(r.vec, V_LRU_W3) * r.lru_xbuf[SUBLANES:SUBLANES + tb, :])
        r.lru_xbuf[0:SUBLANES, :] = r.lru_xbuf[tb:tb + SUBLANES, :]
        gates = _dot(xc.astype(BF16), r.wab[...])
        gate_a = _sigmoid(gates[:, 0:W_GROUP] + _vrow(r.vec, V_LRU_BA))
        gate_x = _sigmoid(gates[:, W_GROUP:2 * W_GROUP] + _vrow(r.vec, V_LRU_BX))
        log_a = _vrow(r.vec, V_LRU_CA) * gate_a
        a = jnp.exp(log_a)
        b = xc * gate_x * jnp.sqrt(-jnp.tanh(log_a) * (a * a + 1.0))
        for j in range(NSLAB):
            r.lru_a[j] = a[:, _slab(j)]
            r.lru_b[j] = b[:, _slab(j)]
    st.add(LRU, "lru_pre", 500, pre, ["p_lru_x"])

    def scan(first):
        for t in range(first, first + SCAN_PIECE):
            a_t, b_t = seg_tile(r.lru_a, 0, t), seg_tile(r.lru_b, 0, t)
            env.h, env.p = (b_t, a_t) if t == 0 else (a_t * env.h + b_t, a_t * env.p)
            r.lru_h[_tile(t), :] = env.h
            r.lru_p[_tile(t), :] = env.p
    for first in range(0, SEG, SCAN_PIECE):
        st.add(LRU, f"lru_scan{first}", 18 * SCAN_PIECE, functools.partial(scan, first))

    def carry():
        r.lru_e[0:SUBLANES, :] = env.h
        r.lru_e[SUBLANES:2 * SUBLANES, :] = env.p
        cur = r.lru_state[0:1, :]
        for s in range(SUBLANES):
            r.lru_cin[s:s + 1, :] = cur
            cur = r.lru_e[s:s + 1, :] + r.lru_e[SUBLANES + s:SUBLANES + s + 1, :] * cur
        r.lru_state[0:1, :] = cur
    st.add(LRU, "lru_carry", 50, carry)

    def fix(first):
        cin = r.lru_cin[...]
        for t in range(first, first + SCAN_PIECE):
            res = (r.lru_h[_tile(t), :] + r.lru_p[_tile(t), :] * cin) * _gelu_tanh(seg_tile(r.segin, NSLAB, t))
            for j in range(NSLAB):
                r.segout[NSLAB + j, _seg_rows(t), :] = res[:, _slab(j)]
    for first in range(0, SEG, SCAN_PIECE):
        st.add(LRU, f"lru_fix{first}", 38 * SCAN_PIECE, functools.partial(fix, first), ["p_lru_y"])


_SCRATCH = (
    ("hprev", (TIME_BLOCK, D_MODEL), F32),
    ("mix", (TIME_BLOCK, 4 * W_GROUP), BF16),
    ("res", (TIME_BLOCK, D_MODEL), F32),
    ("hn", (TIME_BLOCK, D_MODEL), F32),
    ("hnb", (TIME_BLOCK, D_MODEL), BF16),
    ("act", (TIME_BLOCK, D_FF), BF16),
    ("hb", (TIME_BLOCK, D_MODEL), BF16),
    ("hgp", (TIME_BLOCK, 4 * W_GROUP), F32),
    ("scp", (TIME_BLOCK, 3 * W_GROUP), F32),
    ("segin", (2 * NSLAB, TIME_BLOCK, LANES), F32),
    ("segout", (2 * NSLAB, TIME_BLOCK, LANES), F32),
    ("uperm", (TIME_BLOCK, W_GROUP), F32),
    ("bu", (TIME_BLOCK, 2 * S5_W), F32),
    ("xbf", (TIME_BLOCK, 2 * S5_W), BF16),
    ("s5_e", (SUBLANES, 2 * S5_W), F32),
    ("s5_cin", (SUBLANES, 2 * S5_W), F32),
    ("s5_state", (SUBLANES, 2 * S5_W), F32),
    ("sc_buf", (TIME_BLOCK + SUBLANES, W_GROUP), F32),
    ("hg_qt", (HG_LEVELS + 2, TIME_BLOCK, W_GROUP), BF16),
    ("hg_kt", (HG_LEVELS + 2, TIME_BLOCK, W_GROUP), BF16),
    ("hg_cin", (TIME_BLOCK, W_GROUP), F32),
    ("hg_state", (W_GROUP, W_GROUP), F32),
    ("hg_upd", (TIME_BLOCK // HG_CHUNK, W_GROUP, W_GROUP), F32),
    ("hg_o", (TIME_BLOCK, W_GROUP), F32),
    ("lru_xbuf", (TIME_BLOCK + SUBLANES, W_GROUP), F32),
    ("lru_a", (NSLAB, TIME_BLOCK, LANES), F32),
    ("lru_b", (NSLAB, TIME_BLOCK, LANES), F32),
    ("lru_h", (TIME_BLOCK, W_GROUP), F32),
    ("lru_p", (TIME_BLOCK, W_GROUP), F32),
    ("lru_e", (2 * SUBLANES, W_GROUP), F32),
    ("lru_cin", (SUBLANES, W_GROUP), F32),
    ("lru_state", (SUBLANES, W_GROUP), F32),
)
_INPUTS = ("h", "w_in", "vec", "bmat", "cmat", "lam", "ptab", "glu_w", "wab", "headmask", "bdmask", "ones",
           "lv", "w_out", "ln", "w_ffn_in", "w_ffn_out")


def _layer_kernel(nblk, *refs):
    names = _INPUTS + ("out",) + tuple(s[0] for s in _SCRATCH)
    r = types.SimpleNamespace(**dict(zip(names, refs, strict=True)))
    step = pl.program_id(0)

    @pl.when(step == 0)
    def _():
        r.hprev[...] = jnp.zeros_like(r.hprev)
        r.mix[...] = jnp.zeros_like(r.mix)

    @pl.when(step % nblk == 0)
    def _():
        r.s5_state[...] = jnp.zeros_like(r.s5_state)
        r.sc_buf[0:SUBLANES, :] = jnp.zeros((SUBLANES, W_GROUP), F32)
        r.hg_state[...] = jnp.zeros_like(r.hg_state)
        r.lru_xbuf[0:SUBLANES, :] = jnp.zeros((SUBLANES, W_GROUP), F32)
        r.lru_state[...] = jnp.zeros_like(r.lru_state)

    st = _Streams()
    proj = _in_proj_pieces(st, r)
    _out_proj_pieces(st, r)
    _in_proj_rest(st, r, proj)
    _ffn_pieces(st, r)
    _s5_head(st, r)
    _ln1_piece(st, r)
    _s5_pieces(st, r)
    _hgrn2_pieces(st, r)
    _rglru_pieces(st, r)
    _short_conv_pieces(st, r)

    def finish():
        for j in range(NSLAB):
            r.mix[:, _slab(j)] = r.segout[j].astype(BF16)
            r.mix[:, _slab(j, 3 * W_GROUP)] = r.segout[NSLAB + j].astype(BF16)
        r.hprev[...] = r.h[...]
    st.add(VPU, "finish", 200, finish,
           [f"out_proj{D_MODEL // MXU_N - 1}", "s5_post", f"lru_fix{SEG - SCAN_PIECE}"])
    _channel_tail(st, r)
    st.emit()


def _const_spec(shape):
    return pl.BlockSpec(shape, lambda *_: (0,) * len(shape), pipeline_mode=pl.Buffered(1))


def _layer_call(h, consts, nblk):
    rows, _ = h.shape
    tb = TIME_BLOCK
    nsteps = rows // tb
    return pl.pallas_call(
        functools.partial(_layer_kernel, nblk),
        grid=(nsteps + 1,),
        in_specs=[pl.BlockSpec((tb, D_MODEL), lambda n: (jnp.minimum(n, nsteps - 1), 0))]
                 + [_const_spec(c.shape) for c in consts],
        out_specs=pl.BlockSpec((tb, D_MODEL), lambda n: (jnp.maximum(n - 1, 0), 0)),
        out_shape=jax.ShapeDtypeStruct((rows, D_MODEL), F32),
        scratch_shapes=[pltpu.VMEM(shape, dtype) for _, shape, dtype in _SCRATCH],
        compiler_params=pltpu.CompilerParams(
            dimension_semantics=("arbitrary",), vmem_limit_bytes=VMEM_LIMIT_BYTES),
    )(h, *consts)


def _block_diag(blocks):
    n, a, b = blocks.shape
    eye = jnp.eye(n, dtype=blocks.dtype)
    return jnp.einsum('nab,nm->namb', blocks, eye).reshape(n * a, n * b)


def _hgrn2_level_map():
    t = jnp.arange(HG_CHUNK, dtype=jnp.int32)[:, None]
    s = jnp.arange(HG_CHUNK, dtype=jnp.int32)[None, :]
    x = t ^ s
    lv = jnp.full((HG_CHUNK, HG_CHUNK), -1, jnp.int32)
    for level in range(HG_LEVELS):
        n = 1 << level
        lv = jnp.where((x >= n) & (x < 2 * n) & (t > s), level, lv)
    lv = jnp.where(t == s, HG_LEVELS, lv)
    return jnp.tile(lv.T, (HG_HEADS, 1))


def _prepare_layer(l, lb, w_in, s5_lam_re, s5_lam_im, s5_b_re, s5_b_im, s5_c_re, s5_c_im, s5_d,
                   s5_log_dt, s5_glu_w, s5_glu_b, sc_conv_w, hg_gnorm, lru_conv_w, lru_conv_b,
                   lru_wa, lru_ba, lru_wx, lru_bx, lru_a_param):
    lam_r, lam_i = s5_lam_re[l].astype(F32), s5_lam_im[l].astype(F32)
    dt = jnp.exp(s5_log_dt[l].astype(F32))[:, None]
    arg_r, arg_i = lam_r * dt, lam_i * dt

    def lam_bar_pow(k):
        mag = jnp.exp(arg_r * k)
        return mag * jnp.cos(arg_i * k), mag * jnp.sin(arg_i * k)
    bar_r, bar_i = lam_bar_pow(1.0)
    num_r, num_i = bar_r - 1.0, bar_i
    den = lam_r * lam_r + lam_i * lam_i
    coef_r = ((num_r * lam_r + num_i * lam_i) / den)[..., None]
    coef_i = ((num_i * lam_r - num_r * lam_i) / den)[..., None]
    b_r, b_i = s5_b_re[l].astype(F32), s5_b_im[l].astype(F32)
    bbar_r = jnp.transpose(coef_r * b_r - coef_i * b_i, (0, 2, 1))
    bbar_i = jnp.transpose(coef_r * b_i + coef_i * b_r, (0, 2, 1))
    bmat = jnp.concatenate([_block_diag(bbar_r), _block_diag(bbar_i)], axis=1)
    c_r = jnp.transpose(s5_c_re[l].astype(F32), (0, 2, 1))
    c_i = jnp.transpose(s5_c_im[l].astype(F32), (0, 2, 1))
    cmat = jnp.concatenate([_block_diag(c_r), -_block_diag(c_i)], axis=0)

    seg_r, seg_i = lam_bar_pow(float(SEG))
    lam_rows = jnp.stack([jnp.concatenate([bar_r.reshape(-1), bar_i.reshape(-1)]),
                          jnp.concatenate([seg_r.reshape(-1), seg_i.reshape(-1)])])
    steps = jnp.arange(1, SEG + 1, dtype=F32)[:, None, None]
    pw_r, pw_i = lam_bar_pow(steps)
    ptab = jnp.concatenate([pw_r.reshape(SEG, -1), pw_i.reshape(SEG, -1)], axis=1)

    one_m_lb = 1.0 - lb
    rows = [None] * N_VEC
    rows[V_S5_D] = s5_d[l]
    rows[V_GLU_B] = s5_glu_b[l]
    rows[V_SC_W0], rows[V_SC_W1], rows[V_SC_W2] = sc_conv_w[l, 0], sc_conv_w[l, 1], sc_conv_w[l, 2]
    rows[V_HG_LOGLB] = jnp.maximum(jnp.log(lb), -1e30)
    rows[V_HG_LOG1MLB] = jnp.log1p(-lb)
    rows[V_HG_1MLB] = one_m_lb
    rows[V_HG_GNORM] = hg_gnorm[l]
    for i, v in enumerate((V_LRU_W0, V_LRU_W1, V_LRU_W2, V_LRU_W3)):
        rows[v] = lru_conv_w[l, i]
    rows[V_LRU_CB] = lru_conv_b[l]
    rows[V_LRU_CA] = -LRU_C * jax.nn.softplus(-lru_a_param[l].astype(F32))
    rows[V_LRU_BA] = lru_ba[l]
    rows[V_LRU_BX] = lru_bx[l]
    zero = jnp.zeros((W_GROUP,), F32)
    vec = jnp.stack([zero if r is None else r.astype(F32) for r in rows])
    wab = jnp.concatenate([_block_diag(lru_wa[l].astype(F32)), _block_diag(lru_wx[l].astype(F32))], axis=1)
    return (w_in[l].astype(BF16), vec, bmat.astype(BF16), cmat.astype(BF16), lam_rows, ptab,
            s5_glu_w[l].astype(BF16), wab.astype(BF16))


def kernel(x, meta_tokens, hg_lb_raw, w_in, w_out, s5_lam_re, s5_lam_im, s5_b_re, s5_b_im, s5_c_re, s5_c_im, s5_d, s5_log_dt, s5_glu_w, s5_glu_b, sc_conv_w, hg_gnorm, lru_conv_w, lru_conv_b, lru_wa, lru_ba, lru_wx, lru_bx, lru_a_param, ln1_g, ln1_b, w_ffn_in, w_ffn_out, ln2_g, ln2_b):
    bsz, seq, d = x.shape
    assert d == D_MODEL
    n_tok = N_META + seq
    nblk = -(-n_tok // TIME_BLOCK)
    lp = nblk * TIME_BLOCK
    meta = jnp.broadcast_to(meta_tokens.astype(x.dtype)[None], (bsz, N_META, d))
    h = jnp.concatenate([meta, x, jnp.zeros((bsz, lp - n_tok, d), x.dtype)], axis=1).reshape(bsz * lp, d)

    lb_all = jnp.cumsum(jax.nn.softmax(hg_lb_raw.astype(F32), axis=0), axis=0)
    lb_all = lb_all - lb_all[0:1]

    head = jnp.arange(W_GROUP) // HG_HEAD_DIM
    same_head = head[:, None] == head[None, :]
    bdmask = same_head.astype(F32)
    headmask = same_head.astype(BF16)
    ones = (same_head.astype(F32) / HG_HEAD_DIM).astype(BF16)
    lv = _hgrn2_level_map()

    for l in range(DEPTH):
        prep = _prepare_layer(l, lb_all[l], w_in, s5_lam_re, s5_lam_im, s5_b_re, s5_b_im, s5_c_re, s5_c_im,
                              s5_d, s5_log_dt, s5_glu_w, s5_glu_b, sc_conv_w, hg_gnorm, lru_conv_w,
                              lru_conv_b, lru_wa, lru_ba, lru_wx, lru_bx, lru_a_param)
        ln = jnp.stack([ln1_g[l], ln1_b[l], ln2_g[l], ln2_b[l]]).astype(F32)
        consts = prep + (headmask, bdmask, ones, lv, w_out[l].astype(BF16), ln,
                         w_ffn_in[l].astype(BF16), w_ffn_out[l].astype(BF16))
        h = _layer_call(h, consts, nblk)
    return h.reshape(bsz, lp, d)[:, N_META:n_tok]
```

```python
import functools
import math
import types

import jax
import jax.numpy as jnp
from jax import lax
from jax.experimental import pallas as pl
from jax.experimental.pallas import tpu as pltpu

F32 = jnp.float32
BF16 = jnp.bfloat16

D_MODEL = 1024
DEPTH = 2
N_META = 16
W_GROUP = 256
N_IN = 10 * W_GROUP
S5_GROUP = 16
S5_NGROUPS = 16
S5_STATE = 64
S5_W = S5_NGROUPS * S5_STATE
HG_HEADS = 4
HG_HEAD_DIM = 64
LRU_HEADS = 4
LRU_C = 8.0
D_FF = 2816
ALPHA = (2 * DEPTH) ** 0.25
EPS = 1e-5

SUBLANES = 8
LANES = 128
MXU_N = 256
NSLAB = W_GROUP // LANES
TIME_BLOCK = 320
SEG = TIME_BLOCK // SUBLANES
SCAN_PIECE = 8
HG_CHUNK = 64
HG_LEVELS = 6
VMEM_LIMIT_BYTES = 60 * 1024 * 1024

(V_S5_D, V_GLU_B, V_SC_W0, V_SC_W1, V_SC_W2, V_HG_LOGLB, V_HG_LOG1MLB, V_HG_1MLB, V_HG_GNORM,
 V_LRU_W0, V_LRU_W1, V_LRU_W2, V_LRU_W3, V_LRU_CB, V_LRU_CA, V_LRU_BA, V_LRU_BX) = range(17)
N_VEC = 24


def _sigmoid(x):
    return 0.5 + 0.5 * jnp.tanh(0.5 * x)


def _silu(x):
    h = 0.5 * x
    return h + h * jnp.tanh(h)


def _gelu_tanh(x):
    c = math.sqrt(2.0 / math.pi)
    return x * (0.5 * (1.0 + jnp.tanh(c * (x + 0.044715 * (x * x * x)))))


def _dot(a, b):
    return jnp.dot(a, b, preferred_element_type=F32)


def _dot_nt(a, b):
    return lax.dot_general(a, b, (((1,), (1,)), ((), ())), preferred_element_type=F32)


def _dot_tn(a, b):
    return lax.dot_general(a, b, (((0,), (0,)), ((), ())), preferred_element_type=F32)


def _tile(r):
    return slice(r * SUBLANES, (r + 1) * SUBLANES)


def _slab(j, first=0):
    return slice(first + j * LANES, first + (j + 1) * LANES)


def _cols(j, first=0):
    return slice(first + j * MXU_N, first + (j + 1) * MXU_N)


def _seg_rows(r):
    return pl.ds(r, SUBLANES, stride=SEG)


def _vrow(vec_ref, i):
    return vec_ref[i:i + 1, :]


def _layer_norm(x, g, b):
    mu = jnp.mean(x, axis=-1, keepdims=True)
    xc = x - mu
    var = jnp.mean(xc * xc, axis=-1, keepdims=True)
    return xc * lax.rsqrt(var + EPS) * g + b


def _sibling(x, n):
    c, w = x.shape
    if n < SUBLANES:
        x3 = x.reshape(c // SUBLANES, SUBLANES, w)
        fwd = pltpu.roll(x3, n, 1)
        if 2 * n == SUBLANES:
            return fwd.reshape(c, w)
        bwd = pltpu.roll(x3, SUBLANES - n, 1)
        row = lax.broadcasted_iota(jnp.int32, x3.shape, 1)
        return jnp.where((row & n) != 0, fwd, bwd).reshape(c, w)
    x4 = x.reshape(c // (2 * n), 2, n, w)
    return jnp.concatenate([x4[:, 1:2], x4[:, 0:1]], axis=1).reshape(c, w)


class _Streams:
    def __init__(self):
        self.lanes = {}

    def add(self, lane, name, cost, fn, deps=()):
        self.lanes.setdefault(lane, []).append((name, cost, tuple(deps), fn))

    def emit(self):
        total = {k: sum(p[1] for p in v) for k, v in self.lanes.items()}
        pos = dict.fromkeys(self.lanes, 0)
        spent = dict.fromkeys(self.lanes, 0)
        done = set()

        def ready(k):
            return pos[k] < len(self.lanes[k]) and all(d in done for d in self.lanes[k][pos[k]][2])
        while any(pos[k] < len(v) for k, v in self.lanes.items()):
            order = sorted(self.lanes, key=lambda k: spent[k] / total[k] * LANE_LAG.get(k, 1.0))
            pick = next((k for k in order if ready(k)), None)
            assert pick is not None, "piece dependencies cannot be met"
            name, cost, _, fn = self.lanes[pick][pos[pick]]
            fn()
            done.add(name)
            pos[pick] += 1
            spent[pick] += cost


MXU, PROJ, VPU, S5, HG, LRU, TAIL = "mxu", "proj", "misc", "s5", "hg", "lru", "tail"
LANE_LAG = {MXU: 0.8, PROJ: 0.3}


def _out_proj_pieces(st, r):
    for j in range(D_MODEL // MXU_N):
        def out_proj(j=j):
            r.res[:, _cols(j)] = ALPHA * r.hprev[:, _cols(j)] + _dot(r.mix[...], r.w_out[:, _cols(j)])
        st.add(MXU, f"out_proj{j}", 320, out_proj)


def _ln1_piece(st, r):
    def ln1():
        hn = _layer_norm(r.res[...], r.ln[0:1, :], r.ln[1:2, :])
        r.hn[...] = hn
        r.hnb[...] = hn.astype(BF16)
    st.add(VPU, "ln1", 800, ln1, [f"out_proj{D_MODEL // MXU_N - 1}"])


def _ffn_pieces(st, r):
    ncol = D_MODEL // MXU_N
    nff = D_FF // MXU_N
    for j in range(nff):
        def ffn_in(j=j):
            hb = r.hnb[...]
            gate = _dot(hb, r.w_ffn_in[:, _cols(j)])
            up = _dot(hb, r.w_ffn_in[:, _cols(j, D_FF)])
            r.act[:, _cols(j)] = (_silu(gate) * up).astype(BF16)
        st.add(MXU, f"ffn_in{j}", 640, ffn_in, ["ln1"])

    for j in range(ncol):
        def ffn_out(j=j):
            r.ybuf[:, _cols(j)] = ALPHA * r.hn[:, _cols(j)] + _dot(r.act[...], r.w_ffn_out[:, _cols(j)])
        st.add(MXU, f"ffn_out{j}", 880, ffn_out, [f"ffn_in{nff - 1}", "ln2"])


def _ln2_pieces(st, r, shift_rows):
    def norm(y):
        return _layer_norm(y, r.ln[2:3, :], r.ln[3:4, :])

    def ln2():
        if shift_rows:
            r.out[0:TIME_BLOCK - shift_rows, :] = norm(r.ybuf[shift_rows:TIME_BLOCK, :])
        else:
            r.out[...] = norm(r.ybuf[...])
    st.add(VPU, "ln2", 800, ln2)

    def ln2_tail():
        r.out[TIME_BLOCK - shift_rows:TIME_BLOCK, :] = norm(r.ybuf[0:shift_rows, :])
    if shift_rows:
        st.add(TAIL, "ln2_tail", 50, ln2_tail, [f"ffn_out{D_MODEL // MXU_N - 1}"])


def _assemble_piece(st, r, block, n_tok):
    def assemble():
        keep = TIME_BLOCK - N_META
        row = lax.broadcasted_iota(jnp.int32, (keep, D_MODEL), 0) + (block * TIME_BLOCK + N_META)
        r.hcur[0:N_META, :] = r.xcarry[...]
        r.hcur[N_META:TIME_BLOCK, :] = jnp.where(row < n_tok, r.x[0:keep, :], 0.0)
        r.xcarry[...] = r.x[keep:TIME_BLOCK, :]
    st.add(VPU, "assemble", 100, assemble)


def _in_proj_pieces(st, r):
    def cast():
        r.hb[...] = r.h[...].astype(BF16)
    st.add(VPU, "cast", 100, cast)

    def proj(j):
        return _dot(r.hb[...], r.w_in[:, _cols(j)])

    def p_s5():
        p = proj(0)
        for j in range(NSLAB):
            r.segin[j] = p[:, _slab(j)]
    st.add(PROJ, "p_s5", 320, p_s5, ["cast"])
    return proj


def _in_proj_rest(st, r, proj):
    for j in range(4):
        def p_hg(j=j):
            r.hgp[:, _cols(j)] = proj(4 + j)
        st.add(PROJ, f"p_hg{j}", 320, p_hg)

    def p_lru_x():
        r.lru_xbuf[SUBLANES:SUBLANES + TIME_BLOCK, :] = proj(8)
    st.add(PROJ, "p_lru_x", 320, p_lru_x)

    def p_lru_y():
        p = proj(9)
        for j in range(NSLAB):
            r.segin[NSLAB + j] = p[:, _slab(j)]
    st.add(PROJ, "p_lru_y", 320, p_lru_y)
    for j in range(3):
        def p_sc(j=j):
            r.scp[:, _cols(j)] = proj(1 + j)
        st.add(PROJ, f"p_sc{j}", 320, p_sc)


def _s5_head(st, r):
    def gather():
        for t in range(SEG):
            for j in range(NSLAB):
                r.uperm[_tile(t), _slab(j)] = r.segin[j, _seg_rows(t), :]
        r.bu[...] = _dot(r.uperm[...].astype(BF16), r.bmat[...])
    st.add(S5, "s5_gather", 300, gather, ["p_s5"])


def _s5_pieces(st, r):
    n = S5_W
    env = types.SimpleNamespace()

    def scan(first):
        lam_r = jnp.broadcast_to(r.lam[0:1, 0:n], (SUBLANES, n))
        lam_i = jnp.broadcast_to(r.lam[0:1, n:2 * n], (SUBLANES, n))
        for t in range(first, first + SCAN_PIECE):
            br, bi = r.bu[_tile(t), 0:n], r.bu[_tile(t), n:2 * n]
            if t == 0:
                env.xr, env.xi = br, bi
            else:
                env.xr, env.xi = (lam_r * env.xr - lam_i * env.xi + br,
                                  lam_r * env.xi + lam_i * env.xr + bi)
                r.bu[_tile(t), 0:n] = env.xr
                r.bu[_tile(t), n:2 * n] = env.xi
    for first in range(0, SEG, SCAN_PIECE):
        st.add(S5, f"s5_scan{first}", 22 * SCAN_PIECE, functools.partial(scan, first))

    def carry():
        r.s5_e[:, 0:n] = env.xr
        r.s5_e[:, n:2 * n] = env.xi
        pr, pi_ = r.lam[1:2, 0:n], r.lam[1:2, n:2 * n]
        cr, ci = r.s5_state[0:1, 0:n], r.s5_state[0:1, n:2 * n]
        for s in range(SUBLANES):
            r.s5_cin[s:s + 1, 0:n] = cr
            r.s5_cin[s:s + 1, n:2 * n] = ci
            sr, si = r.s5_e[s:s + 1, 0:n], r.s5_e[s:s + 1, n:2 * n]
            cr, ci = sr + (pr * cr - pi_ * ci), si + (pr * ci + pi_ * cr)
        r.s5_state[0:1, 0:n] = cr
        r.s5_state[0:1, n:2 * n] = ci
    st.add(S5, "s5_carry", 100, carry)

    def fix(first):
        cin_r = r.s5_cin[:, 0:n]
        cin_i = r.s5_cin[:, n:2 * n]
        for tt in range(first // 2, (first + SCAN_PIECE) // 2):
            parts_r, parts_i = [], []
            for t in (2 * tt, 2 * tt + 1):
                qr = jnp.broadcast_to(r.ptab[t:t + 1, 0:n], (SUBLANES, n))
                qi = jnp.broadcast_to(r.ptab[t:t + 1, n:2 * n], (SUBLANES, n))
                parts_r.append(r.bu[_tile(t), 0:n] + (qr * cin_r - qi * cin_i))
                parts_i.append(r.bu[_tile(t), n:2 * n] + (qr * cin_i + qi * cin_r))
            rows2 = slice(tt * 2 * SUBLANES, (tt + 1) * 2 * SUBLANES)
            r.xbf[rows2, 0:n] = jnp.concatenate(parts_r, axis=0).astype(BF16)
            r.xbf[rows2, n:2 * n] = jnp.concatenate(parts_i, axis=0).astype(BF16)
    for first in range(0, SEG, SCAN_PIECE):
        st.add(S5, f"s5_fix{first}", 32 * SCAN_PIECE, functools.partial(fix, first))

    def post():
        y = _dot(r.xbf[...], r.cmat[...])
        y = y + _vrow(r.vec, V_S5_D) * r.uperm[...]
        g = _gelu_tanh(y)
        gate = _sigmoid(_dot(g.astype(BF16), r.glu_w[...]) + _vrow(r.vec, V_GLU_B))
        r.uperm[...] = g * gate
        for t in range(SEG):
            for j in range(NSLAB):
                r.segout[j, _seg_rows(t), :] = r.uperm[_tile(t), _slab(j)]
    st.add(S5, "s5_post", 500, post)


def _short_conv_pieces(st, r):
    def conv():
        tb = TIME_BLOCK
        h, gb, gc = r.scp[:, 0:W_GROUP], r.scp[:, W_GROUP:2 * W_GROUP], r.scp[:, 2 * W_GROUP:3 * W_GROUP]
        r.sc_buf[SUBLANES:SUBLANES + tb, :] = gc * h
        out = (_vrow(r.vec, V_SC_W0) * r.sc_buf[SUBLANES - 2:SUBLANES - 2 + tb, :]
               + _vrow(r.vec, V_SC_W1) * r.sc_buf[SUBLANES - 1:SUBLANES - 1 + tb, :]
               + _vrow(r.vec, V_SC_W2) * r.sc_buf[SUBLANES:SUBLANES + tb, :])
        r.mix[:, W_GROUP:2 * W_GROUP] = (gb * out).astype(BF16)
        r.sc_buf[0:SUBLANES, :] = r.sc_buf[tb:tb + SUBLANES, :]
    st.add(VPU, "sc", 400, conv, ["p_sc2", f"out_proj{D_MODEL // MXU_N - 1}"])


def _hgrn2_pieces(st, r):
    c = HG_CHUNK
    env = types.SimpleNamespace()

    def stack_heads(x_bf):
        return jnp.concatenate([x_bf] * HG_HEADS, axis=0) * r.headmask[...]

    def gates():
        q_in = r.hgp[:, 0:W_GROUP]
        z = r.hgp[:, W_GROUP:2 * W_GROUP]
        q = _silu(q_in) * (HG_HEAD_DIM ** -0.5)
        log_sig = jnp.minimum(z, 0.0) - jnp.log(1.0 + jnp.exp(-jnp.abs(z)))
        cpl = _vrow(r.vec, V_HG_LOG1MLB) + log_sig
        loglb = _vrow(r.vec, V_HG_LOGLB)
        g = jnp.maximum(loglb, cpl) + jnp.log(1.0 + jnp.exp(-jnp.abs(loglb - cpl)))
        k = _vrow(r.vec, V_HG_1MLB) * _sigmoid(-z)
        r.hg_qt[HG_LEVELS + 1] = q.astype(BF16)
        r.hg_kt[HG_LEVELS + 1] = k.astype(BF16)
        env.q, env.k, env.cin, env.sfx = q, k, g, jnp.zeros_like(g)
    st.add(HG, "hg_gates", 600, gates, ["p_hg1"])

    def level(lvl):
        r.hg_qt[lvl] = (env.q * jnp.exp(env.cin)).astype(BF16)
        r.hg_kt[lvl] = (env.k * jnp.exp(env.sfx)).astype(BF16)
        if lvl == HG_LEVELS:
            r.hg_cin[...] = env.cin
            return
        n = 1 << lvl
        row = lax.broadcasted_iota(jnp.int32, (TIME_BLOCK, W_GROUP), 0)
        sib = _sibling(env.cin + env.sfx, n)
        right = (row & n) != 0
        env.cin = env.cin + jnp.where(right, sib, 0.0)
        env.sfx = env.sfx + jnp.where(right, 0.0, sib)
    for lvl in range(HG_LEVELS + 1):
        st.add(HG, f"hg_level{lvl}", 350, functools.partial(level, lvl))

    nchunk = TIME_BLOCK // c

    def chunk_rows(ci):
        return slice(ci * c, (ci + 1) * c)

    def intra(ci):
        rows = chunk_rows(ci)
        lv = r.lv[...]
        v = r.hgp[rows, 2 * W_GROUP:3 * W_GROUP].astype(BF16)
        scores = None
        for lvl in (HG_LEVELS + 1,) + tuple(range(HG_LEVELS)):
            s_n = _dot_nt(stack_heads(r.hg_kt[lvl, rows, :]), r.hg_qt[lvl, rows, :])
            code = HG_LEVELS if lvl == HG_LEVELS + 1 else lvl
            scores = jnp.where(lv == code, s_n, 0.0 if scores is None else scores)
        r.hg_o[rows, :] = _dot_tn(scores.astype(BF16), stack_heads(v))
        r.hg_upd[ci] = _dot_tn(v, r.hg_kt[HG_LEVELS, rows, :]) * r.bdmask[...]
    for ci in range(nchunk):
        st.add(HG, f"hg_intra{ci}", 400, functools.partial(intra, ci), ["p_hg3"])

    def inter(ci):
        rows = chunk_rows(ci)
        state = r.hg_state[...]
        r.hg_o[rows, :] = r.hg_o[rows, :] + _dot_nt(r.hg_qt[HG_LEVELS, rows, :], state.astype(BF16))
        decay = jnp.exp(r.hg_cin[(ci + 1) * c - 1:(ci + 1) * c, :])
        r.hg_state[...] = state * decay + r.hg_upd[ci]
    for ci in range(nchunk):
        st.add(HG, f"hg_inter{ci}", 150, functools.partial(inter, ci))

    def norm():
        o = r.hg_o[...]
        ms = _dot((o * o).astype(BF16), r.ones[...])
        o = o * lax.rsqrt(ms + EPS) * _vrow(r.vec, V_HG_GNORM)
        g_in = r.hgp[:, 3 * W_GROUP:4 * W_GROUP]
        r.mix[:, 2 * W_GROUP:3 * W_GROUP] = (o * _silu(g_in)).astype(BF16)
    st.add(HG, "hg_norm", 300, norm, [f"out_proj{D_MODEL // MXU_N - 1}"])


def _rglru_pieces(st, r):
    tb = TIME_BLOCK
    env = types.SimpleNamespace()

    def seg_tile(ref, first, t):
        return jnp.concatenate([ref[first + j, _seg_rows(t), :] for j in range(NSLAB)], axis=1)

    def pre():
        xc = _vrow(r.vec, V_LRU_CB) + (
            _vrow(r.vec, V_LRU_W0) * r.lru_xbuf[SUBLANES - 3:SUBLANES - 3 + tb, :]
            + _vrow(r.vec, V_LRU_W1) * r.lru_xbuf[SUBLANES - 2:SUBLANES - 2 + tb, :]
            + _vrow(r.vec, V_LRU_W2) * r.lru_xbuf[SUBLANES - 1:SUBLANES - 1 + tb, :]
            + _vrow(r.vec, V_LRU_W3) * r.lru_xbuf[SUBLANES:SUBLANES + tb, :])
        r.lru_xbuf[0:SUBLANES, :] = r.lru_xbuf[tb:tb + SUBLANES, :]
        gates = _dot(xc.astype(BF16), r.wab[...])
        gate_a = _sigmoid(gates[:, 0:W_GROUP] + _vrow(r.vec, V_LRU_BA))
        gate_x = _sigmoid(gates[:, W_GROUP:2 * W_GROUP] + _vrow(r.vec, V_LRU_BX))
        log_a = _vrow(r.vec, V_LRU_CA) * gate_a
        a = jnp.exp(log_a)
        b = xc * gate_x * jnp.sqrt(-jnp.tanh(log_a) * (a * a + 1.0))
        for j in range(NSLAB):
            r.lru_a[j] = a[:, _slab(j)]
            r.lru_b[j] = b[:, _slab(j)]
    st.add(LRU, "lru_pre", 500, pre, ["p_lru_x"])

    def scan(first):
        for t in range(first, first + SCAN_PIECE):
            a_t, b_t = seg_tile(r.lru_a, 0, t), seg_tile(r.lru_b, 0, t)
            env.h, env.p = (b_t, a_t) if t == 0 else (a_t * env.h + b_t, a_t * env.p)
            r.lru_h[_tile(t), :] = env.h
            r.lru_p[_tile(t), :] = env.p
    for first in range(0, SEG, SCAN_PIECE):
        st.add(LRU, f"lru_scan{first}", 18 * SCAN_PIECE, functools.partial(scan, first))

    def carry():
        r.lru_e[0:SUBLANES, :] = env.h
        r.lru_e[SUBLANES:2 * SUBLANES, :] = env.p
        cur = r.lru_state[0:1, :]
        for s in range(SUBLANES):
            r.lru_cin[s:s + 1, :] = cur
            cur = r.lru_e[s:s + 1, :] + r.lru_e[SUBLANES + s:SUBLANES + s + 1, :] * cur
        r.lru_state[0:1, :] = cur
    st.add(LRU, "lru_carry", 50, carry)

    def fix(first):
        cin = r.lru_cin[...]
        for t in range(first, first + SCAN_PIECE):
            res = (r.lru_h[_tile(t), :] + r.lru_p[_tile(t), :] * cin) * _gelu_tanh(seg_tile(r.segin, NSLAB, t))
            for j in range(NSLAB):
                r.segout[NSLAB + j, _seg_rows(t), :] = res[:, _slab(j)]
    for first in range(0, SEG, SCAN_PIECE):
        st.add(LRU, f"lru_fix{first}", 38 * SCAN_PIECE, functools.partial(fix, first), ["p_lru_y"])


_SCRATCH = (
    ("hprev", (TIME_BLOCK, D_MODEL), F32),
    ("mix", (TIME_BLOCK, 4 * W_GROUP), BF16),
    ("res", (TIME_BLOCK, D_MODEL), F32),
    ("ybuf", (TIME_BLOCK, D_MODEL), F32),
    ("hcur", (TIME_BLOCK, D_MODEL), F32),
    ("xcarry", (N_META, D_MODEL), F32),
    ("hn", (TIME_BLOCK, D_MODEL), F32),
    ("hnb", (TIME_BLOCK, D_MODEL), BF16),
    ("act", (TIME_BLOCK, D_FF), BF16),
    ("hb", (TIME_BLOCK, D_MODEL), BF16),
    ("hgp", (TIME_BLOCK, 4 * W_GROUP), F32),
    ("scp", (TIME_BLOCK, 3 * W_GROUP), F32),
    ("segin", (2 * NSLAB, TIME_BLOCK, LANES), F32),
    ("segout", (2 * NSLAB, TIME_BLOCK, LANES), F32),
    ("uperm", (TIME_BLOCK, W_GROUP), F32),
    ("bu", (TIME_BLOCK, 2 * S5_W), F32),
    ("xbf", (TIME_BLOCK, 2 * S5_W), BF16),
    ("s5_e", (SUBLANES, 2 * S5_W), F32),
    ("s5_cin", (SUBLANES, 2 * S5_W), F32),
    ("s5_state", (SUBLANES, 2 * S5_W), F32),
    ("sc_buf", (TIME_BLOCK + SUBLANES, W_GROUP), F32),
    ("hg_qt", (HG_LEVELS + 2, TIME_BLOCK, W_GROUP), BF16),
    ("hg_kt", (HG_LEVELS + 2, TIME_BLOCK, W_GROUP), BF16),
    ("hg_cin", (TIME_BLOCK, W_GROUP), F32),
    ("hg_state", (W_GROUP, W_GROUP), F32),
    ("hg_upd", (TIME_BLOCK // HG_CHUNK, W_GROUP, W_GROUP), F32),
    ("hg_o", (TIME_BLOCK, W_GROUP), F32),
    ("lru_xbuf", (TIME_BLOCK + SUBLANES, W_GROUP), F32),
    ("lru_a", (NSLAB, TIME_BLOCK, LANES), F32),
    ("lru_b", (NSLAB, TIME_BLOCK, LANES), F32),
    ("lru_h", (TIME_BLOCK, W_GROUP), F32),
    ("lru_p", (TIME_BLOCK, W_GROUP), F32),
    ("lru_e", (2 * SUBLANES, W_GROUP), F32),
    ("lru_cin", (SUBLANES, W_GROUP), F32),
    ("lru_state", (SUBLANES, W_GROUP), F32),
)
_CONSTS = ("w_in", "vec", "bmat", "cmat", "lam", "ptab", "glu_w", "wab", "headmask", "bdmask", "ones",
           "lv", "w_out", "ln", "w_ffn_in", "w_ffn_out")


def _layer_kernel(nblk, nsteps, first, last, n_tok, *refs):
    names = (("x", "meta") if first else ("h",)) + _CONSTS + ("out",) + tuple(s[0] for s in _SCRATCH)
    r = types.SimpleNamespace(**dict(zip(names, refs, strict=True)))
    step = pl.program_id(0)
    block = jnp.minimum(step, nsteps - 1) % nblk

    @pl.when(step == 0)
    def _():
        r.hprev[...] = jnp.zeros_like(r.hprev)
        r.mix[...] = jnp.zeros_like(r.mix)
        r.ybuf[...] = jnp.zeros_like(r.ybuf)

    if first:
        r.h = r.hcur

        @pl.when(block == 0)
        def _():
            r.xcarry[...] = r.meta[...]

    @pl.when(step % nblk == 0)
    def _():
        r.s5_state[...] = jnp.zeros_like(r.s5_state)
        r.sc_buf[0:SUBLANES, :] = jnp.zeros((SUBLANES, W_GROUP), F32)
        r.hg_state[...] = jnp.zeros_like(r.hg_state)
        r.lru_xbuf[0:SUBLANES, :] = jnp.zeros((SUBLANES, W_GROUP), F32)
        r.lru_state[...] = jnp.zeros_like(r.lru_state)

    st = _Streams()
    if first:
        _assemble_piece(st, r, block, n_tok)
    proj = _in_proj_pieces(st, r)
    _ln2_pieces(st, r, N_META if last else 0)
    _out_proj_pieces(st, r)
    _in_proj_rest(st, r, proj)
    _ffn_pieces(st, r)
    _s5_head(st, r)
    _ln1_piece(st, r)
    _s5_pieces(st, r)
    _hgrn2_pieces(st, r)
    _rglru_pieces(st, r)
    _short_conv_pieces(st, r)

    def finish():
        for j in range(NSLAB):
            r.mix[:, _slab(j)] = r.segout[j].astype(BF16)
            r.mix[:, _slab(j, 3 * W_GROUP)] = r.segout[NSLAB + j].astype(BF16)
        r.hprev[...] = r.h[...]
    st.add(VPU, "finish", 200, finish,
           [f"out_proj{D_MODEL // MXU_N - 1}", "s5_post", f"lru_fix{SEG - SCAN_PIECE}"])
    st.emit()


def _const_spec(shape):
    return pl.BlockSpec(shape, lambda *_: (0,) * len(shape), pipeline_mode=pl.Buffered(1))


def _layer_call(inputs, consts, bsz, seq, first, last):
    tb = TIME_BLOCK
    n_tok = N_META + seq
    nblk = pl.cdiv(n_tok, tb)
    nsteps = bsz * nblk

    def in_block(n):
        return jnp.minimum(n, nsteps - 1)

    def out_block(n):
        return jnp.maximum(n - 2, 0)
    if first:
        in_specs = [pl.BlockSpec((None, tb, D_MODEL), lambda n: (in_block(n) // nblk, in_block(n) % nblk, 0)),
                    _const_spec((N_META, D_MODEL))]
    else:
        in_specs = [pl.BlockSpec((tb, D_MODEL), lambda n: (in_block(n), 0))]
    if last:
        out_spec = pl.BlockSpec((None, tb, D_MODEL), lambda n: (out_block(n) // nblk, out_block(n) % nblk, 0))
        out_shape = jax.ShapeDtypeStruct((bsz, seq, D_MODEL), F32)
    else:
        out_spec = pl.BlockSpec((tb, D_MODEL), lambda n: (out_block(n), 0))
        out_shape = jax.ShapeDtypeStruct((nsteps * tb, D_MODEL), F32)
    return pl.pallas_call(
        functools.partial(_layer_kernel, nblk, nsteps, first, last, n_tok),
        grid=(nsteps + 2,),
        in_specs=in_specs + [_const_spec(c.shape) for c in consts],
        out_specs=out_spec,
        out_shape=out_shape,
        scratch_shapes=[pltpu.VMEM(shape, dtype) for _, shape, dtype in _SCRATCH],
        compiler_params=pltpu.CompilerParams(
            dimension_semantics=("arbitrary",), vmem_limit_bytes=VMEM_LIMIT_BYTES),
    )(*inputs, *consts)


def _block_diag(blocks):
    n, a, b = blocks.shape
    eye = jnp.eye(n, dtype=blocks.dtype)
    return jnp.einsum('nab,nm->namb', blocks, eye).reshape(n * a, n * b)


def _hgrn2_level_map():
    t = jnp.arange(HG_CHUNK, dtype=jnp.int32)[:, None]
    s = jnp.arange(HG_CHUNK, dtype=jnp.int32)[None, :]
    x = t ^ s
    lv = jnp.full((HG_CHUNK, HG_CHUNK), -1, jnp.int32)
    for level in range(HG_LEVELS):
        n = 1 << level
        lv = jnp.where((x >= n) & (x < 2 * n) & (t > s), level, lv)
    lv = jnp.where(t == s, HG_LEVELS, lv)
    return jnp.tile(lv.T, (HG_HEADS, 1))


def _prepare_layer(l, lb, w_in, s5_lam_re, s5_lam_im, s5_b_re, s5_b_im, s5_c_re, s5_c_im, s5_d,
                   s5_log_dt, s5_glu_w, s5_glu_b, sc_conv_w, hg_gnorm, lru_conv_w, lru_conv_b,
                   lru_wa, lru_ba, lru_wx, lru_bx, lru_a_param):
    lam_r, lam_i = s5_lam_re[l].astype(F32), s5_lam_im[l].astype(F32)
    dt = jnp.exp(s5_log_dt[l].astype(F32))[:, None]
    arg_r, arg_i = lam_r * dt, lam_i * dt

    def lam_bar_pow(k):
        mag = jnp.exp(arg_r * k)
        return mag * jnp.cos(arg_i * k), mag * jnp.sin(arg_i * k)
    bar_r, bar_i = lam_bar_pow(1.0)
    num_r, num_i = bar_r - 1.0, bar_i
    den = lam_r * lam_r + lam_i * lam_i
    coef_r = ((num_r * lam_r + num_i * lam_i) / den)[..., None]
    coef_i = ((num_i * lam_r - num_r * lam_i) / den)[..., None]
    b_r, b_i = s5_b_re[l].astype(F32), s5_b_im[l].astype(F32)
    bbar_r = jnp.transpose(coef_r * b_r - coef_i * b_i, (0, 2, 1))
    bbar_i = jnp.transpose(coef_r * b_i + coef_i * b_r, (0, 2, 1))
    bmat = jnp.concatenate([_block_diag(bbar_r), _block_diag(bbar_i)], axis=1)
    c_r = jnp.transpose(s5_c_re[l].astype(F32), (0, 2, 1))
    c_i = jnp.transpose(s5_c_im[l].astype(F32), (0, 2, 1))
    cmat = jnp.concatenate([_block_diag(c_r), -_block_diag(c_i)], axis=0)

    seg_r, seg_i = lam_bar_pow(float(SEG))
    lam_rows = jnp.stack([jnp.concatenate([bar_r.reshape(-1), bar_i.reshape(-1)]),
                          jnp.concatenate([seg_r.reshape(-1), seg_i.reshape(-1)])])
    steps = jnp.arange(1, SEG + 1, dtype=F32)[:, None, None]
    pw_r, pw_i = lam_bar_pow(steps)
    ptab = jnp.concatenate([pw_r.reshape(SEG, -1), pw_i.reshape(SEG, -1)], axis=1)

    one_m_lb = 1.0 - lb
    rows = [None] * N_VEC
    rows[V_S5_D] = s5_d[l]
    rows[V_GLU_B] = s5_glu_b[l]
    rows[V_SC_W0], rows[V_SC_W1], rows[V_SC_W2] = sc_conv_w[l, 0], sc_conv_w[l, 1], sc_conv_w[l, 2]
    rows[V_HG_LOGLB] = jnp.maximum(jnp.log(lb), -1e30)
    rows[V_HG_LOG1MLB] = jnp.log1p(-lb)
    rows[V_HG_1MLB] = one_m_lb
    rows[V_HG_GNORM] = hg_gnorm[l]
    for i, v in enumerate((V_LRU_W0, V_LRU_W1, V_LRU_W2, V_LRU_W3)):
        rows[v] = lru_conv_w[l, i]
    rows[V_LRU_CB] = lru_conv_b[l]
    rows[V_LRU_CA] = -LRU_C * jax.nn.softplus(-lru_a_param[l].astype(F32))
    rows[V_LRU_BA] = lru_ba[l]
    rows[V_LRU_BX] = lru_bx[l]
    zero = jnp.zeros((W_GROUP,), F32)
    vec = jnp.stack([zero if r is None else r.astype(F32) for r in rows])
    wab = jnp.concatenate([_block_diag(lru_wa[l].astype(F32)), _block_diag(lru_wx[l].astype(F32))], axis=1)
    return (w_in[l].astype(BF16), vec, bmat.astype(BF16), cmat.astype(BF16), lam_rows, ptab,
            s5_glu_w[l].astype(BF16), wab.astype(BF16))


def kernel(x, meta_tokens, hg_lb_raw, w_in, w_out, s5_lam_re, s5_lam_im, s5_b_re, s5_b_im, s5_c_re, s5_c_im, s5_d, s5_log_dt, s5_glu_w, s5_glu_b, sc_conv_w, hg_gnorm, lru_conv_w, lru_conv_b, lru_wa, lru_ba, lru_wx, lru_bx, lru_a_param, ln1_g, ln1_b, w_ffn_in, w_ffn_out, ln2_g, ln2_b):
    bsz, seq, d = x.shape
    assert d == D_MODEL
    inputs = (x.astype(F32), meta_tokens.astype(F32))

    lb_all = jnp.cumsum(jax.nn.softmax(hg_lb_raw.astype(F32), axis=0), axis=0)
    lb_all = lb_all - lb_all[0:1]

    head = jnp.arange(W_GROUP) // HG_HEAD_DIM
    same_head = head[:, None] == head[None, :]
    bdmask = same_head.astype(F32)
    headmask = same_head.astype(BF16)
    ones = (same_head.astype(F32) / HG_HEAD_DIM).astype(BF16)
    lv = _hgrn2_level_map()

    for l in range(DEPTH):
        prep = _prepare_layer(l, lb_all[l], w_in, s5_lam_re, s5_lam_im, s5_b_re, s5_b_im, s5_c_re, s5_c_im,
                              s5_d, s5_log_dt, s5_glu_w, s5_glu_b, sc_conv_w, hg_gnorm, lru_conv_w,
                              lru_conv_b, lru_wa, lru_ba, lru_wx, lru_bx, lru_a_param)
        ln = jnp.stack([ln1_g[l], ln1_b[l], ln2_g[l], ln2_b[l]]).astype(F32)
        consts = prep + (headmask, bdmask, ones, lv, w_out[l].astype(BF16), ln,
                         w_ffn_in[l].astype(BF16), w_ffn_out[l].astype(BF16))
        inputs = (_layer_call(inputs, consts, bsz, seq, first=l == 0, last=l == DEPTH - 1),)
    return inputs[0]
```

```python
import functools
import math
import types

import jax
import jax.numpy as jnp
from jax import lax
from jax.experimental import pallas as pl
from jax.experimental.pallas import tpu as pltpu

F32 = jnp.float32
BF16 = jnp.bfloat16

D_MODEL = 1024
DEPTH = 2
N_META = 16
W_GROUP = 256
N_IN = 10 * W_GROUP
S5_GROUP = 16
S5_NGROUPS = 16
S5_STATE = 64
S5_W = S5_NGROUPS * S5_STATE
HG_HEADS = 4
HG_HEAD_DIM = 64
LRU_HEADS = 4
LRU_C = 8.0
D_FF = 2816
ALPHA = (2 * DEPTH) ** 0.25
EPS = 1e-5

SUBLANES = 8
LANES = 128
MXU_N = 256
NSLAB = W_GROUP // LANES
TIME_BLOCK = 320
SEG = TIME_BLOCK // SUBLANES
SCAN_PIECE = 8
PIPELINE_DEPTH = 2
HG_CHUNK = 64
HG_LEVELS = 6
VMEM_LIMIT_BYTES = 60 * 1024 * 1024

(V_S5_D, V_GLU_B, V_SC_W0, V_SC_W1, V_SC_W2, V_HG_LOGLB, V_HG_LOG1MLB, V_HG_1MLB, V_HG_GNORM,
 V_LRU_W0, V_LRU_W1, V_LRU_W2, V_LRU_W3, V_LRU_CB, V_LRU_CA, V_LRU_BA, V_LRU_BX) = range(17)
N_VEC = 24


def _sigmoid(x):
    return 0.5 + 0.5 * jnp.tanh(0.5 * x)


def _silu(x):
    h = 0.5 * x
    return h + h * jnp.tanh(h)


def _gelu_tanh(x):
    c = math.sqrt(2.0 / math.pi)
    return x * (0.5 * (1.0 + jnp.tanh(c * (x + 0.044715 * (x * x * x)))))


def _dot(a, b):
    return jnp.dot(a, b, preferred_element_type=F32)


def _dot_nt(a, b):
    return lax.dot_general(a, b, (((1,), (1,)), ((), ())), preferred_element_type=F32)


def _dot_tn(a, b):
    return lax.dot_general(a, b, (((0,), (0,)), ((), ())), preferred_element_type=F32)


def _tile(r):
    return slice(r * SUBLANES, (r + 1) * SUBLANES)


def _slab(j, first=0):
    return slice(first + j * LANES, first + (j + 1) * LANES)


def _cols(j, first=0):
    return slice(first + j * MXU_N, first + (j + 1) * MXU_N)


def _seg_rows(r):
    return pl.ds(r, SUBLANES, stride=SEG)


def _vrow(vec_ref, i):
    return vec_ref[i:i + 1, :]


def _layer_norm(x, g, b):
    mu = jnp.mean(x, axis=-1, keepdims=True)
    xc = x - mu
    var = jnp.mean(xc * xc, axis=-1, keepdims=True)
    return xc * lax.rsqrt(var + EPS) * g + b


def _sibling(x, n):
    c, w = x.shape
    if n < SUBLANES:
        x3 = x.reshape(c // SUBLANES, SUBLANES, w)
        fwd = pltpu.roll(x3, n, 1)
        if 2 * n == SUBLANES:
            return fwd.reshape(c, w)
        bwd = pltpu.roll(x3, SUBLANES - n, 1)
        row = lax.broadcasted_iota(jnp.int32, x3.shape, 1)
        return jnp.where((row & n) != 0, fwd, bwd).reshape(c, w)
    x4 = x.reshape(c // (2 * n), 2, n, w)
    return jnp.concatenate([x4[:, 1:2], x4[:, 0:1]], axis=1).reshape(c, w)


class _Streams:
    def __init__(self):
        self.lanes = {}

    def add(self, lane, name, cost, fn, deps=()):
        self.lanes.setdefault(lane, []).append((name, cost, tuple(deps), fn))

    def emit(self):
        total = {k: sum(p[1] for p in v) for k, v in self.lanes.items()}
        pos = dict.fromkeys(self.lanes, 0)
        spent = dict.fromkeys(self.lanes, 0)
        done = set()

        def ready(k):
            return pos[k] < len(self.lanes[k]) and all(d in done for d in self.lanes[k][pos[k]][2])
        while any(pos[k] < len(v) for k, v in self.lanes.items()):
            order = sorted(self.lanes, key=lambda k: spent[k] / total[k] * LANE_LAG.get(k, 1.0))
            pick = next((k for k in order if ready(k)), None)
            assert pick is not None, "piece dependencies cannot be met"
            name, cost, _, fn = self.lanes[pick][pos[pick]]
            fn()
            done.add(name)
            pos[pick] += 1
            spent[pick] += cost


MXU, PROJ, VPU, S5, HG, LRU, TAIL = "mxu", "proj", "misc", "s5", "hg", "lru", "tail"
LANE_LAG = {MXU: 0.8, PROJ: 0.3}


def _out_proj_pieces(st, r):
    for j in range(D_MODEL // MXU_N):
        def out_proj(j=j):
            r.res[:, _cols(j)] = ALPHA * r.hprev[:, _cols(j)] + _dot(r.mix[...], r.w_out[:, _cols(j)])
        st.add(MXU, f"out_proj{j}", 320, out_proj)


def _ln1_piece(st, r):
    def ln1():
        hn = _layer_norm(r.res[...], r.ln[0:1, :], r.ln[1:2, :])
        r.hn[...] = hn
        r.hnb[...] = hn.astype(BF16)
    st.add(VPU, "ln1", 800, ln1, [f"out_proj{D_MODEL // MXU_N - 1}"])


def _ffn_pieces(st, r):
    ncol = D_MODEL // MXU_N
    nff = D_FF // MXU_N
    for j in range(nff):
        def ffn_in(j=j):
            hb = r.hnb[...]
            gate = _dot(hb, r.w_ffn_in[:, _cols(j)])
            up = _dot(hb, r.w_ffn_in[:, _cols(j, D_FF)])
            r.act[:, _cols(j)] = (_silu(gate) * up).astype(BF16)
        st.add(MXU, f"ffn_in{j}", 640, ffn_in, ["ln1"])

    for j in range(ncol):
        def ffn_out(j=j):
            r.ybuf[:, _cols(j)] = ALPHA * r.hn[:, _cols(j)] + _dot(r.act[...], r.w_ffn_out[:, _cols(j)])
        st.add(MXU, f"ffn_out{j}", 880, ffn_out, [f"ffn_in{nff - 1}", "ln2"])


def _ln2_pieces(st, r, shift_rows):
    def norm(y):
        return _layer_norm(y, r.ln[2:3, :], r.ln[3:4, :])

    def ln2():
        if shift_rows:
            r.out[0:TIME_BLOCK - shift_rows, :] = norm(r.ybuf[shift_rows:TIME_BLOCK, :])
        else:
            r.out[...] = norm(r.ybuf[...])
    st.add(VPU, "ln2", 800, ln2)

    def ln2_tail():
        r.out[TIME_BLOCK - shift_rows:TIME_BLOCK, :] = norm(r.ybuf[0:shift_rows, :])
    if shift_rows:
        st.add(TAIL, "ln2_tail", 50, ln2_tail, [f"ffn_out{D_MODEL // MXU_N - 1}"])


def _assemble_piece(st, r, block, n_tok):
    def assemble():
        keep = TIME_BLOCK - N_META
        row = lax.broadcasted_iota(jnp.int32, (keep, D_MODEL), 0) + (block * TIME_BLOCK + N_META)
        r.hcur[0:N_META, :] = r.xcarry[...]
        r.hcur[N_META:TIME_BLOCK, :] = jnp.where(row < n_tok, r.x[0:keep, :], 0.0)
        r.xcarry[...] = r.x[keep:TIME_BLOCK, :]
    st.add(VPU, "assemble", 100, assemble)


def _in_proj_pieces(st, r):
    def cast():
        r.hb[...] = r.h[...].astype(BF16)
    st.add(VPU, "cast", 100, cast)

    def proj(j):
        return _dot(r.hb[...], r.w_in[:, _cols(j)])

    def p_s5():
        p = proj(0)
        for j in range(NSLAB):
            r.segin[j] = p[:, _slab(j)]
    st.add(PROJ, "p_s5", 320, p_s5, ["cast"])
    return proj


def _in_proj_rest(st, r, proj):
    for j in range(4):
        def p_hg(j=j):
            r.hgp[:, _cols(j)] = proj(4 + j)
        st.add(PROJ, f"p_hg{j}", 320, p_hg)

    def p_lru_x():
        r.lru_xbuf[SUBLANES:SUBLANES + TIME_BLOCK, :] = proj(8)
    st.add(PROJ, "p_lru_x", 320, p_lru_x)

    def p_lru_y():
        p = proj(9)
        for j in range(NSLAB):
            r.segin[NSLAB + j] = p[:, _slab(j)]
    st.add(PROJ, "p_lru_y", 320, p_lru_y)
    for j in range(3):
        def p_sc(j=j):
            r.scp[:, _cols(j)] = proj(1 + j)
        st.add(PROJ, f"p_sc{j}", 320, p_sc)


def _s5_head(st, r):
    def gather():
        for t in range(SEG):
            for j in range(NSLAB):
                r.uperm[_tile(t), _slab(j)] = r.segin[j, _seg_rows(t), :]
        r.bu[...] = _dot(r.uperm[...].astype(BF16), r.bmat[...])
    st.add(S5, "s5_gather", 300, gather, ["p_s5"])


def _s5_pieces(st, r):
    n = S5_W
    env = types.SimpleNamespace()

    def scan(first):
        lam_r = jnp.broadcast_to(r.lam[0:1, 0:n], (SUBLANES, n))
        lam_i = jnp.broadcast_to(r.lam[0:1, n:2 * n], (SUBLANES, n))
        for t in range(first, first + SCAN_PIECE):
            br, bi = r.bu[_tile(t), 0:n], r.bu[_tile(t), n:2 * n]
            if t == 0:
                env.xr, env.xi = br, bi
            else:
                env.xr, env.xi = (lam_r * env.xr - lam_i * env.xi + br,
                                  lam_r * env.xi + lam_i * env.xr + bi)
                r.bu[_tile(t), 0:n] = env.xr
                r.bu[_tile(t), n:2 * n] = env.xi
    for first in range(0, SEG, SCAN_PIECE):
        st.add(S5, f"s5_scan{first}", 22 * SCAN_PIECE, functools.partial(scan, first))

    def carry():
        r.s5_e[:, 0:n] = env.xr
        r.s5_e[:, n:2 * n] = env.xi
        pr, pi_ = r.lam[1:2, 0:n], r.lam[1:2, n:2 * n]
        cr, ci = r.s5_state[0:1, 0:n], r.s5_state[0:1, n:2 * n]
        for s in range(SUBLANES):
            r.s5_cin[s:s + 1, 0:n] = cr
            r.s5_cin[s:s + 1, n:2 * n] = ci
            sr, si = r.s5_e[s:s + 1, 0:n], r.s5_e[s:s + 1, n:2 * n]
            cr, ci = sr + (pr * cr - pi_ * ci), si + (pr * ci + pi_ * cr)
        r.s5_state[0:1, 0:n] = cr
        r.s5_state[0:1, n:2 * n] = ci
    st.add(S5, "s5_carry", 100, carry)

    def fix(first):
        cin_r = r.s5_cin[:, 0:n]
        cin_i = r.s5_cin[:, n:2 * n]
        for tt in range(first // 2, (first + SCAN_PIECE) // 2):
            parts_r, parts_i = [], []
            for t in (2 * tt, 2 * tt + 1):
                qr = jnp.broadcast_to(r.ptab[t:t + 1, 0:n], (SUBLANES, n))
                qi = jnp.broadcast_to(r.ptab[t:t + 1, n:2 * n], (SUBLANES, n))
                parts_r.append(r.bu[_tile(t), 0:n] + (qr * cin_r - qi * cin_i))
                parts_i.append(r.bu[_tile(t), n:2 * n] + (qr * cin_i + qi * cin_r))
            rows2 = slice(tt * 2 * SUBLANES, (tt + 1) * 2 * SUBLANES)
            r.xbf[rows2, 0:n] = jnp.concatenate(parts_r, axis=0).astype(BF16)
            r.xbf[rows2, n:2 * n] = jnp.concatenate(parts_i, axis=0).astype(BF16)
    for first in range(0, SEG, SCAN_PIECE):
        st.add(S5, f"s5_fix{first}", 32 * SCAN_PIECE, functools.partial(fix, first))

    def post():
        y = _dot(r.xbf[...], r.cmat[...])
        y = y + _vrow(r.vec, V_S5_D) * r.uperm[...]
        g = _gelu_tanh(y)
        gate = _sigmoid(_dot(g.astype(BF16), r.glu_w[...]) + _vrow(r.vec, V_GLU_B))
        r.uperm[...] = g * gate
        for t in range(SEG):
            for j in range(NSLAB):
                r.segout[j, _seg_rows(t), :] = r.uperm[_tile(t), _slab(j)]
    st.add(S5, "s5_post", 500, post)


def _short_conv_pieces(st, r):
    def conv():
        tb = TIME_BLOCK
        h, gb, gc = r.scp[:, 0:W_GROUP], r.scp[:, W_GROUP:2 * W_GROUP], r.scp[:, 2 * W_GROUP:3 * W_GROUP]
        r.sc_buf[SUBLANES:SUBLANES + tb, :] = gc * h
        out = (_vrow(r.vec, V_SC_W0) * r.sc_buf[SUBLANES - 2:SUBLANES - 2 + tb, :]
               + _vrow(r.vec, V_SC_W1) * r.sc_buf[SUBLANES - 1:SUBLANES - 1 + tb, :]
               + _vrow(r.vec, V_SC_W2) * r.sc_buf[SUBLANES:SUBLANES + tb, :])
        r.mix[:, W_GROUP:2 * W_GROUP] = (gb * out).astype(BF16)
        r.sc_buf[0:SUBLANES, :] = r.sc_buf[tb:tb + SUBLANES, :]
    st.add(VPU, "sc", 400, conv, ["p_sc2", f"out_proj{D_MODEL // MXU_N - 1}"])


def _hgrn2_pieces(st, r):
    c = HG_CHUNK
    env = types.SimpleNamespace()

    def stack_heads(x_bf):
        return jnp.concatenate([x_bf] * HG_HEADS, axis=0) * r.headmask[...]

    def gates():
        q_in = r.hgp[:, 0:W_GROUP]
        z = r.hgp[:, W_GROUP:2 * W_GROUP]
        q = _silu(q_in) * (HG_HEAD_DIM ** -0.5)
        log_sig = jnp.minimum(z, 0.0) - jnp.log(1.0 + jnp.exp(-jnp.abs(z)))
        cpl = _vrow(r.vec, V_HG_LOG1MLB) + log_sig
        loglb = _vrow(r.vec, V_HG_LOGLB)
        g = jnp.maximum(loglb, cpl) + jnp.log(1.0 + jnp.exp(-jnp.abs(loglb - cpl)))
        k = _vrow(r.vec, V_HG_1MLB) * _sigmoid(-z)
        r.hg_qt[HG_LEVELS + 1] = q.astype(BF16)
        r.hg_kt[HG_LEVELS + 1] = k.astype(BF16)
        env.q, env.k, env.cin, env.sfx = q, k, g, jnp.zeros_like(g)
    st.add(HG, "hg_gates", 600, gates, ["p_hg1"])

    def level(lvl):
        r.hg_qt[lvl] = (env.q * jnp.exp(env.cin)).astype(BF16)
        r.hg_kt[lvl] = (env.k * jnp.exp(env.sfx)).astype(BF16)
        if lvl == HG_LEVELS:
            r.hg_cin[...] = env.cin
            return
        n = 1 << lvl
        row = lax.broadcasted_iota(jnp.int32, (TIME_BLOCK, W_GROUP), 0)
        sib = _sibling(env.cin + env.sfx, n)
        right = (row & n) != 0
        env.cin = env.cin + jnp.where(right, sib, 0.0)
        env.sfx = env.sfx + jnp.where(right, 0.0, sib)
    for lvl in range(HG_LEVELS + 1):
        st.add(HG, f"hg_level{lvl}", 350, functools.partial(level, lvl))

    nchunk = TIME_BLOCK // c

    def chunk_rows(ci):
        return slice(ci * c, (ci + 1) * c)

    def intra(ci):
        rows = chunk_rows(ci)
        lv = r.lv[...]
        v = r.hgp[rows, 2 * W_GROUP:3 * W_GROUP].astype(BF16)
        scores = None
        for lvl in (HG_LEVELS + 1,) + tuple(range(HG_LEVELS)):
            s_n = _dot_nt(stack_heads(r.hg_kt[lvl, rows, :]), r.hg_qt[lvl, rows, :])
            code = HG_LEVELS if lvl == HG_LEVELS + 1 else lvl
            scores = jnp.where(lv == code, s_n, 0.0 if scores is None else scores)
        r.hg_o[rows, :] = _dot_tn(scores.astype(BF16), stack_heads(v))
        r.hg_upd[ci] = _dot_tn(v, r.hg_kt[HG_LEVELS, rows, :]) * r.bdmask[...]
    for ci in range(nchunk):
        st.add(HG, f"hg_intra{ci}", 400, functools.partial(intra, ci), ["p_hg3"])

    def inter(ci):
        rows = chunk_rows(ci)
        state = r.hg_state[...]
        r.hg_o[rows, :] = r.hg_o[rows, :] + _dot_nt(r.hg_qt[HG_LEVELS, rows, :], state.astype(BF16))
        decay = jnp.exp(r.hg_cin[(ci + 1) * c - 1:(ci + 1) * c, :])
        r.hg_state[...] = state * decay + r.hg_upd[ci]
    for ci in range(nchunk):
        st.add(HG, f"hg_inter{ci}", 150, functools.partial(inter, ci))

    def norm():
        o = r.hg_o[...]
        ms = _dot((o * o).astype(BF16), r.ones[...])
        o = o * lax.rsqrt(ms + EPS) * _vrow(r.vec, V_HG_GNORM)
        g_in = r.hgp[:, 3 * W_GROUP:4 * W_GROUP]
        r.mix[:, 2 * W_GROUP:3 * W_GROUP] = (o * _silu(g_in)).astype(BF16)
    st.add(HG, "hg_norm", 300, norm, [f"out_proj{D_MODEL // MXU_N - 1}"])


def _rglru_pieces(st, r):
    tb = TIME_BLOCK
    env = types.SimpleNamespace()

    def seg_tile(ref, first, t):
        return jnp.concatenate([ref[first + j, _seg_rows(t), :] for j in range(NSLAB)], axis=1)

    def pre():
        xc = _vrow(r.vec, V_LRU_CB) + (
            _vrow(r.vec, V_LRU_W0) * r.lru_xbuf[SUBLANES - 3:SUBLANES - 3 + tb, :]
            + _vrow(r.vec, V_LRU_W1) * r.lru_xbuf[SUBLANES - 2:SUBLANES - 2 + tb, :]
            + _vrow(r.vec, V_LRU_W2) * r.lru_xbuf[SUBLANES - 1:SUBLANES - 1 + tb, :]
            + _vrow(r.vec, V_LRU_W3) * r.lru_xbuf[SUBLANES:SUBLANES + tb, :])
        r.lru_xbuf[0:SUBLANES, :] = r.lru_xbuf[tb:tb + SUBLANES, :]
        gates = _dot(xc.astype(BF16), r.wab[...])
        gate_a = _sigmoid(gates[:, 0:W_GROUP] + _vrow(r.vec, V_LRU_BA))
        gate_x = _sigmoid(gates[:, W_GROUP:2 * W_GROUP] + _vrow(r.vec, V_LRU_BX))
        log_a = _vrow(r.vec, V_LRU_CA) * gate_a
        a = jnp.exp(log_a)
        b = xc * gate_x * jnp.sqrt(-jnp.tanh(log_a) * (a * a + 1.0))
        for j in range(NSLAB):
            r.lru_a[j] = a[:, _slab(j)]
            r.lru_b[j] = b[:, _slab(j)]
    st.add(LRU, "lru_pre", 500, pre, ["p_lru_x"])

    def scan(first):
        for t in range(first, first + SCAN_PIECE):
            a_t, b_t = seg_tile(r.lru_a, 0, t), seg_tile(r.lru_b, 0, t)
            env.h, env.p = (b_t, a_t) if t == 0 else (a_t * env.h + b_t, a_t * env.p)
            r.lru_h[_tile(t), :] = env.h
            r.lru_p[_tile(t), :] = env.p
    for first in range(0, SEG, SCAN_PIECE):
        st.add(LRU, f"lru_scan{first}", 18 * SCAN_PIECE, functools.partial(scan, first))

    def carry():
        r.lru_e[0:SUBLANES, :] = env.h
        r.lru_e[SUBLANES:2 * SUBLANES, :] = env.p
        cur = r.lru_state[0:1, :]
        for s in range(SUBLANES):
            r.lru_cin[s:s + 1, :] = cur
            cur = r.lru_e[s:s + 1, :] + r.lru_e[SUBLANES + s:SUBLANES + s + 1, :] * cur
        r.lru_state[0:1, :] = cur
    st.add(LRU, "lru_carry", 50, carry)

    def fix(first):
        cin = r.lru_cin[...]
        for t in range(first, first + SCAN_PIECE):
            res = (r.lru_h[_tile(t), :] + r.lru_p[_tile(t), :] * cin) * _gelu_tanh(seg_tile(r.segin, NSLAB, t))
            for j in range(NSLAB):
                r.segout[NSLAB + j, _seg_rows(t), :] = res[:, _slab(j)]
    for first in range(0, SEG, SCAN_PIECE):
        st.add(LRU, f"lru_fix{first}", 38 * SCAN_PIECE, functools.partial(fix, first), ["p_lru_y"])


_SCRATCH = (
    ("hprev", (TIME_BLOCK, D_MODEL), F32),
    ("mix", (TIME_BLOCK, 4 * W_GROUP), BF16),
    ("res", (TIME_BLOCK, D_MODEL), F32),
    ("ybuf", (TIME_BLOCK, D_MODEL), F32),
    ("hcur", (TIME_BLOCK, D_MODEL), F32),
    ("xcarry", (N_META, D_MODEL), F32),
    ("hn", (TIME_BLOCK, D_MODEL), F32),
    ("hnb", (TIME_BLOCK, D_MODEL), BF16),
    ("act", (TIME_BLOCK, D_FF), BF16),
    ("hb", (TIME_BLOCK, D_MODEL), BF16),
    ("hgp", (TIME_BLOCK, 4 * W_GROUP), F32),
    ("scp", (TIME_BLOCK, 3 * W_GROUP), F32),
    ("segin", (2 * NSLAB, TIME_BLOCK, LANES), F32),
    ("segout", (2 * NSLAB, TIME_BLOCK, LANES), F32),
    ("uperm", (TIME_BLOCK, W_GROUP), F32),
    ("bu", (TIME_BLOCK, 2 * S5_W), F32),
    ("xbf", (TIME_BLOCK, 2 * S5_W), BF16),
    ("s5_e", (SUBLANES, 2 * S5_W), F32),
    ("s5_cin", (SUBLANES, 2 * S5_W), F32),
    ("s5_state", (SUBLANES, 2 * S5_W), F32),
    ("sc_buf", (TIME_BLOCK + SUBLANES, W_GROUP), F32),
    ("hg_qt", (HG_LEVELS + 2, TIME_BLOCK, W_GROUP), BF16),
    ("hg_kt", (HG_LEVELS + 2, TIME_BLOCK, W_GROUP), BF16),
    ("hg_cin", (TIME_BLOCK, W_GROUP), F32),
    ("hg_state", (W_GROUP, W_GROUP), F32),
    ("hg_upd", (TIME_BLOCK // HG_CHUNK, W_GROUP, W_GROUP), F32),
    ("hg_o", (TIME_BLOCK, W_GROUP), F32),
    ("lru_xbuf", (TIME_BLOCK + SUBLANES, W_GROUP), F32),
    ("lru_a", (NSLAB, TIME_BLOCK, LANES), F32),
    ("lru_b", (NSLAB, TIME_BLOCK, LANES), F32),
    ("lru_h", (TIME_BLOCK, W_GROUP), F32),
    ("lru_p", (TIME_BLOCK, W_GROUP), F32),
    ("lru_e", (2 * SUBLANES, W_GROUP), F32),
    ("lru_cin", (SUBLANES, W_GROUP), F32),
    ("lru_state", (SUBLANES, W_GROUP), F32),
)
_CONSTS = ("w_in", "vec", "bmat", "cmat", "lam", "ptab", "glu_w", "wab", "headmask", "bdmask", "ones",
           "lv", "w_out", "ln", "w_ffn_in", "w_ffn_out")


def _layer_kernel(nblk, nsteps, first, last, n_tok, *refs):
    names = (("x", "meta") if first else ("h",)) + _CONSTS + ("out",) + tuple(s[0] for s in _SCRATCH)
    r = types.SimpleNamespace(**dict(zip(names, refs, strict=True)))
    step = pl.program_id(0)
    block = jnp.minimum(step, nsteps - 1) % nblk

    @pl.when(step == 0)
    def _():
        r.hprev[...] = jnp.zeros_like(r.hprev)
        r.mix[...] = jnp.zeros_like(r.mix)
        r.ybuf[...] = jnp.zeros_like(r.ybuf)

    if first:
        r.h = r.hcur

        @pl.when(block == 0)
        def _():
            r.xcarry[...] = r.meta[...]

    @pl.when(step % nblk == 0)
    def _():
        r.s5_state[...] = jnp.zeros_like(r.s5_state)
        r.sc_buf[0:SUBLANES, :] = jnp.zeros((SUBLANES, W_GROUP), F32)
        r.hg_state[...] = jnp.zeros_like(r.hg_state)
        r.lru_xbuf[0:SUBLANES, :] = jnp.zeros((SUBLANES, W_GROUP), F32)
        r.lru_state[...] = jnp.zeros_like(r.lru_state)

    st = _Streams()
    if first:
        _assemble_piece(st, r, block, n_tok)
    proj = _in_proj_pieces(st, r)
    _ln2_pieces(st, r, N_META if last else 0)
    _out_proj_pieces(st, r)
    _in_proj_rest(st, r, proj)
    _ffn_pieces(st, r)
    _s5_head(st, r)
    _ln1_piece(st, r)
    _s5_pieces(st, r)
    _hgrn2_pieces(st, r)
    _rglru_pieces(st, r)
    _short_conv_pieces(st, r)

    def finish():
        for j in range(NSLAB):
            r.mix[:, _slab(j)] = r.segout[j].astype(BF16)
            r.mix[:, _slab(j, 3 * W_GROUP)] = r.segout[NSLAB + j].astype(BF16)
        r.hprev[...] = r.h[...]
    st.add(VPU, "finish", 200, finish,
           [f"out_proj{D_MODEL // MXU_N - 1}", "s5_post", f"lru_fix{SEG - SCAN_PIECE}"])
    st.emit()


def _const_spec(shape):
    return pl.BlockSpec(shape, lambda *_: (0,) * len(shape), pipeline_mode=pl.Buffered(1))


def _layer_spec(shape, layer):
    return pl.BlockSpec((None,) + shape[1:], lambda *_: (layer, 0, 0), pipeline_mode=pl.Buffered(1))


def _layer_call(inputs, consts, bsz, seq, layer, first, last):
    tb = TIME_BLOCK
    n_tok = N_META + seq
    nblk = pl.cdiv(n_tok, tb)
    nsteps = bsz * nblk

    def in_block(n):
        return jnp.minimum(n, nsteps - 1)

    def out_block(n):
        return jnp.maximum(n - PIPELINE_DEPTH, 0)
    if first:
        in_specs = [pl.BlockSpec((None, tb, D_MODEL), lambda n: (in_block(n) // nblk, in_block(n) % nblk, 0)),
                    _const_spec((N_META, D_MODEL))]
    else:
        in_specs = [pl.BlockSpec((tb, D_MODEL), lambda n: (in_block(n), 0))]
    if last:
        out_spec = pl.BlockSpec((None, tb, D_MODEL), lambda n: (out_block(n) // nblk, out_block(n) % nblk, 0))
        out_shape = jax.ShapeDtypeStruct((bsz, seq, D_MODEL), F32)
    else:
        out_spec = pl.BlockSpec((tb, D_MODEL), lambda n: (out_block(n), 0))
        out_shape = jax.ShapeDtypeStruct((nsteps * tb, D_MODEL), F32)
    return pl.pallas_call(
        functools.partial(_layer_kernel, nblk, nsteps, first, last, n_tok),
        grid=(nsteps + PIPELINE_DEPTH,),
        in_specs=in_specs + [_layer_spec(c.shape, layer) if c.ndim == 3 else _const_spec(c.shape) for c in consts],
        out_specs=out_spec,
        out_shape=out_shape,
        scratch_shapes=[pltpu.VMEM(shape, dtype) for _, shape, dtype in _SCRATCH],
        compiler_params=pltpu.CompilerParams(
            dimension_semantics=("arbitrary",), vmem_limit_bytes=VMEM_LIMIT_BYTES),
    )(*inputs, *consts)


def _block_diag(blocks):
    n, a, b = blocks.shape
    eye = jnp.eye(n, dtype=blocks.dtype)
    return jnp.einsum('nab,nm->namb', blocks, eye).reshape(n * a, n * b)


def _hgrn2_level_map():
    t = jnp.arange(HG_CHUNK, dtype=jnp.int32)[:, None]
    s = jnp.arange(HG_CHUNK, dtype=jnp.int32)[None, :]
    x = t ^ s
    lv = jnp.full((HG_CHUNK, HG_CHUNK), -1, jnp.int32)
    for level in range(HG_LEVELS):
        n = 1 << level
        lv = jnp.where((x >= n) & (x < 2 * n) & (t > s), level, lv)
    lv = jnp.where(t == s, HG_LEVELS, lv)
    return jnp.tile(lv.T, (HG_HEADS, 1))


def _prepare_layer(l, lb, w_in, s5_lam_re, s5_lam_im, s5_b_re, s5_b_im, s5_c_re, s5_c_im, s5_d,
                   s5_log_dt, s5_glu_w, s5_glu_b, sc_conv_w, hg_gnorm, lru_conv_w, lru_conv_b,
                   lru_wa, lru_ba, lru_wx, lru_bx, lru_a_param):
    lam_r, lam_i = s5_lam_re[l].astype(F32), s5_lam_im[l].astype(F32)
    dt = jnp.exp(s5_log_dt[l].astype(F32))[:, None]
    arg_r, arg_i = lam_r * dt, lam_i * dt

    def lam_bar_pow(k):
        mag = jnp.exp(arg_r * k)
        return mag * jnp.cos(arg_i * k), mag * jnp.sin(arg_i * k)
    bar_r, bar_i = lam_bar_pow(1.0)
    num_r, num_i = bar_r - 1.0, bar_i
    den = lam_r * lam_r + lam_i * lam_i
    coef_r = ((num_r * lam_r + num_i * lam_i) / den)[..., None]
    coef_i = ((num_i * lam_r - num_r * lam_i) / den)[..., None]
    b_r, b_i = s5_b_re[l].astype(F32), s5_b_im[l].astype(F32)
    bbar_r = jnp.transpose(coef_r * b_r - coef_i * b_i, (0, 2, 1))
    bbar_i = jnp.transpose(coef_r * b_i + coef_i * b_r, (0, 2, 1))
    bmat = jnp.concatenate([_block_diag(bbar_r), _block_diag(bbar_i)], axis=1)
    c_r = jnp.transpose(s5_c_re[l].astype(F32), (0, 2, 1))
    c_i = jnp.transpose(s5_c_im[l].astype(F32), (0, 2, 1))
    cmat = jnp.concatenate([_block_diag(c_r), -_block_diag(c_i)], axis=0)

    seg_r, seg_i = lam_bar_pow(float(SEG))
    lam_rows = jnp.stack([jnp.concatenate([bar_r.reshape(-1), bar_i.reshape(-1)]),
                          jnp.concatenate([seg_r.reshape(-1), seg_i.reshape(-1)])])
    steps = jnp.arange(1, SEG + 1, dtype=F32)[:, None, None]
    pw_r, pw_i = lam_bar_pow(steps)
    ptab = jnp.concatenate([pw_r.reshape(SEG, -1), pw_i.reshape(SEG, -1)], axis=1)

    one_m_lb = 1.0 - lb
    rows = [None] * N_VEC
    rows[V_S5_D] = s5_d[l]
    rows[V_GLU_B] = s5_glu_b[l]
    rows[V_SC_W0], rows[V_SC_W1], rows[V_SC_W2] = sc_conv_w[l, 0], sc_conv_w[l, 1], sc_conv_w[l, 2]
    rows[V_HG_LOGLB] = jnp.maximum(jnp.log(lb), -1e30)
    rows[V_HG_LOG1MLB] = jnp.log1p(-lb)
    rows[V_HG_1MLB] = one_m_lb
    rows[V_HG_GNORM] = hg_gnorm[l]
    for i, v in enumerate((V_LRU_W0, V_LRU_W1, V_LRU_W2, V_LRU_W3)):
        rows[v] = lru_conv_w[l, i]
    rows[V_LRU_CB] = lru_conv_b[l]
    rows[V_LRU_CA] = -LRU_C * jax.nn.softplus(-lru_a_param[l].astype(F32))
    rows[V_LRU_BA] = lru_ba[l]
    rows[V_LRU_BX] = lru_bx[l]
    zero = jnp.zeros((W_GROUP,), F32)
    vec = jnp.stack([zero if r is None else r.astype(F32) for r in rows])
    wab = jnp.concatenate([_block_diag(lru_wa[l].astype(F32)), _block_diag(lru_wx[l].astype(F32))], axis=1)
    return (w_in, vec, bmat.astype(BF16), cmat.astype(BF16), lam_rows, ptab,
            s5_glu_w[l].astype(BF16), wab.astype(BF16))


def kernel(x, meta_tokens, hg_lb_raw, w_in, w_out, s5_lam_re, s5_lam_im, s5_b_re, s5_b_im, s5_c_re, s5_c_im, s5_d, s5_log_dt, s5_glu_w, s5_glu_b, sc_conv_w, hg_gnorm, lru_conv_w, lru_conv_b, lru_wa, lru_ba, lru_wx, lru_bx, lru_a_param, ln1_g, ln1_b, w_ffn_in, w_ffn_out, ln2_g, ln2_b):
    bsz, seq, d = x.shape
    assert d == D_MODEL
    inputs = (x.astype(F32), meta_tokens.astype(F32))

    lb_all = jnp.cumsum(jax.nn.softmax(hg_lb_raw.astype(F32), axis=0), axis=0)
    lb_all = lb_all - lb_all[0:1]

    head = jnp.arange(W_GROUP) // HG_HEAD_DIM
    same_head = head[:, None] == head[None, :]
    bdmask = same_head.astype(F32)
    headmask = same_head.astype(BF16)
    ones = (same_head.astype(F32) / HG_HEAD_DIM).astype(BF16)
    lv = _hgrn2_level_map()

    w_in_bf, w_out_bf = w_in.astype(BF16), w_out.astype(BF16)
    w_ffn_in_bf, w_ffn_out_bf = w_ffn_in.astype(BF16), w_ffn_out.astype(BF16)
    for l in range(DEPTH):
        prep = _prepare_layer(l, lb_all[l], w_in_bf, s5_lam_re, s5_lam_im, s5_b_re, s5_b_im, s5_c_re, s5_c_im,
                              s5_d, s5_log_dt, s5_glu_w, s5_glu_b, sc_conv_w, hg_gnorm, lru_conv_w,
                              lru_conv_b, lru_wa, lru_ba, lru_wx, lru_bx, lru_a_param)
        ln = jnp.stack([ln1_g[l], ln1_b[l], ln2_g[l], ln2_b[l]]).astype(F32)
        consts = prep + (headmask, bdmask, ones, lv, w_out_bf, ln, w_ffn_in_bf, w_ffn_out_bf)
        inputs = (_layer_call(inputs, consts, bsz, seq, l, first=l == 0, last=l == DEPTH - 1),)
    return inputs[0]
```

```python
import functools
import math
import types

import jax
import jax.numpy as jnp
from jax import lax
from jax.experimental import pallas as pl
from jax.experimental.pallas import tpu as pltpu

F32 = jnp.float32
BF16 = jnp.bfloat16

D_MODEL = 1024
DEPTH = 2
N_META = 16
W_GROUP = 256
N_IN = 10 * W_GROUP
S5_GROUP = 16
S5_NGROUPS = 16
S5_STATE = 64
S5_W = S5_NGROUPS * S5_STATE
HG_HEADS = 4
HG_HEAD_DIM = 64
LRU_HEADS = 4
LRU_C = 8.0
D_FF = 2816
ALPHA = (2 * DEPTH) ** 0.25
EPS = 1e-5

SUBLANES = 8
LANES = 128
MXU_N = 256
NSLAB = W_GROUP // LANES
TIME_BLOCK = 320
SEG = TIME_BLOCK // SUBLANES
SCAN_PIECE = 8
PIPELINE_DEPTH = 2
HG_CHUNK = 64
HG_LEVELS = 6
VMEM_LIMIT_BYTES = 60 * 1024 * 1024

(V_S5_D, V_GLU_B, V_SC_W0, V_SC_W1, V_SC_W2, V_HG_LOGLB, V_HG_LOG1MLB, V_HG_1MLB, V_HG_GNORM,
 V_LRU_W0, V_LRU_W1, V_LRU_W2, V_LRU_W3, V_LRU_CB, V_LRU_CA, V_LRU_BA, V_LRU_BX) = range(17)
N_VEC = 24


def _sigmoid(x):
    return 0.5 + 0.5 * jnp.tanh(0.5 * x)


def _silu(x):
    h = 0.5 * x
    return h + h * jnp.tanh(h)


def _gelu_tanh(x):
    c = math.sqrt(2.0 / math.pi)
    return x * (0.5 * (1.0 + jnp.tanh(c * (x + 0.044715 * (x * x * x)))))


def _dot(a, b):
    return jnp.dot(a, b, preferred_element_type=F32)


def _dot_nt(a, b):
    return lax.dot_general(a, b, (((1,), (1,)), ((), ())), preferred_element_type=F32)


def _dot_tn(a, b):
    return lax.dot_general(a, b, (((0,), (0,)), ((), ())), preferred_element_type=F32)


def _tile(r):
    return slice(r * SUBLANES, (r + 1) * SUBLANES)


def _slab(j, first=0):
    return slice(first + j * LANES, first + (j + 1) * LANES)


def _cols(j, first=0):
    return slice(first + j * MXU_N, first + (j + 1) * MXU_N)


def _seg_rows(r):
    return pl.ds(r, SUBLANES, stride=SEG)


def _vrow(vec_ref, i):
    return vec_ref[i:i + 1, :]


def _layer_norm(x, g, b):
    mu = jnp.mean(x, axis=-1, keepdims=True)
    xc = x - mu
    var = jnp.mean(xc * xc, axis=-1, keepdims=True)
    return xc * lax.rsqrt(var + EPS) * g + b


def _sibling(x, n):
    c, w = x.shape
    if n < SUBLANES:
        x3 = x.reshape(c // SUBLANES, SUBLANES, w)
        fwd = pltpu.roll(x3, n, 1)
        if 2 * n == SUBLANES:
            return fwd.reshape(c, w)
        bwd = pltpu.roll(x3, SUBLANES - n, 1)
        row = lax.broadcasted_iota(jnp.int32, x3.shape, 1)
        return jnp.where((row & n) != 0, fwd, bwd).reshape(c, w)
    x4 = x.reshape(c // (2 * n), 2, n, w)
    return jnp.concatenate([x4[:, 1:2], x4[:, 0:1]], axis=1).reshape(c, w)


class _Streams:
    def __init__(self):
        self.lanes = {}

    def add(self, lane, name, cost, fn, deps=()):
        self.lanes.setdefault(lane, []).append((name, cost, tuple(deps), fn))

    def emit(self):
        total = {k: sum(p[1] for p in v) for k, v in self.lanes.items()}
        pos = dict.fromkeys(self.lanes, 0)
        spent = dict.fromkeys(self.lanes, 0)
        done = set()

        def ready(k):
            return pos[k] < len(self.lanes[k]) and all(d in done for d in self.lanes[k][pos[k]][2])
        while any(pos[k] < len(v) for k, v in self.lanes.items()):
            order = sorted(self.lanes, key=lambda k: spent[k] / total[k] * LANE_LAG.get(k, 1.0))
            pick = next((k for k in order if ready(k)), None)
            assert pick is not None, "piece dependencies cannot be met"
            name, cost, _, fn = self.lanes[pick][pos[pick]]
            fn()
            done.add(name)
            pos[pick] += 1
            spent[pick] += cost


MXU, PROJ, VPU, S5, HG, LRU, TAIL = "mxu", "proj", "misc", "s5", "hg", "lru", "tail"
LANE_LAG = {MXU: 0.8, PROJ: 0.3}


def _out_proj_pieces(st, r):
    for j in range(D_MODEL // MXU_N):
        def out_proj(j=j):
            r.res[:, _cols(j)] = ALPHA * r.hprev[:, _cols(j)] + _dot(r.mix[...], r.w_out[:, _cols(j)])
        st.add(MXU, f"out_proj{j}", 320, out_proj)


def _ln1_piece(st, r):
    def ln1():
        hn = _layer_norm(r.res[...], r.ln[0:1, :], r.ln[1:2, :])
        r.hn[...] = hn
        r.hnb[...] = hn.astype(BF16)
    st.add(VPU, "ln1", 800, ln1, [f"out_proj{D_MODEL // MXU_N - 1}"])


def _ffn_pieces(st, r):
    ncol = D_MODEL // MXU_N
    nff = D_FF // MXU_N
    for j in range(nff):
        def ffn_in(j=j):
            hb = r.hnb[...]
            gate = _dot(hb, r.w_ffn_in[:, _cols(j)])
            up = _dot(hb, r.w_ffn_in[:, _cols(j, D_FF)])
            r.act[:, _cols(j)] = (_silu(gate) * up).astype(BF16)
        st.add(MXU, f"ffn_in{j}", 640, ffn_in, ["ln1"])

    for j in range(ncol):
        def ffn_out(j=j):
            r.ybuf[:, _cols(j)] = ALPHA * r.hn[:, _cols(j)] + _dot(r.act[...], r.w_ffn_out[:, _cols(j)])
        st.add(MXU, f"ffn_out{j}", 880, ffn_out, [f"ffn_in{nff - 1}", "ln2"])


def _ln2_pieces(st, r, shift_rows, with_tail):
    def norm(y):
        return _layer_norm(y, r.ln[2:3, :], r.ln[3:4, :])

    def ln2():
        if shift_rows:
            r.out[0:TIME_BLOCK - shift_rows, :] = norm(r.ybuf[shift_rows:TIME_BLOCK, :])
        else:
            r.out[...] = norm(r.ybuf[...])
    st.add(VPU, "ln2", 800, ln2)

    def ln2_tail():
        r.out[TIME_BLOCK - shift_rows:TIME_BLOCK, :] = norm(r.ybuf[0:shift_rows, :])
    if shift_rows and with_tail:
        st.add(TAIL, "ln2_tail", 50, ln2_tail, [f"ffn_out{D_MODEL // MXU_N - 1}"])


def _assemble_piece(st, r, block, n_tok):
    def assemble():
        keep = TIME_BLOCK - N_META
        row = lax.broadcasted_iota(jnp.int32, (keep, D_MODEL), 0) + (block * TIME_BLOCK + N_META)
        r.hcur[0:N_META, :] = r.xcarry[...]
        r.hcur[N_META:TIME_BLOCK, :] = jnp.where(row < n_tok, r.x[0:keep, :], 0.0)
        r.xcarry[...] = r.x[keep:TIME_BLOCK, :]
    st.add(VPU, "assemble", 100, assemble)


def _in_proj_pieces(st, r):
    def cast():
        r.hb[...] = r.h[...].astype(BF16)
    st.add(VPU, "cast", 100, cast)

    def proj(j):
        return _dot(r.hb[...], r.w_in[:, _cols(j)])

    def p_s5():
        p = proj(0)
        for j in range(NSLAB):
            r.segin[j] = p[:, _slab(j)]
    st.add(PROJ, "p_s5", 320, p_s5, ["cast"])
    return proj


def _in_proj_rest(st, r, proj):
    for j in range(4):
        def p_hg(j=j):
            r.hgp[:, _cols(j)] = proj(4 + j)
        st.add(PROJ, f"p_hg{j}", 320, p_hg)

    def p_lru_x():
        r.lru_xbuf[SUBLANES:SUBLANES + TIME_BLOCK, :] = proj(8)
    st.add(PROJ, "p_lru_x", 320, p_lru_x)

    def p_lru_y():
        p = proj(9)
        for j in range(NSLAB):
            r.segin[NSLAB + j] = p[:, _slab(j)]
    st.add(PROJ, "p_lru_y", 320, p_lru_y)
    for j in range(3):
        def p_sc(j=j):
            r.scp[:, _cols(j)] = proj(1 + j)
        st.add(PROJ, f"p_sc{j}", 320, p_sc)


def _s5_head(st, r):
    def gather():
        for t in range(SEG):
            for j in range(NSLAB):
                r.uperm[_tile(t), _slab(j)] = r.segin[j, _seg_rows(t), :]
        r.bu[...] = _dot(r.uperm[...].astype(BF16), r.bmat[...])
    st.add(S5, "s5_gather", 300, gather, ["p_s5"])


def _s5_pieces(st, r):
    n = S5_W
    env = types.SimpleNamespace()

    def scan(first):
        lam_r = jnp.broadcast_to(r.lam[0:1, 0:n], (SUBLANES, n))
        lam_i = jnp.broadcast_to(r.lam[0:1, n:2 * n], (SUBLANES, n))
        for t in range(first, first + SCAN_PIECE):
            br, bi = r.bu[_tile(t), 0:n], r.bu[_tile(t), n:2 * n]
            if t == 0:
                env.xr, env.xi = br, bi
            else:
                env.xr, env.xi = (lam_r * env.xr - lam_i * env.xi + br,
                                  lam_r * env.xi + lam_i * env.xr + bi)
                r.bu[_tile(t), 0:n] = env.xr
                r.bu[_tile(t), n:2 * n] = env.xi
    for first in range(0, SEG, SCAN_PIECE):
        st.add(S5, f"s5_scan{first}", 22 * SCAN_PIECE, functools.partial(scan, first))

    def carry():
        r.s5_e[:, 0:n] = env.xr
        r.s5_e[:, n:2 * n] = env.xi
        pr, pi_ = r.lam[1:2, 0:n], r.lam[1:2, n:2 * n]
        cr, ci = r.s5_state[0:1, 0:n], r.s5_state[0:1, n:2 * n]
        for s in range(SUBLANES):
            r.s5_cin[s:s + 1, 0:n] = cr
            r.s5_cin[s:s + 1, n:2 * n] = ci
            sr, si = r.s5_e[s:s + 1, 0:n], r.s5_e[s:s + 1, n:2 * n]
            cr, ci = sr + (pr * cr - pi_ * ci), si + (pr * ci + pi_ * cr)
        r.s5_state[0:1, 0:n] = cr
        r.s5_state[0:1, n:2 * n] = ci
    st.add(S5, "s5_carry", 100, carry)

    def fix(first):
        cin_r = r.s5_cin[:, 0:n]
        cin_i = r.s5_cin[:, n:2 * n]
        for tt in range(first // 2, (first + SCAN_PIECE) // 2):
            parts_r, parts_i = [], []
            for t in (2 * tt, 2 * tt + 1):
                qr = jnp.broadcast_to(r.ptab[t:t + 1, 0:n], (SUBLANES, n))
                qi = jnp.broadcast_to(r.ptab[t:t + 1, n:2 * n], (SUBLANES, n))
                parts_r.append(r.bu[_tile(t), 0:n] + (qr * cin_r - qi * cin_i))
                parts_i.append(r.bu[_tile(t), n:2 * n] + (qr * cin_i + qi * cin_r))
            rows2 = slice(tt * 2 * SUBLANES, (tt + 1) * 2 * SUBLANES)
            r.xbf[rows2, 0:n] = jnp.concatenate(parts_r, axis=0).astype(BF16)
            r.xbf[rows2, n:2 * n] = jnp.concatenate(parts_i, axis=0).astype(BF16)
    for first in range(0, SEG, SCAN_PIECE):
        st.add(S5, f"s5_fix{first}", 32 * SCAN_PIECE, functools.partial(fix, first))

    def post():
        y = _dot(r.xbf[...], r.cmat[...])
        y = y + _vrow(r.vec, V_S5_D) * r.uperm[...]
        g = _gelu_tanh(y)
        gate = _sigmoid(_dot(g.astype(BF16), r.glu_w[...]) + _vrow(r.vec, V_GLU_B))
        r.uperm[...] = g * gate
        for t in range(SEG):
            for j in range(NSLAB):
                r.segout[j, _seg_rows(t), :] = r.uperm[_tile(t), _slab(j)]
    st.add(S5, "s5_post", 500, post)


def _short_conv_pieces(st, r):
    def conv():
        tb = TIME_BLOCK
        h, gb, gc = r.scp[:, 0:W_GROUP], r.scp[:, W_GROUP:2 * W_GROUP], r.scp[:, 2 * W_GROUP:3 * W_GROUP]
        r.sc_buf[SUBLANES:SUBLANES + tb, :] = gc * h
        out = (_vrow(r.vec, V_SC_W0) * r.sc_buf[SUBLANES - 2:SUBLANES - 2 + tb, :]
               + _vrow(r.vec, V_SC_W1) * r.sc_buf[SUBLANES - 1:SUBLANES - 1 + tb, :]
               + _vrow(r.vec, V_SC_W2) * r.sc_buf[SUBLANES:SUBLANES + tb, :])
        r.mix[:, W_GROUP:2 * W_GROUP] = (gb * out).astype(BF16)
        r.sc_buf[0:SUBLANES, :] = r.sc_buf[tb:tb + SUBLANES, :]
    st.add(VPU, "sc", 400, conv, ["p_sc2", f"out_proj{D_MODEL // MXU_N - 1}"])


def _hgrn2_pieces(st, r):
    c = HG_CHUNK
    env = types.SimpleNamespace()

    def stack_heads(x_bf):
        return jnp.concatenate([x_bf] * HG_HEADS, axis=0) * r.headmask[...]

    def gates():
        q_in = r.hgp[:, 0:W_GROUP]
        z = r.hgp[:, W_GROUP:2 * W_GROUP]
        q = _silu(q_in) * (HG_HEAD_DIM ** -0.5)
        log_sig = jnp.minimum(z, 0.0) - jnp.log(1.0 + jnp.exp(-jnp.abs(z)))
        cpl = _vrow(r.vec, V_HG_LOG1MLB) + log_sig
        loglb = _vrow(r.vec, V_HG_LOGLB)
        g = jnp.maximum(loglb, cpl) + jnp.log(1.0 + jnp.exp(-jnp.abs(loglb - cpl)))
        k = _vrow(r.vec, V_HG_1MLB) * _sigmoid(-z)
        r.hg_qt[HG_LEVELS + 1] = q.astype(BF16)
        r.hg_kt[HG_LEVELS + 1] = k.astype(BF16)
        env.q, env.k, env.cin, env.sfx = q, k, g, jnp.zeros_like(g)
    st.add(HG, "hg_gates", 600, gates, ["p_hg1"])

    def level(lvl):
        r.hg_qt[lvl] = (env.q * jnp.exp(env.cin)).astype(BF16)
        r.hg_kt[lvl] = (env.k * jnp.exp(env.sfx)).astype(BF16)
        if lvl == HG_LEVELS:
            r.hg_cin[...] = env.cin
            return
        n = 1 << lvl
        row = lax.broadcasted_iota(jnp.int32, (TIME_BLOCK, W_GROUP), 0)
        sib = _sibling(env.cin + env.sfx, n)
        right = (row & n) != 0
        env.cin = env.cin + jnp.where(right, sib, 0.0)
        env.sfx = env.sfx + jnp.where(right, 0.0, sib)
    for lvl in range(HG_LEVELS + 1):
        st.add(HG, f"hg_level{lvl}", 350, functools.partial(level, lvl))

    nchunk = TIME_BLOCK // c

    def chunk_rows(ci):
        return slice(ci * c, (ci + 1) * c)

    def intra(ci):
        rows = chunk_rows(ci)
        lv = r.lv[...]
        v = r.hgp[rows, 2 * W_GROUP:3 * W_GROUP].astype(BF16)
        scores = None
        for lvl in (HG_LEVELS + 1,) + tuple(range(HG_LEVELS)):
            s_n = _dot_nt(r.hg_qt[lvl, rows, :], stack_heads(r.hg_kt[lvl, rows, :]))
            code = HG_LEVELS if lvl == HG_LEVELS + 1 else lvl
            scores = jnp.where(lv == code, s_n, 0.0 if scores is None else scores)
        r.hg_o[rows, :] = _dot(scores.astype(BF16), stack_heads(v))
        r.hg_upd[ci] = _dot_tn(v, r.hg_kt[HG_LEVELS, rows, :]) * r.bdmask[...]
    for ci in range(nchunk):
        st.add(HG, f"hg_intra{ci}", 400, functools.partial(intra, ci), ["p_hg3"])

    def inter(ci):
        rows = chunk_rows(ci)
        state = r.hg_state[...]
        r.hg_o[rows, :] = r.hg_o[rows, :] + _dot_nt(r.hg_qt[HG_LEVELS, rows, :], state.astype(BF16))
        decay = jnp.exp(r.hg_cin[(ci + 1) * c - 1:(ci + 1) * c, :])
        r.hg_state[...] = state * decay + r.hg_upd[ci]
    for ci in range(nchunk):
        st.add(HG, f"hg_inter{ci}", 150, functools.partial(inter, ci))

    def norm():
        o = r.hg_o[...]
        ms = _dot((o * o).astype(BF16), r.ones[...])
        o = o * lax.rsqrt(ms + EPS) * _vrow(r.vec, V_HG_GNORM)
        g_in = r.hgp[:, 3 * W_GROUP:4 * W_GROUP]
        r.mix[:, 2 * W_GROUP:3 * W_GROUP] = (o * _silu(g_in)).astype(BF16)
    st.add(HG, "hg_norm", 300, norm, [f"out_proj{D_MODEL // MXU_N - 1}"])


def _rglru_pieces(st, r):
    tb = TIME_BLOCK
    env = types.SimpleNamespace()

    def seg_tile(ref, first, t):
        return jnp.concatenate([ref[first + j, _seg_rows(t), :] for j in range(NSLAB)], axis=1)

    def pre():
        xc = _vrow(r.vec, V_LRU_CB) + (
            _vrow(r.vec, V_LRU_W0) * r.lru_xbuf[SUBLANES - 3:SUBLANES - 3 + tb, :]
            + _vrow(r.vec, V_LRU_W1) * r.lru_xbuf[SUBLANES - 2:SUBLANES - 2 + tb, :]
            + _vrow(r.vec, V_LRU_W2) * r.lru_xbuf[SUBLANES - 1:SUBLANES - 1 + tb, :]
            + _vrow(r.vec, V_LRU_W3) * r.lru_xbuf[SUBLANES:SUBLANES + tb, :])
        r.lru_xbuf[0:SUBLANES, :] = r.lru_xbuf[tb:tb + SUBLANES, :]
        gates = _dot(xc.astype(BF16), r.wab[...])
        gate_a = _sigmoid(gates[:, 0:W_GROUP] + _vrow(r.vec, V_LRU_BA))
        gate_x = _sigmoid(gates[:, W_GROUP:2 * W_GROUP] + _vrow(r.vec, V_LRU_BX))
        log_a = _vrow(r.vec, V_LRU_CA) * gate_a
        a = jnp.exp(log_a)
        b = xc * gate_x * jnp.sqrt(-jnp.tanh(log_a) * (a * a + 1.0))
        for j in range(NSLAB):
            r.lru_a[j] = a[:, _slab(j)]
            r.lru_b[j] = b[:, _slab(j)]
    st.add(LRU, "lru_pre", 500, pre, ["p_lru_x"])

    def scan(first):
        for t in range(first, first + SCAN_PIECE):
            a_t, b_t = seg_tile(r.lru_a, 0, t), seg_tile(r.lru_b, 0, t)
            env.h, env.p = (b_t, a_t) if t == 0 else (a_t * env.h + b_t, a_t * env.p)
            r.lru_h[_tile(t), :] = env.h
            r.lru_p[_tile(t), :] = env.p
    for first in range(0, SEG, SCAN_PIECE):
        st.add(LRU, f"lru_scan{first}", 18 * SCAN_PIECE, functools.partial(scan, first))

    def carry():
        r.lru_e[0:SUBLANES, :] = env.h
        r.lru_e[SUBLANES:2 * SUBLANES, :] = env.p
        cur = r.lru_state[0:1, :]
        for s in range(SUBLANES):
            r.lru_cin[s:s + 1, :] = cur
            cur = r.lru_e[s:s + 1, :] + r.lru_e[SUBLANES + s:SUBLANES + s + 1, :] * cur
        r.lru_state[0:1, :] = cur
    st.add(LRU, "lru_carry", 50, carry)

    def fix(first):
        cin = r.lru_cin[...]
        for t in range(first, first + SCAN_PIECE):
            res = (r.lru_h[_tile(t), :] + r.lru_p[_tile(t), :] * cin) * _gelu_tanh(seg_tile(r.segin, NSLAB, t))
            for j in range(NSLAB):
                r.segout[NSLAB + j, _seg_rows(t), :] = res[:, _slab(j)]
    for first in range(0, SEG, SCAN_PIECE):
        st.add(LRU, f"lru_fix{first}", 38 * SCAN_PIECE, functools.partial(fix, first), ["p_lru_y"])


_SCRATCH = (
    ("hprev", (TIME_BLOCK, D_MODEL), F32),
    ("mix", (TIME_BLOCK, 4 * W_GROUP), BF16),
    ("res", (TIME_BLOCK, D_MODEL), F32),
    ("ybuf", (TIME_BLOCK, D_MODEL), F32),
    ("hcur", (TIME_BLOCK, D_MODEL), F32),
    ("xcarry", (N_META, D_MODEL), F32),
    ("hn", (TIME_BLOCK, D_MODEL), F32),
    ("hnb", (TIME_BLOCK, D_MODEL), BF16),
    ("act", (TIME_BLOCK, D_FF), BF16),
    ("hb", (TIME_BLOCK, D_MODEL), BF16),
    ("hgp", (TIME_BLOCK, 4 * W_GROUP), F32),
    ("scp", (TIME_BLOCK, 3 * W_GROUP), F32),
    ("segin", (2 * NSLAB, TIME_BLOCK, LANES), F32),
    ("segout", (2 * NSLAB, TIME_BLOCK, LANES), F32),
    ("uperm", (TIME_BLOCK, W_GROUP), F32),
    ("bu", (TIME_BLOCK, 2 * S5_W), F32),
    ("xbf", (TIME_BLOCK, 2 * S5_W), BF16),
    ("s5_e", (SUBLANES, 2 * S5_W), F32),
    ("s5_cin", (SUBLANES, 2 * S5_W), F32),
    ("s5_state", (SUBLANES, 2 * S5_W), F32),
    ("sc_buf", (TIME_BLOCK + SUBLANES, W_GROUP), F32),
    ("hg_qt", (HG_LEVELS + 2, TIME_BLOCK, W_GROUP), BF16),
    ("hg_kt", (HG_LEVELS + 2, TIME_BLOCK, W_GROUP), BF16),
    ("hg_cin", (TIME_BLOCK, W_GROUP), F32),
    ("hg_state", (W_GROUP, W_GROUP), F32),
    ("hg_upd", (TIME_BLOCK // HG_CHUNK, W_GROUP, W_GROUP), F32),
    ("hg_o", (TIME_BLOCK, W_GROUP), F32),
    ("lru_xbuf", (TIME_BLOCK + SUBLANES, W_GROUP), F32),
    ("lru_a", (NSLAB, TIME_BLOCK, LANES), F32),
    ("lru_b", (NSLAB, TIME_BLOCK, LANES), F32),
    ("lru_h", (TIME_BLOCK, W_GROUP), F32),
    ("lru_p", (TIME_BLOCK, W_GROUP), F32),
    ("lru_e", (2 * SUBLANES, W_GROUP), F32),
    ("lru_cin", (SUBLANES, W_GROUP), F32),
    ("lru_state", (SUBLANES, W_GROUP), F32),
)
_CONSTS = ("w_in", "vec", "bmat", "cmat", "lam", "ptab", "glu_w", "wab", "headmask", "bdmask", "ones",
           "lv", "w_out", "ln", "w_ffn_in", "w_ffn_out")


def _layer_kernel(nblk, nsteps, first, last, n_tok, *refs):
    names = (("x", "meta") if first else ("h",)) + _CONSTS + ("out",) + tuple(s[0] for s in _SCRATCH)
    r = types.SimpleNamespace(**dict(zip(names, refs, strict=True)))
    step = pl.program_id(0)
    block = jnp.minimum(step, nsteps - 1) % nblk

    @pl.when(step == 0)
    def _():
        r.hprev[...] = jnp.zeros_like(r.hprev)
        r.mix[...] = jnp.zeros_like(r.mix)
        r.ybuf[...] = jnp.zeros_like(r.ybuf)

    if first:
        r.h = r.hcur

        @pl.when(block == 0)
        def _():
            r.xcarry[...] = r.meta[...]

    @pl.when(step % nblk == 0)
    def _():
        r.s5_state[...] = jnp.zeros_like(r.s5_state)
        r.sc_buf[0:SUBLANES, :] = jnp.zeros((SUBLANES, W_GROUP), F32)
        r.hg_state[...] = jnp.zeros_like(r.hg_state)
        r.lru_xbuf[0:SUBLANES, :] = jnp.zeros((SUBLANES, W_GROUP), F32)
        r.lru_state[...] = jnp.zeros_like(r.lru_state)

    def finish():
        for j in range(NSLAB):
            r.mix[:, _slab(j)] = r.segout[j].astype(BF16)
            r.mix[:, _slab(j, 3 * W_GROUP)] = r.segout[NSLAB + j].astype(BF16)
        r.hprev[...] = r.h[...]

    def body(mixers, channel):
        st = _Streams()
        if mixers:
            if first:
                _assemble_piece(st, r, block, n_tok)
            proj = _in_proj_pieces(st, r)
        _ln2_pieces(st, r, N_META if last else 0, with_tail=channel)
        if channel:
            _out_proj_pieces(st, r)
        if mixers:
            _in_proj_rest(st, r, proj)
        if channel:
            _ffn_pieces(st, r)
        if mixers:
            _s5_head(st, r)
        if channel:
            _ln1_piece(st, r)
        if mixers:
            _s5_pieces(st, r)
            _hgrn2_pieces(st, r)
            _rglru_pieces(st, r)
            _short_conv_pieces(st, r)
            st.add(VPU, "finish", 200, finish,
                   [f"out_proj{D_MODEL // MXU_N - 1}", "s5_post", f"lru_fix{SEG - SCAN_PIECE}"])
        st.emit()

    pl.when(step < nsteps)(functools.partial(body, True, True))
    pl.when(step == nsteps)(functools.partial(body, False, True))
    pl.when(step == nsteps + 1)(functools.partial(body, False, False))


def _const_spec(shape):
    return pl.BlockSpec(shape, lambda *_: (0,) * len(shape), pipeline_mode=pl.Buffered(1))


def _layer_spec(shape, layer):
    return pl.BlockSpec((None,) + shape[1:], lambda *_: (layer, 0, 0), pipeline_mode=pl.Buffered(1))


def _layer_call(inputs, consts, bsz, seq, layer, first, last):
    tb = TIME_BLOCK
    n_tok = N_META + seq
    nblk = pl.cdiv(n_tok, tb)
    nsteps = bsz * nblk

    def in_block(n):
        return jnp.minimum(n, nsteps - 1)

    def out_block(n):
        return jnp.maximum(n - PIPELINE_DEPTH, 0)
    if first:
        in_specs = [pl.BlockSpec((None, tb, D_MODEL), lambda n: (in_block(n) // nblk, in_block(n) % nblk, 0)),
                    _const_spec((N_META, D_MODEL))]
    else:
        in_specs = [pl.BlockSpec((tb, D_MODEL), lambda n: (in_block(n), 0))]
    if last:
        out_spec = pl.BlockSpec((None, tb, D_MODEL), lambda n: (out_block(n) // nblk, out_block(n) % nblk, 0))
        out_shape = jax.ShapeDtypeStruct((bsz, seq, D_MODEL), F32)
    else:
        out_spec = pl.BlockSpec((tb, D_MODEL), lambda n: (out_block(n), 0))
        out_shape = jax.ShapeDtypeStruct((nsteps * tb, D_MODEL), F32)
    return pl.pallas_call(
        functools.partial(_layer_kernel, nblk, nsteps, first, last, n_tok),
        grid=(nsteps + PIPELINE_DEPTH,),
        in_specs=in_specs + [_layer_spec(c.shape, layer) if c.ndim == 3 else _const_spec(c.shape) for c in consts],
        out_specs=out_spec,
        out_shape=out_shape,
        scratch_shapes=[pltpu.VMEM(shape, dtype) for _, shape, dtype in _SCRATCH],
        compiler_params=pltpu.CompilerParams(
            dimension_semantics=("arbitrary",), vmem_limit_bytes=VMEM_LIMIT_BYTES),
    )(*inputs, *consts)


def _block_diag(blocks):
    n, a, b = blocks.shape
    eye = jnp.eye(n, dtype=blocks.dtype)
    return jnp.einsum('nab,nm->namb', blocks, eye).reshape(n * a, n * b)


def _hgrn2_level_map():
    t = jnp.arange(HG_CHUNK, dtype=jnp.int32)[:, None]
    s = jnp.arange(HG_CHUNK, dtype=jnp.int32)[None, :]
    x = t ^ s
    lv = jnp.full((HG_CHUNK, HG_CHUNK), -1, jnp.int32)
    for level in range(HG_LEVELS):
        n = 1 << level
        lv = jnp.where((x >= n) & (x < 2 * n) & (t > s), level, lv)
    lv = jnp.where(t == s, HG_LEVELS, lv)
    return jnp.tile(lv, (1, HG_HEADS))


def _prepare_layer(l, lb, w_in, s5_lam_re, s5_lam_im, s5_b_re, s5_b_im, s5_c_re, s5_c_im, s5_d,
                   s5_log_dt, s5_glu_w, s5_glu_b, sc_conv_w, hg_gnorm, lru_conv_w, lru_conv_b,
                   lru_wa, lru_ba, lru_wx, lru_bx, lru_a_param):
    lam_r, lam_i = s5_lam_re[l].astype(F32), s5_lam_im[l].astype(F32)
    dt = jnp.exp(s5_log_dt[l].astype(F32))[:, None]
    arg_r, arg_i = lam_r * dt, lam_i * dt

    def lam_bar_pow(k):
        mag = jnp.exp(arg_r * k)
        return mag * jnp.cos(arg_i * k), mag * jnp.sin(arg_i * k)
    bar_r, bar_i = lam_bar_pow(1.0)
    num_r, num_i = bar_r - 1.0, bar_i
    den = lam_r * lam_r + lam_i * lam_i
    coef_r = ((num_r * lam_r + num_i * lam_i) / den)[..., None]
    coef_i = ((num_i * lam_r - num_r * lam_i) / den)[..., None]
    b_r, b_i = s5_b_re[l].astype(F32), s5_b_im[l].astype(F32)
    bbar_r = jnp.transpose(coef_r * b_r - coef_i * b_i, (0, 2, 1))
    bbar_i = jnp.transpose(coef_r * b_i + coef_i * b_r, (0, 2, 1))
    bmat = jnp.concatenate([_block_diag(bbar_r), _block_diag(bbar_i)], axis=1)
    c_r = jnp.transpose(s5_c_re[l].astype(F32), (0, 2, 1))
    c_i = jnp.transpose(s5_c_im[l].astype(F32), (0, 2, 1))
    cmat = jnp.concatenate([_block_diag(c_r), -_block_diag(c_i)], axis=0)

    seg_r, seg_i = lam_bar_pow(float(SEG))
    lam_rows = jnp.stack([jnp.concatenate([bar_r.reshape(-1), bar_i.reshape(-1)]),
                          jnp.concatenate([seg_r.reshape(-1), seg_i.reshape(-1)])])
    steps = jnp.arange(1, SEG + 1, dtype=F32)[:, None, None]
    pw_r, pw_i = lam_bar_pow(steps)
    ptab = jnp.concatenate([pw_r.reshape(SEG, -1), pw_i.reshape(SEG, -1)], axis=1)

    one_m_lb = 1.0 - lb
    rows = [None] * N_VEC
    rows[V_S5_D] = s5_d[l]
    rows[V_GLU_B] = s5_glu_b[l]
    rows[V_SC_W0], rows[V_SC_W1], rows[V_SC_W2] = sc_conv_w[l, 0], sc_conv_w[l, 1], sc_conv_w[l, 2]
    rows[V_HG_LOGLB] = jnp.maximum(jnp.log(lb), -1e30)
    rows[V_HG_LOG1MLB] = jnp.log1p(-lb)
    rows[V_HG_1MLB] = one_m_lb
    rows[V_HG_GNORM] = hg_gnorm[l]
    for i, v in enumerate((V_LRU_W0, V_LRU_W1, V_LRU_W2, V_LRU_W3)):
        rows[v] = lru_conv_w[l, i]
    rows[V_LRU_CB] = lru_conv_b[l]
    rows[V_LRU_CA] = -LRU_C * jax.nn.softplus(-lru_a_param[l].astype(F32))
    rows[V_LRU_BA] = lru_ba[l]
    rows[V_LRU_BX] = lru_bx[l]
    zero = jnp.zeros((W_GROUP,), F32)
    vec = jnp.stack([zero if r is None else r.astype(F32) for r in rows])
    wab = jnp.concatenate([_block_diag(lru_wa[l].astype(F32)), _block_diag(lru_wx[l].astype(F32))], axis=1)
    return (w_in, vec, bmat.astype(BF16), cmat.astype(BF16), lam_rows, ptab,
            s5_glu_w[l].astype(BF16), wab.astype(BF16))


def kernel(x, meta_tokens, hg_lb_raw, w_in, w_out, s5_lam_re, s5_lam_im, s5_b_re, s5_b_im, s5_c_re, s5_c_im, s5_d, s5_log_dt, s5_glu_w, s5_glu_b, sc_conv_w, hg_gnorm, lru_conv_w, lru_conv_b, lru_wa, lru_ba, lru_wx, lru_bx, lru_a_param, ln1_g, ln1_b, w_ffn_in, w_ffn_out, ln2_g, ln2_b):
    bsz, seq, d = x.shape
    assert d == D_MODEL
    inputs = (x.astype(F32), meta_tokens.astype(F32))

    lb_all = jnp.cumsum(jax.nn.softmax(hg_lb_raw.astype(F32), axis=0), axis=0)
    lb_all = lb_all - lb_all[0:1]

    head = jnp.arange(W_GROUP) // HG_HEAD_DIM
    same_head = head[:, None] == head[None, :]
    bdmask = same_head.astype(F32)
    headmask = same_head.astype(BF16)
    ones = (same_head.astype(F32) / HG_HEAD_DIM).astype(BF16)
    lv = _hgrn2_level_map()

    w_in_bf, w_out_bf = w_in.astype(BF16), w_out.astype(BF16)
    w_ffn_in_bf, w_ffn_out_bf = w_ffn_in.astype(BF16), w_ffn_out.astype(BF16)
    for l in range(DEPTH):
        prep = _prepare_layer(l, lb_all[l], w_in_bf, s5_lam_re, s5_lam_im, s5_b_re, s5_b_im, s5_c_re, s5_c_im,
                              s5_d, s5_log_dt, s5_glu_w, s5_glu_b, sc_conv_w, hg_gnorm, lru_conv_w,
                              lru_conv_b, lru_wa, lru_ba, lru_wx, lru_bx, lru_a_param)
        ln = jnp.stack([ln1_g[l], ln1_b[l], ln2_g[l], ln2_b[l]]).astype(F32)
        consts = prep + (headmask, bdmask, ones, lv, w_out_bf, ln, w_ffn_in_bf, w_ffn_out_bf)
        inputs = (_layer_call(inputs, consts, bsz, seq, l, first=l == 0, last=l == DEPTH - 1),)
    return inputs[0]
```

```python
import functools
import math
import types

import jax
import jax.numpy as jnp
from jax import lax
from jax.experimental import pallas as pl
from jax.experimental.pallas import tpu as pltpu

F32 = jnp.float32
BF16 = jnp.bfloat16

D_MODEL = 1024
DEPTH = 2
N_META = 16
W_GROUP = 256
N_IN = 10 * W_GROUP
S5_GROUP = 16
S5_NGROUPS = 16
S5_STATE = 64
S5_W = S5_NGROUPS * S5_STATE
HG_HEADS = 4
HG_HEAD_DIM = 64
LRU_HEADS = 4
LRU_C = 8.0
D_FF = 2816
ALPHA = (2 * DEPTH) ** 0.25
EPS = 1e-5

SUBLANES = 8
LANES = 128
MXU_N = 256
NSLAB = W_GROUP // LANES
TIME_BLOCK = 320
SEG = TIME_BLOCK // SUBLANES
SCAN_PIECE = 8
PIPELINE_DEPTH = 2
HG_CHUNK = 64
HG_LEVELS = 6
VMEM_LIMIT_BYTES = 60 * 1024 * 1024

(V_S5_D, V_GLU_B, V_SC_W0, V_SC_W1, V_SC_W2, V_HG_LOGLB, V_HG_LOG1MLB, V_HG_1MLB, V_HG_GNORM,
 V_LRU_W0, V_LRU_W1, V_LRU_W2, V_LRU_W3, V_LRU_CB, V_LRU_CA, V_LRU_BA, V_LRU_BX) = range(17)
N_VEC = 24


def _sigmoid(x):
    return 0.5 + 0.5 * jnp.tanh(0.5 * x)


def _silu(x):
    h = 0.5 * x
    return h + h * jnp.tanh(h)


def _gelu_tanh(x):
    c = math.sqrt(2.0 / math.pi)
    return x * (0.5 * (1.0 + jnp.tanh(c * (x + 0.044715 * (x * x * x)))))


def _dot(a, b):
    return jnp.dot(a, b, preferred_element_type=F32)


def _dot_nt(a, b):
    return lax.dot_general(a, b, (((1,), (1,)), ((), ())), preferred_element_type=F32)


def _dot_tn(a, b):
    return lax.dot_general(a, b, (((0,), (0,)), ((), ())), preferred_element_type=F32)


def _tile(r):
    return slice(r * SUBLANES, (r + 1) * SUBLANES)


def _slab(j, first=0):
    return slice(first + j * LANES, first + (j + 1) * LANES)


def _cols(j, first=0):
    return slice(first + j * MXU_N, first + (j + 1) * MXU_N)


def _seg_rows(r):
    return pl.ds(r, SUBLANES, stride=SEG)


def _vrow(vec_ref, i):
    return vec_ref[i:i + 1, :]


def _layer_norm(x, g, b):
    mu = jnp.mean(x, axis=-1, keepdims=True)
    xc = x - mu
    var = jnp.mean(xc * xc, axis=-1, keepdims=True)
    return xc * lax.rsqrt(var + EPS) * g + b


def _sibling(x, n):
    c, w = x.shape
    if n < SUBLANES:
        x3 = x.reshape(c // SUBLANES, SUBLANES, w)
        fwd = pltpu.roll(x3, n, 1)
        if 2 * n == SUBLANES:
            return fwd.reshape(c, w)
        bwd = pltpu.roll(x3, SUBLANES - n, 1)
        row = lax.broadcasted_iota(jnp.int32, x3.shape, 1)
        return jnp.where((row & n) != 0, fwd, bwd).reshape(c, w)
    x4 = x.reshape(c // (2 * n), 2, n, w)
    return jnp.concatenate([x4[:, 1:2], x4[:, 0:1]], axis=1).reshape(c, w)


class _Streams:
    def __init__(self):
        self.lanes = {}

    def add(self, lane, name, cost, fn, deps=()):
        self.lanes.setdefault(lane, []).append((name, cost, tuple(deps), fn))

    def emit(self):
        total = {k: sum(p[1] for p in v) for k, v in self.lanes.items()}
        pos = dict.fromkeys(self.lanes, 0)
        spent = dict.fromkeys(self.lanes, 0)
        done = set()

        def ready(k):
            return pos[k] < len(self.lanes[k]) and all(d in done for d in self.lanes[k][pos[k]][2])
        while any(pos[k] < len(v) for k, v in self.lanes.items()):
            order = sorted(self.lanes, key=lambda k: spent[k] / total[k] * LANE_LAG.get(k, 1.0))
            pick = next((k for k in order if ready(k)), None)
            assert pick is not None, "piece dependencies cannot be met"
            name, cost, _, fn = self.lanes[pick][pos[pick]]
            fn()
            done.add(name)
            pos[pick] += 1
            spent[pick] += cost


MXU, PROJ, VPU, S5, HG, LRU, TAIL = "mxu", "proj", "misc", "s5", "hg", "lru", "tail"
LANE_LAG = {MXU: 0.8, PROJ: 0.3}


def _out_proj_pieces(st, r):
    for j in range(D_MODEL // MXU_N):
        def out_proj(j=j):
            r.res[:, _cols(j)] = ALPHA * r.hprev[:, _cols(j)] + _dot(r.mix[...], r.w_out[:, _cols(j)])
        st.add(MXU, f"out_proj{j}", 320, out_proj)


def _ln1_piece(st, r):
    def ln1():
        hn = _layer_norm(r.res[...], r.ln[0:1, :], r.ln[1:2, :])
        r.hn[...] = hn
        r.hnb[...] = hn.astype(BF16)
    st.add(VPU, "ln1", 800, ln1, [f"out_proj{D_MODEL // MXU_N - 1}"])


def _ffn_pieces(st, r):
    ncol = D_MODEL // MXU_N
    nff = D_FF // MXU_N
    for j in range(nff):
        def ffn_in(j=j):
            hb = r.hnb[...]
            gate = _dot(hb, r.w_ffn_in[:, _cols(j)])
            up = _dot(hb, r.w_ffn_in[:, _cols(j, D_FF)])
            r.act[:, _cols(j)] = (_silu(gate) * up).astype(BF16)
        st.add(MXU, f"ffn_in{j}", 640, ffn_in, ["ln1"])

    for j in range(ncol):
        def ffn_out(j=j):
            r.ybuf[:, _cols(j)] = ALPHA * r.hn[:, _cols(j)] + _dot(r.act[...], r.w_ffn_out[:, _cols(j)])
        st.add(MXU, f"ffn_out{j}", 880, ffn_out, [f"ffn_in{nff - 1}", "ln2"])


def _ln2_pieces(st, r, shift_rows, with_tail):
    def norm(y):
        return _layer_norm(y, r.ln[2:3, :], r.ln[3:4, :])

    def ln2():
        if shift_rows:
            r.out[0:TIME_BLOCK - shift_rows, :] = norm(r.ybuf[shift_rows:TIME_BLOCK, :])
        else:
            r.out[...] = norm(r.ybuf[...])
    st.add(VPU, "ln2", 800, ln2)

    def ln2_tail():
        r.out[TIME_BLOCK - shift_rows:TIME_BLOCK, :] = norm(r.ybuf[0:shift_rows, :])
    if shift_rows and with_tail:
        st.add(TAIL, "ln2_tail", 50, ln2_tail, [f"ffn_out{D_MODEL // MXU_N - 1}"])


def _first_layer_block(r, block, n_tok):
    keep = TIME_BLOCK - N_META
    row = lax.broadcasted_iota(jnp.int32, (keep, D_MODEL), 0) + (block * TIME_BLOCK + N_META)
    return jnp.concatenate([r.xcarry[...], jnp.where(row < n_tok, r.x[0:keep, :], 0.0)], axis=0)


def _in_proj_pieces(st, r):
    def cast():
        r.hb[...] = r.read_h().astype(BF16)
    st.add(VPU, "cast", 100, cast)

    def proj(j):
        return _dot(r.hb[...], r.w_in[:, _cols(j)])

    def p_s5():
        p = proj(0)
        for j in range(NSLAB):
            r.segin[j] = p[:, _slab(j)]
    st.add(PROJ, "p_s5", 320, p_s5, ["cast"])
    return proj


def _in_proj_rest(st, r, proj):
    for j in range(4):
        def p_hg(j=j):
            r.hgp[:, _cols(j)] = proj(4 + j)
        st.add(PROJ, f"p_hg{j}", 320, p_hg)

    def p_lru_x():
        r.lru_xbuf[SUBLANES:SUBLANES + TIME_BLOCK, :] = proj(8)
    st.add(PROJ, "p_lru_x", 320, p_lru_x)

    def p_lru_y():
        p = proj(9)
        for j in range(NSLAB):
            r.segin[NSLAB + j] = p[:, _slab(j)]
    st.add(PROJ, "p_lru_y", 320, p_lru_y)
    for j in range(3):
        def p_sc(j=j):
            r.scp[:, _cols(j)] = proj(1 + j)
        st.add(PROJ, f"p_sc{j}", 320, p_sc)


def _s5_head(st, r):
    def gather():
        for t in range(SEG):
            for j in range(NSLAB):
                r.uperm[_tile(t), _slab(j)] = r.segin[j, _seg_rows(t), :]
        r.bu[...] = _dot(r.uperm[...].astype(BF16), r.bmat[...])
    st.add(S5, "s5_gather", 300, gather, ["p_s5"])


def _s5_pieces(st, r):
    n = S5_W
    env = types.SimpleNamespace()

    def scan(first):
        lam_r = jnp.broadcast_to(r.lam[0:1, 0:n], (SUBLANES, n))
        lam_i = jnp.broadcast_to(r.lam[0:1, n:2 * n], (SUBLANES, n))
        for t in range(first, first + SCAN_PIECE):
            br, bi = r.bu[_tile(t), 0:n], r.bu[_tile(t), n:2 * n]
            if t == 0:
                env.xr, env.xi = br, bi
            else:
                env.xr, env.xi = (lam_r * env.xr - lam_i * env.xi + br,
                                  lam_r * env.xi + lam_i * env.xr + bi)
                r.bu[_tile(t), 0:n] = env.xr
                r.bu[_tile(t), n:2 * n] = env.xi
    for first in range(0, SEG, SCAN_PIECE):
        st.add(S5, f"s5_scan{first}", 22 * SCAN_PIECE, functools.partial(scan, first))

    def carry():
        r.s5_e[:, 0:n] = env.xr
        r.s5_e[:, n:2 * n] = env.xi
        pr, pi_ = r.lam[1:2, 0:n], r.lam[1:2, n:2 * n]
        cr, ci = r.s5_state[0:1, 0:n], r.s5_state[0:1, n:2 * n]
        for s in range(SUBLANES):
            r.s5_cin[s:s + 1, 0:n] = cr
            r.s5_cin[s:s + 1, n:2 * n] = ci
            sr, si = r.s5_e[s:s + 1, 0:n], r.s5_e[s:s + 1, n:2 * n]
            cr, ci = sr + (pr * cr - pi_ * ci), si + (pr * ci + pi_ * cr)
        r.s5_state[0:1, 0:n] = cr
        r.s5_state[0:1, n:2 * n] = ci
    st.add(S5, "s5_carry", 100, carry)

    def fix(first):
        cin_r = r.s5_cin[:, 0:n]
        cin_i = r.s5_cin[:, n:2 * n]
        for tt in range(first // 2, (first + SCAN_PIECE) // 2):
            parts_r, parts_i = [], []
            for t in (2 * tt, 2 * tt + 1):
                qr = jnp.broadcast_to(r.ptab[t:t + 1, 0:n], (SUBLANES, n))
                qi = jnp.broadcast_to(r.ptab[t:t + 1, n:2 * n], (SUBLANES, n))
                parts_r.append(r.bu[_tile(t), 0:n] + (qr * cin_r - qi * cin_i))
                parts_i.append(r.bu[_tile(t), n:2 * n] + (qr * cin_i + qi * cin_r))
            rows2 = slice(tt * 2 * SUBLANES, (tt + 1) * 2 * SUBLANES)
            r.xbf[rows2, 0:n] = jnp.concatenate(parts_r, axis=0).astype(BF16)
            r.xbf[rows2, n:2 * n] = jnp.concatenate(parts_i, axis=0).astype(BF16)
    for first in range(0, SEG, SCAN_PIECE):
        st.add(S5, f"s5_fix{first}", 32 * SCAN_PIECE, functools.partial(fix, first))

    def post():
        y = _dot(r.xbf[...], r.cmat[...])
        y = y + _vrow(r.vec, V_S5_D) * r.uperm[...]
        g = _gelu_tanh(y)
        gate = _sigmoid(_dot(g.astype(BF16), r.glu_w[...]) + _vrow(r.vec, V_GLU_B))
        r.uperm[...] = g * gate
        for t in range(SEG):
            for j in range(NSLAB):
                r.segout[j, _seg_rows(t), :] = r.uperm[_tile(t), _slab(j)]
    st.add(S5, "s5_post", 500, post)


def _short_conv_pieces(st, r):
    def conv():
        tb = TIME_BLOCK
        h, gb, gc = r.scp[:, 0:W_GROUP], r.scp[:, W_GROUP:2 * W_GROUP], r.scp[:, 2 * W_GROUP:3 * W_GROUP]
        r.sc_buf[SUBLANES:SUBLANES + tb, :] = gc * h
        out = (_vrow(r.vec, V_SC_W0) * r.sc_buf[SUBLANES - 2:SUBLANES - 2 + tb, :]
               + _vrow(r.vec, V_SC_W1) * r.sc_buf[SUBLANES - 1:SUBLANES - 1 + tb, :]
               + _vrow(r.vec, V_SC_W2) * r.sc_buf[SUBLANES:SUBLANES + tb, :])
        r.mix[:, W_GROUP:2 * W_GROUP] = (gb * out).astype(BF16)
        r.sc_buf[0:SUBLANES, :] = r.sc_buf[tb:tb + SUBLANES, :]
    st.add(VPU, "sc", 400, conv, ["p_sc2", f"out_proj{D_MODEL // MXU_N - 1}"])


def _hgrn2_pieces(st, r):
    c = HG_CHUNK
    env = types.SimpleNamespace()

    def stack_heads(x_bf):
        return jnp.concatenate([x_bf] * HG_HEADS, axis=0) * r.headmask[...]

    def gates():
        q_in = r.hgp[:, 0:W_GROUP]
        z = r.hgp[:, W_GROUP:2 * W_GROUP]
        q = _silu(q_in) * (HG_HEAD_DIM ** -0.5)
        log_sig = jnp.minimum(z, 0.0) - jnp.log(1.0 + jnp.exp(-jnp.abs(z)))
        cpl = _vrow(r.vec, V_HG_LOG1MLB) + log_sig
        loglb = _vrow(r.vec, V_HG_LOGLB)
        g = jnp.maximum(loglb, cpl) + jnp.log(1.0 + jnp.exp(-jnp.abs(loglb - cpl)))
        k = _vrow(r.vec, V_HG_1MLB) * _sigmoid(-z)
        r.hg_qt[HG_LEVELS + 1] = q.astype(BF16)
        r.hg_kt[HG_LEVELS + 1] = k.astype(BF16)
        env.q, env.k, env.cin, env.sfx = q, k, g, jnp.zeros_like(g)
    st.add(HG, "hg_gates", 600, gates, ["p_hg1"])

    def level(lvl):
        r.hg_qt[lvl] = (env.q * jnp.exp(env.cin)).astype(BF16)
        r.hg_kt[lvl] = (env.k * jnp.exp(env.sfx)).astype(BF16)
        if lvl == HG_LEVELS:
            r.hg_cin[...] = env.cin
            return
        n = 1 << lvl
        row = lax.broadcasted_iota(jnp.int32, (TIME_BLOCK, W_GROUP), 0)
        sib = _sibling(env.cin + env.sfx, n)
        right = (row & n) != 0
        env.cin = env.cin + jnp.where(right, sib, 0.0)
        env.sfx = env.sfx + jnp.where(right, 0.0, sib)
    for lvl in range(HG_LEVELS + 1):
        st.add(HG, f"hg_level{lvl}", 350, functools.partial(level, lvl))

    nchunk = TIME_BLOCK // c

    def chunk_rows(ci):
        return slice(ci * c, (ci + 1) * c)

    def intra(ci):
        rows = chunk_rows(ci)
        lv = r.lv[...]
        v = r.hgp[rows, 2 * W_GROUP:3 * W_GROUP].astype(BF16)
        scores = None
        for lvl in (HG_LEVELS + 1,) + tuple(range(HG_LEVELS)):
            s_n = _dot_nt(r.hg_qt[lvl, rows, :], stack_heads(r.hg_kt[lvl, rows, :]))
            code = HG_LEVELS if lvl == HG_LEVELS + 1 else lvl
            scores = jnp.where(lv == code, s_n, 0.0 if scores is None else scores)
        r.hg_o[rows, :] = _dot(scores.astype(BF16), stack_heads(v))
        r.hg_upd[ci] = _dot_tn(v, r.hg_kt[HG_LEVELS, rows, :]) * r.bdmask[...]
    for ci in range(nchunk):
        st.add(HG, f"hg_intra{ci}", 400, functools.partial(intra, ci), ["p_hg3"])

    def inter(ci):
        rows = chunk_rows(ci)
        state = r.hg_state[...]
        r.hg_o[rows, :] = r.hg_o[rows, :] + _dot_nt(r.hg_qt[HG_LEVELS, rows, :], state.astype(BF16))
        decay = jnp.exp(r.hg_cin[(ci + 1) * c - 1:(ci + 1) * c, :])
        r.hg_state[...] = state * decay + r.hg_upd[ci]
    for ci in range(nchunk):
        st.add(HG, f"hg_inter{ci}", 150, functools.partial(inter, ci))

    def norm():
        o = r.hg_o[...]
        ms = _dot((o * o).astype(BF16), r.ones[...])
        o = o * lax.rsqrt(ms + EPS) * _vrow(r.vec, V_HG_GNORM)
        g_in = r.hgp[:, 3 * W_GROUP:4 * W_GROUP]
        r.mix[:, 2 * W_GROUP:3 * W_GROUP] = (o * _silu(g_in)).astype(BF16)
    st.add(HG, "hg_norm", 300, norm, [f"out_proj{D_MODEL // MXU_N - 1}"])


def _rglru_pieces(st, r):
    tb = TIME_BLOCK
    env = types.SimpleNamespace()

    def seg_tile(ref, first, t):
        return jnp.concatenate([ref[first + j, _seg_rows(t), :] for j in range(NSLAB)], axis=1)

    def pre():
        xc = _vrow(r.vec, V_LRU_CB) + (
            _vrow(r.vec, V_LRU_W0) * r.lru_xbuf[SUBLANES - 3:SUBLANES - 3 + tb, :]
            + _vrow(r.vec, V_LRU_W1) * r.lru_xbuf[SUBLANES - 2:SUBLANES - 2 + tb, :]
            + _vrow(r.vec, V_LRU_W2) * r.lru_xbuf[SUBLANES - 1:SUBLANES - 1 + tb, :]
            + _vrow(r.vec, V_LRU_W3) * r.lru_xbuf[SUBLANES:SUBLANES + tb, :])
        r.lru_xbuf[0:SUBLANES, :] = r.lru_xbuf[tb:tb + SUBLANES, :]
        gates = _dot(xc.astype(BF16), r.wab[...])
        gate_a = _sigmoid(gates[:, 0:W_GROUP] + _vrow(r.vec, V_LRU_BA))
        gate_x = _sigmoid(gates[:, W_GROUP:2 * W_GROUP] + _vrow(r.vec, V_LRU_BX))
        log_a = _vrow(r.vec, V_LRU_CA) * gate_a
        a = jnp.exp(log_a)
        b = xc * gate_x * jnp.sqrt(-jnp.tanh(log_a) * (a * a + 1.0))
        for j in range(NSLAB):
            r.lru_a[j] = a[:, _slab(j)]
            r.lru_b[j] = b[:, _slab(j)]
    st.add(LRU, "lru_pre", 500, pre, ["p_lru_x"])

    def scan(first):
        for t in range(first, first + SCAN_PIECE):
            a_t, b_t = seg_tile(r.lru_a, 0, t), seg_tile(r.lru_b, 0, t)
            env.h, env.p = (b_t, a_t) if t == 0 else (a_t * env.h + b_t, a_t * env.p)
            r.lru_h[_tile(t), :] = env.h
            r.lru_p[_tile(t), :] = env.p
    for first in range(0, SEG, SCAN_PIECE):
        st.add(LRU, f"lru_scan{first}", 18 * SCAN_PIECE, functools.partial(scan, first))

    def carry():
        r.lru_e[0:SUBLANES, :] = env.h
        r.lru_e[SUBLANES:2 * SUBLANES, :] = env.p
        cur = r.lru_state[0:1, :]
        for s in range(SUBLANES):
            r.lru_cin[s:s + 1, :] = cur
            cur = r.lru_e[s:s + 1, :] + r.lru_e[SUBLANES + s:SUBLANES + s + 1, :] * cur
        r.lru_state[0:1, :] = cur
    st.add(LRU, "lru_carry", 50, carry)

    def fix(first):
        cin = r.lru_cin[...]
        for t in range(first, first + SCAN_PIECE):
            res = (r.lru_h[_tile(t), :] + r.lru_p[_tile(t), :] * cin) * _gelu_tanh(seg_tile(r.segin, NSLAB, t))
            for j in range(NSLAB):
                r.segout[NSLAB + j, _seg_rows(t), :] = res[:, _slab(j)]
    for first in range(0, SEG, SCAN_PIECE):
        st.add(LRU, f"lru_fix{first}", 38 * SCAN_PIECE, functools.partial(fix, first), ["p_lru_y"])


_SCRATCH = (
    ("hprev", (TIME_BLOCK, D_MODEL), F32),
    ("mix", (TIME_BLOCK, 4 * W_GROUP), BF16),
    ("res", (TIME_BLOCK, D_MODEL), F32),
    ("ybuf", (TIME_BLOCK, D_MODEL), F32),
    ("xcarry", (N_META, D_MODEL), F32),
    ("hn", (TIME_BLOCK, D_MODEL), F32),
    ("hnb", (TIME_BLOCK, D_MODEL), BF16),
    ("act", (TIME_BLOCK, D_FF), BF16),
    ("hb", (TIME_BLOCK, D_MODEL), BF16),
    ("hgp", (TIME_BLOCK, 4 * W_GROUP), F32),
    ("scp", (TIME_BLOCK, 3 * W_GROUP), F32),
    ("segin", (2 * NSLAB, TIME_BLOCK, LANES), F32),
    ("segout", (2 * NSLAB, TIME_BLOCK, LANES), F32),
    ("uperm", (TIME_BLOCK, W_GROUP), F32),
    ("bu", (TIME_BLOCK, 2 * S5_W), F32),
    ("xbf", (TIME_BLOCK, 2 * S5_W), BF16),
    ("s5_e", (SUBLANES, 2 * S5_W), F32),
    ("s5_cin", (SUBLANES, 2 * S5_W), F32),
    ("s5_state", (SUBLANES, 2 * S5_W), F32),
    ("sc_buf", (TIME_BLOCK + SUBLANES, W_GROUP), F32),
    ("hg_qt", (HG_LEVELS + 2, TIME_BLOCK, W_GROUP), BF16),
    ("hg_kt", (HG_LEVELS + 2, TIME_BLOCK, W_GROUP), BF16),
    ("hg_cin", (TIME_BLOCK, W_GROUP), F32),
    ("hg_state", (W_GROUP, W_GROUP), F32),
    ("hg_upd", (TIME_BLOCK // HG_CHUNK, W_GROUP, W_GROUP), F32),
    ("hg_o", (TIME_BLOCK, W_GROUP), F32),
    ("lru_xbuf", (TIME_BLOCK + SUBLANES, W_GROUP), F32),
    ("lru_a", (NSLAB, TIME_BLOCK, LANES), F32),
    ("lru_b", (NSLAB, TIME_BLOCK, LANES), F32),
    ("lru_h", (TIME_BLOCK, W_GROUP), F32),
    ("lru_p", (TIME_BLOCK, W_GROUP), F32),
    ("lru_e", (2 * SUBLANES, W_GROUP), F32),
    ("lru_cin", (SUBLANES, W_GROUP), F32),
    ("lru_state", (SUBLANES, W_GROUP), F32),
)
_CONSTS = ("w_in", "vec", "bmat", "cmat", "lam", "ptab", "glu_w", "wab", "headmask", "bdmask", "ones",
           "lv", "w_out", "ln", "w_ffn_in", "w_ffn_out")


def _layer_kernel(nblk, nsteps, first, last, n_tok, *refs):
    names = (("x", "meta") if first else ("h",)) + _CONSTS + ("out",) + tuple(s[0] for s in _SCRATCH)
    r = types.SimpleNamespace(**dict(zip(names, refs, strict=True)))
    step = pl.program_id(0)
    block = jnp.minimum(step, nsteps - 1) % nblk

    @pl.when(step == 0)
    def _():
        r.hprev[...] = jnp.zeros_like(r.hprev)
        r.mix[...] = jnp.zeros_like(r.mix)
        r.ybuf[...] = jnp.zeros_like(r.ybuf)

    if first:
        r.read_h = functools.partial(_first_layer_block, r, block, n_tok)

        @pl.when(block == 0)
        def _():
            r.xcarry[...] = r.meta[...]
    else:
        r.read_h = lambda: r.h[...]

    @pl.when(step % nblk == 0)
    def _():
        r.s5_state[...] = jnp.zeros_like(r.s5_state)
        r.sc_buf[0:SUBLANES, :] = jnp.zeros((SUBLANES, W_GROUP), F32)
        r.hg_state[...] = jnp.zeros_like(r.hg_state)
        r.lru_xbuf[0:SUBLANES, :] = jnp.zeros((SUBLANES, W_GROUP), F32)
        r.lru_state[...] = jnp.zeros_like(r.lru_state)

    def finish():
        for j in range(NSLAB):
            r.mix[:, _slab(j)] = r.segout[j].astype(BF16)
            r.mix[:, _slab(j, 3 * W_GROUP)] = r.segout[NSLAB + j].astype(BF16)
        r.hprev[...] = r.read_h()
        if first:
            r.xcarry[...] = r.x[TIME_BLOCK - N_META:TIME_BLOCK, :]

    def body(mixers, channel):
        st = _Streams()
        if mixers:
            proj = _in_proj_pieces(st, r)
        _ln2_pieces(st, r, N_META if last else 0, with_tail=channel)
        if channel:
            _out_proj_pieces(st, r)
        if mixers:
            _in_proj_rest(st, r, proj)
        if channel:
            _ffn_pieces(st, r)
        if mixers:
            _s5_head(st, r)
        if channel:
            _ln1_piece(st, r)
        if mixers:
            _s5_pieces(st, r)
            _hgrn2_pieces(st, r)
            _rglru_pieces(st, r)
            _short_conv_pieces(st, r)
            st.add(VPU, "finish", 200, finish,
                   [f"out_proj{D_MODEL // MXU_N - 1}", "s5_post", f"lru_fix{SEG - SCAN_PIECE}"])
        st.emit()

    pl.when(step < nsteps)(functools.partial(body, True, True))
    pl.when(step == nsteps)(functools.partial(body, False, True))
    pl.when(step == nsteps + 1)(functools.partial(body, False, False))


def _const_spec(shape):
    return pl.BlockSpec(shape, lambda *_: (0,) * len(shape), pipeline_mode=pl.Buffered(1))


def _layer_spec(shape, layer):
    return pl.BlockSpec((None,) + shape[1:], lambda *_: (layer, 0, 0), pipeline_mode=pl.Buffered(1))


def _layer_call(inputs, consts, bsz, seq, layer, first, last):
    tb = TIME_BLOCK
    n_tok = N_META + seq
    nblk = pl.cdiv(n_tok, tb)
    nsteps = bsz * nblk

    def in_block(n):
        return jnp.minimum(n, nsteps - 1)

    def out_block(n):
        return jnp.maximum(n - PIPELINE_DEPTH, 0)
    if first:
        in_specs = [pl.BlockSpec((None, tb, D_MODEL), lambda n: (in_block(n) // nblk, in_block(n) % nblk, 0)),
                    _const_spec((N_META, D_MODEL))]
    else:
        in_specs = [pl.BlockSpec((tb, D_MODEL), lambda n: (in_block(n), 0))]
    if last:
        out_spec = pl.BlockSpec((None, tb, D_MODEL), lambda n: (out_block(n) // nblk, out_block(n) % nblk, 0))
        out_shape = jax.ShapeDtypeStruct((bsz, seq, D_MODEL), F32)
    else:
        out_spec = pl.BlockSpec((tb, D_MODEL), lambda n: (out_block(n), 0))
        out_shape = jax.ShapeDtypeStruct((nsteps * tb, D_MODEL), F32)
    return pl.pallas_call(
        functools.partial(_layer_kernel, nblk, nsteps, first, last, n_tok),
        grid=(nsteps + PIPELINE_DEPTH,),
        in_specs=in_specs + [_layer_spec(c.shape, layer) if c.ndim == 3 else _const_spec(c.shape) for c in consts],
        out_specs=out_spec,
        out_shape=out_shape,
        scratch_shapes=[pltpu.VMEM(shape, dtype) for _, shape, dtype in _SCRATCH],
        compiler_params=pltpu.CompilerParams(
            dimension_semantics=("arbitrary",), vmem_limit_bytes=VMEM_LIMIT_BYTES),
    )(*inputs, *consts)


def _block_diag(blocks):
    layers, n, a, b = blocks.shape
    eye = jnp.eye(n, dtype=blocks.dtype)
    return jnp.einsum('lnab,nm->lnamb', blocks, eye).reshape(layers, n * a, n * b)


def _hgrn2_level_map():
    t = jnp.arange(HG_CHUNK, dtype=jnp.int32)[:, None]
    s = jnp.arange(HG_CHUNK, dtype=jnp.int32)[None, :]
    x = t ^ s
    lv = jnp.full((HG_CHUNK, HG_CHUNK), -1, jnp.int32)
    for level in range(HG_LEVELS):
        n = 1 << level
        lv = jnp.where((x >= n) & (x < 2 * n) & (t > s), level, lv)
    lv = jnp.where(t == s, HG_LEVELS, lv)
    return jnp.tile(lv, (1, HG_HEADS))


def _prepare_layers(lb, s5_lam_re, s5_lam_im, s5_b_re, s5_b_im, s5_c_re, s5_c_im, s5_d,
                    s5_log_dt, s5_glu_w, s5_glu_b, sc_conv_w, hg_gnorm, lru_conv_w, lru_conv_b,
                    lru_wa, lru_ba, lru_wx, lru_bx, lru_a_param):
    layers = lb.shape[0]
    lam_r, lam_i = s5_lam_re.astype(F32), s5_lam_im.astype(F32)
    dt = jnp.exp(s5_log_dt.astype(F32))[..., None]
    arg_r, arg_i = lam_r * dt, lam_i * dt

    def lam_bar_pow(ar, ai, k):
        mag = jnp.exp(ar * k)
        return mag * jnp.cos(ai * k), mag * jnp.sin(ai * k)
    bar_r, bar_i = lam_bar_pow(arg_r, arg_i, 1.0)
    num_r, num_i = bar_r - 1.0, bar_i
    den = lam_r * lam_r + lam_i * lam_i
    coef_r = ((num_r * lam_r + num_i * lam_i) / den)[..., None]
    coef_i = ((num_i * lam_r - num_r * lam_i) / den)[..., None]
    b_r, b_i = s5_b_re.astype(F32), s5_b_im.astype(F32)
    bbar_r = jnp.swapaxes(coef_r * b_r - coef_i * b_i, -1, -2)
    bbar_i = jnp.swapaxes(coef_r * b_i + coef_i * b_r, -1, -2)
    bmat = jnp.concatenate([_block_diag(bbar_r), _block_diag(bbar_i)], axis=-1)
    c_r = jnp.swapaxes(s5_c_re.astype(F32), -1, -2)
    c_i = jnp.swapaxes(s5_c_im.astype(F32), -1, -2)
    cmat = jnp.concatenate([_block_diag(c_r), -_block_diag(c_i)], axis=-2)

    def flat(zr, zi):
        return jnp.concatenate([zr.reshape(layers, -1), zi.reshape(layers, -1)], axis=-1)
    lam_rows = jnp.stack([flat(bar_r, bar_i), flat(*lam_bar_pow(arg_r, arg_i, float(SEG)))], axis=1)
    steps = jnp.arange(1, SEG + 1, dtype=F32)[None, :, None, None]
    pw_r, pw_i = lam_bar_pow(arg_r[:, None], arg_i[:, None], steps)
    ptab = jnp.concatenate([pw_r.reshape(layers, SEG, -1), pw_i.reshape(layers, SEG, -1)], axis=-1)

    rows = [None] * N_VEC
    rows[V_S5_D] = s5_d
    rows[V_GLU_B] = s5_glu_b
    rows[V_SC_W0], rows[V_SC_W1], rows[V_SC_W2] = sc_conv_w[:, 0], sc_conv_w[:, 1], sc_conv_w[:, 2]
    rows[V_HG_LOGLB] = jnp.maximum(jnp.log(lb), -1e30)
    rows[V_HG_LOG1MLB] = jnp.log1p(-lb)
    rows[V_HG_1MLB] = 1.0 - lb
    rows[V_HG_GNORM] = hg_gnorm
    for i, v in enumerate((V_LRU_W0, V_LRU_W1, V_LRU_W2, V_LRU_W3)):
        rows[v] = lru_conv_w[:, i]
    rows[V_LRU_CB] = lru_conv_b
    rows[V_LRU_CA] = -LRU_C * jax.nn.softplus(-lru_a_param.astype(F32))
    rows[V_LRU_BA] = lru_ba
    rows[V_LRU_BX] = lru_bx
    zero = jnp.zeros((layers, W_GROUP), F32)
    vec = jnp.stack([zero if r is None else r.astype(F32) for r in rows], axis=1)
    wab = jnp.concatenate([_block_diag(lru_wa.astype(F32)), _block_diag(lru_wx.astype(F32))], axis=-1)
    return (vec, bmat.astype(BF16), cmat.astype(BF16), lam_rows, ptab, s5_glu_w.astype(BF16), wab.astype(BF16))


def kernel(x, meta_tokens, hg_lb_raw, w_in, w_out, s5_lam_re, s5_lam_im, s5_b_re, s5_b_im, s5_c_re, s5_c_im, s5_d, s5_log_dt, s5_glu_w, s5_glu_b, sc_conv_w, hg_gnorm, lru_conv_w, lru_conv_b, lru_wa, lru_ba, lru_wx, lru_bx, lru_a_param, ln1_g, ln1_b, w_ffn_in, w_ffn_out, ln2_g, ln2_b):
    bsz, seq, d = x.shape
    assert d == D_MODEL
    inputs = (x.astype(F32), meta_tokens.astype(F32))

    lb_all = jnp.cumsum(jax.nn.softmax(hg_lb_raw.astype(F32), axis=0), axis=0)
    lb_all = lb_all - lb_all[0:1]

    head = jnp.arange(W_GROUP) // HG_HEAD_DIM
    same_head = head[:, None] == head[None, :]
    bdmask = same_head.astype(F32)
    headmask = same_head.astype(BF16)
    ones = (same_head.astype(F32) / HG_HEAD_DIM).astype(BF16)
    lv = _hgrn2_level_map()

    w_in_bf, w_out_bf = w_in.astype(BF16), w_out.astype(BF16)
    w_ffn_in_bf, w_ffn_out_bf = w_ffn_in.astype(BF16), w_ffn_out.astype(BF16)
    prep = _prepare_layers(lb_all, s5_lam_re, s5_lam_im, s5_b_re, s5_b_im, s5_c_re, s5_c_im,
                           s5_d, s5_log_dt, s5_glu_w, s5_glu_b, sc_conv_w, hg_gnorm, lru_conv_w,
                           lru_conv_b, lru_wa, lru_ba, lru_wx, lru_bx, lru_a_param)
    ln = jnp.stack([ln1_g, ln1_b, ln2_g, ln2_b], axis=1).astype(F32)
    consts = (w_in_bf,) + prep + (headmask, bdmask, ones, lv, w_out_bf, ln, w_ffn_in_bf, w_ffn_out_bf)
    for l in range(DEPTH):
        inputs = (_layer_call(inputs, consts, bsz, seq, l, first=l == 0, last=l == DEPTH - 1),)
    return inputs[0]
```

```python
import functools
import math
import types

import jax
import jax.numpy as jnp
import numpy as np
from jax import lax
from jax.experimental import pallas as pl
from jax.experimental.pallas import tpu as pltpu

F32 = jnp.float32
BF16 = jnp.bfloat16

D_MODEL = 1024
DEPTH = 2
N_META = 16
W_GROUP = 256
N_IN = 10 * W_GROUP
S5_GROUP = 16
S5_NGROUPS = 16
S5_STATE = 64
S5_W = S5_NGROUPS * S5_STATE
HG_HEADS = 4
HG_HEAD_DIM = 64
LRU_HEADS = 4
LRU_C = 8.0
D_FF = 2816
ALPHA = (2 * DEPTH) ** 0.25
EPS = 1e-5

SUBLANES = 8
LANES = 128
MXU_N = 256
NSLAB = W_GROUP // LANES
TIME_BLOCK = 320
SEG = TIME_BLOCK // SUBLANES
SCAN_PIECE = 8
PIPELINE_DEPTH = 2
HG_CHUNK = 64
HG_LEVELS = 6
VMEM_LIMIT_BYTES = 60 * 1024 * 1024

(V_S5_D, V_GLU_B, V_SC_W0, V_SC_W1, V_SC_W2, V_HG_LOGLB, V_HG_LOG1MLB, V_HG_1MLB, V_HG_GNORM,
 V_LRU_W0, V_LRU_W1, V_LRU_W2, V_LRU_W3, V_LRU_CB, V_LRU_CA, V_LRU_BA, V_LRU_BX) = range(17)
N_VEC = 24


def _sigmoid(x):
    return 0.5 + 0.5 * jnp.tanh(0.5 * x)


def _silu(x):
    h = 0.5 * x
    return h + h * jnp.tanh(h)


def _gelu_tanh(x):
    c = math.sqrt(2.0 / math.pi)
    return x * (0.5 * (1.0 + jnp.tanh(c * (x + 0.044715 * (x * x * x)))))


def _dot(a, b):
    return jnp.dot(a, b, preferred_element_type=F32)


def _dot_nt(a, b):
    return lax.dot_general(a, b, (((1,), (1,)), ((), ())), preferred_element_type=F32)


def _dot_tn(a, b):
    return lax.dot_general(a, b, (((0,), (0,)), ((), ())), preferred_element_type=F32)


def _tile(r):
    return slice(r * SUBLANES, (r + 1) * SUBLANES)


def _slab(j, first=0):
    return slice(first + j * LANES, first + (j + 1) * LANES)


def _cols(j, first=0):
    return slice(first + j * MXU_N, first + (j + 1) * MXU_N)


def _seg_rows(r):
    return pl.ds(r, SUBLANES, stride=SEG)


def _vrow(vec_ref, i):
    return vec_ref[i:i + 1, :]


def _layer_norm(x, g, b):
    mu = jnp.mean(x, axis=-1, keepdims=True)
    xc = x - mu
    var = jnp.mean(xc * xc, axis=-1, keepdims=True)
    return xc * lax.rsqrt(var + EPS) * g + b


def _sibling(x, n):
    c, w = x.shape
    if n < SUBLANES:
        x3 = x.reshape(c // SUBLANES, SUBLANES, w)
        fwd = pltpu.roll(x3, n, 1)
        if 2 * n == SUBLANES:
            return fwd.reshape(c, w)
        bwd = pltpu.roll(x3, SUBLANES - n, 1)
        row = lax.broadcasted_iota(jnp.int32, x3.shape, 1)
        return jnp.where((row & n) != 0, fwd, bwd).reshape(c, w)
    x4 = x.reshape(c // (2 * n), 2, n, w)
    return jnp.concatenate([x4[:, 1:2], x4[:, 0:1]], axis=1).reshape(c, w)


class _Streams:
    def __init__(self):
        self.lanes = {}

    def add(self, lane, name, cost, fn, deps=()):
        self.lanes.setdefault(lane, []).append((name, cost, tuple(deps), fn))

    def emit(self):
        total = {k: sum(p[1] for p in v) for k, v in self.lanes.items()}
        pos = dict.fromkeys(self.lanes, 0)
        spent = dict.fromkeys(self.lanes, 0)
        done = set()

        def ready(k):
            return pos[k] < len(self.lanes[k]) and all(d in done for d in self.lanes[k][pos[k]][2])
        while any(pos[k] < len(v) for k, v in self.lanes.items()):
            order = sorted(self.lanes, key=lambda k: spent[k] / total[k] * LANE_LAG.get(k, 1.0))
            pick = next((k for k in order if ready(k)), None)
            assert pick is not None, "piece dependencies cannot be met"
            name, cost, _, fn = self.lanes[pick][pos[pick]]
            fn()
            done.add(name)
            pos[pick] += 1
            spent[pick] += cost


MXU, PROJ, VPU, S5, HG, LRU, TAIL = "mxu", "proj", "misc", "s5", "hg", "lru", "tail"
LANE_LAG = {MXU: 0.7, PROJ: 0.3}


OUT_PIECE = MXU_N
N_OUT_PIECES = D_MODEL // OUT_PIECE
LAST_OUT_PROJ = f"out_proj{N_OUT_PIECES - 1}"
LAST_FFN_OUT = f"ffn_out{N_OUT_PIECES - 1}"


def _out_cols(j):
    return slice(j * OUT_PIECE, (j + 1) * OUT_PIECE)


def _out_proj_pieces(st, r):
    for j in range(N_OUT_PIECES):
        def out_proj(j=j):
            r.res[:, _out_cols(j)] = (ALPHA * r.hprev[:, _out_cols(j)]
                                      + _dot(r.mix[...], r.w_out[:, _out_cols(j)]))
        st.add(MXU, f"out_proj{j}", 320 * OUT_PIECE // MXU_N, out_proj)


def _ln1_piece(st, r):
    def ln1():
        hn = _layer_norm(r.res[...], r.ln[0:1, :], r.ln[1:2, :])
        r.hn[...] = hn
        r.hnb[...] = hn.astype(BF16)
    st.add(VPU, "ln1", 800, ln1, [LAST_OUT_PROJ])


def _ffn_pieces(st, r):
    nff = D_FF // MXU_N
    for j in range(nff):
        def ffn_in(j=j):
            hb = r.hnb[...]
            gate = _dot(hb, r.w_ffn_in[:, _cols(j)])
            up = _dot(hb, r.w_ffn_in[:, _cols(j, D_FF)])
            r.act[:, _cols(j)] = (_silu(gate) * up).astype(BF16)
        st.add(MXU, f"ffn_in{j}", 640, ffn_in, ["ln1"])

    for j in range(N_OUT_PIECES):
        def ffn_out(j=j):
            r.ybuf[:, _out_cols(j)] = (ALPHA * r.hn[:, _out_cols(j)]
                                       + _dot(r.act[...], r.w_ffn_out[:, _out_cols(j)]))
        st.add(MXU, f"ffn_out{j}", 880 * OUT_PIECE // MXU_N, ffn_out, [f"ffn_in{nff - 1}", "ln2"])


def _ln2_pieces(st, r, shift_rows, with_tail):
    def norm(y):
        return _layer_norm(y, r.ln[2:3, :], r.ln[3:4, :])

    def ln2():
        if shift_rows:
            r.out[0:TIME_BLOCK - shift_rows, :] = norm(r.ybuf[shift_rows:TIME_BLOCK, :])
        else:
            r.out[...] = norm(r.ybuf[...])
    st.add(VPU, "ln2", 800, ln2)

    def ln2_tail():
        r.out[TIME_BLOCK - shift_rows:TIME_BLOCK, :] = norm(r.ybuf[0:shift_rows, :])
    if shift_rows and with_tail:
        st.add(TAIL, "ln2_tail", 50, ln2_tail, [LAST_FFN_OUT])


def _first_layer_block(r, block, n_tok):
    keep = TIME_BLOCK - N_META
    row = lax.broadcasted_iota(jnp.int32, (keep, D_MODEL), 0) + (block * TIME_BLOCK + N_META)
    return jnp.concatenate([r.xcarry[...], jnp.where(row < n_tok, r.x[0:keep, :], 0.0)], axis=0)


def _in_proj_pieces(st, r):
    def cast():
        r.hb[...] = r.read_h().astype(BF16)
    st.add(VPU, "cast", 100, cast)

    def proj(j):
        return _dot(r.hb[...], r.w_in[:, _cols(j)])

    def p_s5():
        p = proj(0)
        for j in range(NSLAB):
            r.segin[j] = p[:, _slab(j)]
    st.add(PROJ, "p_s5", 320, p_s5, ["cast"])
    return proj


def _in_proj_rest(st, r, proj):
    for j in range(4):
        def p_hg(j=j):
            r.hgp[:, _cols(j)] = proj(4 + j)
        st.add(PROJ, f"p_hg{j}", 320, p_hg)

    def p_lru_x():
        r.lru_xbuf[SUBLANES:SUBLANES + TIME_BLOCK, :] = proj(8)
    st.add(PROJ, "p_lru_x", 320, p_lru_x)

    def p_lru_y():
        p = proj(9)
        for j in range(NSLAB):
            r.segin[NSLAB + j] = p[:, _slab(j)]
    st.add(PROJ, "p_lru_y", 320, p_lru_y)
    for j in range(3):
        def p_sc(j=j):
            r.scp[:, _cols(j)] = proj(1 + j)
        st.add(PROJ, f"p_sc{j}", 320, p_sc)


def _s5_head(st, r):
    def gather():
        for t in range(SEG):
            for j in range(NSLAB):
                r.uperm[_tile(t), _slab(j)] = r.segin[j, _seg_rows(t), :]
        r.bu[...] = _dot(r.uperm[...].astype(BF16), r.bmat[...])
    st.add(S5, "s5_gather", 300, gather, ["p_s5"])


def _s5_pieces(st, r):
    n = S5_W
    env = types.SimpleNamespace()

    def scan(first):
        lam_r = jnp.broadcast_to(r.lam[0:1, 0:n], (SUBLANES, n))
        lam_i = jnp.broadcast_to(r.lam[0:1, n:2 * n], (SUBLANES, n))
        for t in range(first, first + SCAN_PIECE):
            br, bi = r.bu[_tile(t), 0:n], r.bu[_tile(t), n:2 * n]
            if t == 0:
                env.xr, env.xi = br, bi
            else:
                env.xr, env.xi = (lam_r * env.xr - lam_i * env.xi + br,
                                  lam_r * env.xi + lam_i * env.xr + bi)
                r.bu[_tile(t), 0:n] = env.xr
                r.bu[_tile(t), n:2 * n] = env.xi
    for first in range(0, SEG, SCAN_PIECE):
        st.add(S5, f"s5_scan{first}", 22 * SCAN_PIECE, functools.partial(scan, first))

    def carry():
        r.s5_e[:, 0:n] = env.xr
        r.s5_e[:, n:2 * n] = env.xi
        pr, pi_ = r.lam[1:2, 0:n], r.lam[1:2, n:2 * n]
        cr, ci = r.s5_state[0:1, 0:n], r.s5_state[0:1, n:2 * n]
        for s in range(SUBLANES):
            r.s5_cin[s:s + 1, 0:n] = cr
            r.s5_cin[s:s + 1, n:2 * n] = ci
            sr, si = r.s5_e[s:s + 1, 0:n], r.s5_e[s:s + 1, n:2 * n]
            cr, ci = sr + (pr * cr - pi_ * ci), si + (pr * ci + pi_ * cr)
        r.s5_state[0:1, 0:n] = cr
        r.s5_state[0:1, n:2 * n] = ci
    st.add(S5, "s5_carry", 100, carry)

    def fix(first):
        cin_r = r.s5_cin[:, 0:n]
        cin_i = r.s5_cin[:, n:2 * n]
        for tt in range(first // 2, (first + SCAN_PIECE) // 2):
            parts_r, parts_i = [], []
            for t in (2 * tt, 2 * tt + 1):
                qr = jnp.broadcast_to(r.ptab[t:t + 1, 0:n], (SUBLANES, n))
                qi = jnp.broadcast_to(r.ptab[t:t + 1, n:2 * n], (SUBLANES, n))
                parts_r.append(r.bu[_tile(t), 0:n] + (qr * cin_r - qi * cin_i))
                parts_i.append(r.bu[_tile(t), n:2 * n] + (qr * cin_i + qi * cin_r))
            rows2 = slice(tt * 2 * SUBLANES, (tt + 1) * 2 * SUBLANES)
            r.xbf[rows2, 0:n] = jnp.concatenate(parts_r, axis=0).astype(BF16)
            r.xbf[rows2, n:2 * n] = jnp.concatenate(parts_i, axis=0).astype(BF16)
    for first in range(0, SEG, SCAN_PIECE):
        st.add(S5, f"s5_fix{first}", 32 * SCAN_PIECE, functools.partial(fix, first))

    def post():
        y = _dot(r.xbf[...], r.cmat[...])
        y = y + _vrow(r.vec, V_S5_D) * r.uperm[...]
        g = _gelu_tanh(y)
        gate = _sigmoid(_dot(g.astype(BF16), r.glu_w[...]) + _vrow(r.vec, V_GLU_B))
        r.uperm[...] = g * gate
        for t in range(SEG):
            for j in range(NSLAB):
                r.segout[j, _seg_rows(t), :] = r.uperm[_tile(t), _slab(j)]
    st.add(S5, "s5_post", 500, post)


def _short_conv_pieces(st, r):
    def conv():
        tb = TIME_BLOCK
        h, gb, gc = r.scp[:, 0:W_GROUP], r.scp[:, W_GROUP:2 * W_GROUP], r.scp[:, 2 * W_GROUP:3 * W_GROUP]
        r.sc_buf[SUBLANES:SUBLANES + tb, :] = gc * h
        out = (_vrow(r.vec, V_SC_W0) * r.sc_buf[SUBLANES - 2:SUBLANES - 2 + tb, :]
               + _vrow(r.vec, V_SC_W1) * r.sc_buf[SUBLANES - 1:SUBLANES - 1 + tb, :]
               + _vrow(r.vec, V_SC_W2) * r.sc_buf[SUBLANES:SUBLANES + tb, :])
        r.mix[:, W_GROUP:2 * W_GROUP] = (gb * out).astype(BF16)
        r.sc_buf[0:SUBLANES, :] = r.sc_buf[tb:tb + SUBLANES, :]
    st.add(VPU, "sc", 400, conv, ["p_sc2", LAST_OUT_PROJ])


def _hgrn2_pieces(st, r):
    c = HG_CHUNK
    env = types.SimpleNamespace()

    def stack_heads(x_bf):
        return jnp.concatenate([x_bf] * HG_HEADS, axis=0) * r.headmask[...]

    def gates():
        q_in = r.hgp[:, 0:W_GROUP]
        z = r.hgp[:, W_GROUP:2 * W_GROUP]
        q = _silu(q_in) * (HG_HEAD_DIM ** -0.5)
        log_sig = jnp.minimum(z, 0.0) - jnp.log(1.0 + jnp.exp(-jnp.abs(z)))
        cpl = _vrow(r.vec, V_HG_LOG1MLB) + log_sig
        loglb = _vrow(r.vec, V_HG_LOGLB)
        g = jnp.maximum(loglb, cpl) + jnp.log(1.0 + jnp.exp(-jnp.abs(loglb - cpl)))
        k = _vrow(r.vec, V_HG_1MLB) * _sigmoid(-z)
        r.hg_qt[HG_LEVELS + 1] = q.astype(BF16)
        r.hg_kt[HG_LEVELS + 1] = k.astype(BF16)
        env.q, env.k, env.cin, env.sfx = q, k, g, jnp.zeros_like(g)
    st.add(HG, "hg_gates", 600, gates, ["p_hg1"])

    def level(lvl):
        r.hg_qt[lvl] = (env.q * jnp.exp(env.cin)).astype(BF16)
        r.hg_kt[lvl] = (env.k * jnp.exp(env.sfx)).astype(BF16)
        if lvl == HG_LEVELS:
            r.hg_cin[...] = env.cin
            return
        n = 1 << lvl
        row = lax.broadcasted_iota(jnp.int32, (TIME_BLOCK, W_GROUP), 0)
        sib = _sibling(env.cin + env.sfx, n)
        right = (row & n) != 0
        env.cin = env.cin + jnp.where(right, sib, 0.0)
        env.sfx = env.sfx + jnp.where(right, 0.0, sib)
    for lvl in range(HG_LEVELS + 1):
        st.add(HG, f"hg_level{lvl}", 350, functools.partial(level, lvl))

    nchunk = TIME_BLOCK // c

    def chunk_rows(ci):
        return slice(ci * c, (ci + 1) * c)

    def intra(ci):
        rows = chunk_rows(ci)
        lv = r.lv[...]
        v = r.hgp[rows, 2 * W_GROUP:3 * W_GROUP].astype(BF16)
        scores = None
        for lvl in (HG_LEVELS + 1,) + tuple(range(HG_LEVELS)):
            s_n = _dot_nt(r.hg_qt[lvl, rows, :], stack_heads(r.hg_kt[lvl, rows, :]))
            code = HG_LEVELS if lvl == HG_LEVELS + 1 else lvl
            scores = jnp.where(lv == code, s_n, 0.0 if scores is None else scores)
        r.hg_o[rows, :] = _dot(scores.astype(BF16), stack_heads(v))
        r.hg_upd[ci] = _dot_tn(v, r.hg_kt[HG_LEVELS, rows, :]) * r.bdmask[...]
    for ci in range(nchunk):
        st.add(HG, f"hg_intra{ci}", 400, functools.partial(intra, ci), ["p_hg3"])

    def inter(ci):
        rows = chunk_rows(ci)
        state = r.hg_state[...]
        r.hg_o[rows, :] = r.hg_o[rows, :] + _dot_nt(r.hg_qt[HG_LEVELS, rows, :], state.astype(BF16))
        decay = jnp.exp(r.hg_cin[(ci + 1) * c - 1:(ci + 1) * c, :])
        r.hg_state[...] = state * decay + r.hg_upd[ci]
    for ci in range(nchunk):
        st.add(HG, f"hg_inter{ci}", 150, functools.partial(inter, ci))

    def norm():
        o = r.hg_o[...]
        ms = _dot((o * o).astype(BF16), r.ones[...])
        o = o * lax.rsqrt(ms + EPS) * _vrow(r.vec, V_HG_GNORM)
        g_in = r.hgp[:, 3 * W_GROUP:4 * W_GROUP]
        r.mix[:, 2 * W_GROUP:3 * W_GROUP] = (o * _silu(g_in)).astype(BF16)
    st.add(HG, "hg_norm", 300, norm, [LAST_OUT_PROJ])


def _rglru_pieces(st, r):
    tb = TIME_BLOCK
    env = types.SimpleNamespace()

    def seg_tile(ref, first, t):
        return jnp.concatenate([ref[first + j, _seg_rows(t), :] for j in range(NSLAB)], axis=1)

    def pre():
        xc = _vrow(r.vec, V_LRU_CB) + (
            _vrow(r.vec, V_LRU_W0) * r.lru_xbuf[SUBLANES - 3:SUBLANES - 3 + tb, :]
            + _vrow(r.vec, V_LRU_W1) * r.lru_xbuf[SUBLANES - 2:SUBLANES - 2 + tb, :]
            + _vrow(r.vec, V_LRU_W2) * r.lru_xbuf[SUBLANES - 1:SUBLANES - 1 + tb, :]
            + _vrow(r.vec, V_LRU_W3) * r.lru_xbuf[SUBLANES:SUBLANES + tb, :])
        r.lru_xbuf[0:SUBLANES, :] = r.lru_xbuf[tb:tb + SUBLANES, :]
        gates = _dot(xc.astype(BF16), r.wab[...])
        gate_a = _sigmoid(gates[:, 0:W_GROUP] + _vrow(r.vec, V_LRU_BA))
        gate_x = _sigmoid(gates[:, W_GROUP:2 * W_GROUP] + _vrow(r.vec, V_LRU_BX))
        log_a = _vrow(r.vec, V_LRU_CA) * gate_a
        a = jnp.exp(log_a)
        b = xc * gate_x * jnp.sqrt(-jnp.tanh(log_a) * (a * a + 1.0))
        for j in range(NSLAB):
            r.lru_a[j] = a[:, _slab(j)]
            r.lru_b[j] = b[:, _slab(j)]
    st.add(LRU, "lru_pre", 500, pre, ["p_lru_x"])

    def scan(first):
        for t in range(first, first + SCAN_PIECE):
            a_t, b_t = seg_tile(r.lru_a, 0, t), seg_tile(r.lru_b, 0, t)
            env.h, env.p = (b_t, a_t) if t == 0 else (a_t * env.h + b_t, a_t * env.p)
            r.lru_h[_tile(t), :] = env.h
            r.lru_p[_tile(t), :] = env.p
    for first in range(0, SEG, SCAN_PIECE):
        st.add(LRU, f"lru_scan{first}", 18 * SCAN_PIECE, functools.partial(scan, first))

    def carry():
        r.lru_e[0:SUBLANES, :] = env.h
        r.lru_e[SUBLANES:2 * SUBLANES, :] = env.p
        cur = r.lru_state[0:1, :]
        for s in range(SUBLANES):
            r.lru_cin[s:s + 1, :] = cur
            cur = r.lru_e[s:s + 1, :] + r.lru_e[SUBLANES + s:SUBLANES + s + 1, :] * cur
        r.lru_state[0:1, :] = cur
    st.add(LRU, "lru_carry", 50, carry)

    def fix(first):
        cin = r.lru_cin[...]
        for t in range(first, first + SCAN_PIECE):
            res = (r.lru_h[_tile(t), :] + r.lru_p[_tile(t), :] * cin) * _gelu_tanh(seg_tile(r.segin, NSLAB, t))
            for j in range(NSLAB):
                r.segout[NSLAB + j, _seg_rows(t), :] = res[:, _slab(j)]
    for first in range(0, SEG, SCAN_PIECE):
        st.add(LRU, f"lru_fix{first}", 38 * SCAN_PIECE, functools.partial(fix, first), ["p_lru_y"])


_SCRATCH = (
    ("hprev", (TIME_BLOCK, D_MODEL), F32),
    ("mix", (TIME_BLOCK, 4 * W_GROUP), BF16),
    ("res", (TIME_BLOCK, D_MODEL), F32),
    ("ybuf", (TIME_BLOCK, D_MODEL), F32),
    ("xcarry", (N_META, D_MODEL), F32),
    ("hn", (TIME_BLOCK, D_MODEL), F32),
    ("hnb", (TIME_BLOCK, D_MODEL), BF16),
    ("act", (TIME_BLOCK, D_FF), BF16),
    ("hb", (TIME_BLOCK, D_MODEL), BF16),
    ("hgp", (TIME_BLOCK, 4 * W_GROUP), F32),
    ("scp", (TIME_BLOCK, 3 * W_GROUP), F32),
    ("segin", (2 * NSLAB, TIME_BLOCK, LANES), F32),
    ("segout", (2 * NSLAB, TIME_BLOCK, LANES), F32),
    ("uperm", (TIME_BLOCK, W_GROUP), F32),
    ("bu", (TIME_BLOCK, 2 * S5_W), F32),
    ("xbf", (TIME_BLOCK, 2 * S5_W), BF16),
    ("s5_e", (SUBLANES, 2 * S5_W), F32),
    ("s5_cin", (SUBLANES, 2 * S5_W), F32),
    ("s5_state", (SUBLANES, 2 * S5_W), F32),
    ("sc_buf", (TIME_BLOCK + SUBLANES, W_GROUP), F32),
    ("hg_qt", (HG_LEVELS + 2, TIME_BLOCK, W_GROUP), BF16),
    ("hg_kt", (HG_LEVELS + 2, TIME_BLOCK, W_GROUP), BF16),
    ("hg_cin", (TIME_BLOCK, W_GROUP), F32),
    ("hg_state", (W_GROUP, W_GROUP), F32),
    ("hg_upd", (TIME_BLOCK // HG_CHUNK, W_GROUP, W_GROUP), F32),
    ("hg_o", (TIME_BLOCK, W_GROUP), F32),
    ("lru_xbuf", (TIME_BLOCK + SUBLANES, W_GROUP), F32),
    ("lru_a", (NSLAB, TIME_BLOCK, LANES), F32),
    ("lru_b", (NSLAB, TIME_BLOCK, LANES), F32),
    ("lru_h", (TIME_BLOCK, W_GROUP), F32),
    ("lru_p", (TIME_BLOCK, W_GROUP), F32),
    ("lru_e", (2 * SUBLANES, W_GROUP), F32),
    ("lru_cin", (SUBLANES, W_GROUP), F32),
    ("lru_state", (SUBLANES, W_GROUP), F32),
)
_CONSTS = ("w_in", "vec", "bmat", "cmat", "lam", "ptab", "glu_w", "wab", "headmask", "bdmask", "ones",
           "lv", "w_out", "ln", "w_ffn_in", "w_ffn_out")


def _layer_kernel(nblk, nsteps, first, last, n_tok, *refs):
    names = (("x", "meta") if first else ("h",)) + _CONSTS + ("out",) + tuple(s[0] for s in _SCRATCH)
    r = types.SimpleNamespace(**dict(zip(names, refs, strict=True)))
    step = pl.program_id(0)
    block = jnp.minimum(step, nsteps - 1) % nblk

    @pl.when(step == 0)
    def _():
        r.hprev[...] = jnp.zeros_like(r.hprev)
        r.mix[...] = jnp.zeros_like(r.mix)
        r.ybuf[...] = jnp.zeros_like(r.ybuf)

    if first:
        r.read_h = functools.partial(_first_layer_block, r, block, n_tok)

        @pl.when(block == 0)
        def _():
            r.xcarry[...] = r.meta[...]
    else:
        r.read_h = lambda: r.h[...]

    @pl.when(step % nblk == 0)
    def _():
        r.s5_state[...] = jnp.zeros_like(r.s5_state)
        r.sc_buf[0:SUBLANES, :] = jnp.zeros((SUBLANES, W_GROUP), F32)
        r.hg_state[...] = jnp.zeros_like(r.hg_state)
        r.lru_xbuf[0:SUBLANES, :] = jnp.zeros((SUBLANES, W_GROUP), F32)
        r.lru_state[...] = jnp.zeros_like(r.lru_state)

    def finish():
        for j in range(NSLAB):
            r.mix[:, _slab(j)] = r.segout[j].astype(BF16)
            r.mix[:, _slab(j, 3 * W_GROUP)] = r.segout[NSLAB + j].astype(BF16)
        r.hprev[...] = r.read_h()
        if first:
            r.xcarry[...] = r.x[TIME_BLOCK - N_META:TIME_BLOCK, :]

    def body(mixers, channel):
        st = _Streams()
        if mixers:
            proj = _in_proj_pieces(st, r)
        _ln2_pieces(st, r, N_META if last else 0, with_tail=channel)
        if channel:
            _out_proj_pieces(st, r)
        if mixers:
            _in_proj_rest(st, r, proj)
        if channel:
            _ffn_pieces(st, r)
        if mixers:
            _s5_head(st, r)
        if channel:
            _ln1_piece(st, r)
        if mixers:
            _s5_pieces(st, r)
            _hgrn2_pieces(st, r)
            _rglru_pieces(st, r)
            _short_conv_pieces(st, r)
            st.add(VPU, "finish", 200, finish,
                   [LAST_OUT_PROJ, "s5_post", f"lru_fix{SEG - SCAN_PIECE}"])
        st.emit()

    pl.when(step < nsteps)(functools.partial(body, True, True))
    pl.when(step == nsteps)(functools.partial(body, False, True))
    pl.when(step == nsteps + 1)(functools.partial(body, False, False))


def _const_spec(shape):
    return pl.BlockSpec(shape, lambda *_: (0,) * len(shape), pipeline_mode=pl.Buffered(1))


def _layer_spec(shape, layer):
    return pl.BlockSpec((None,) + shape[1:], lambda *_: (layer, 0, 0), pipeline_mode=pl.Buffered(1))


def _layer_call(inputs, consts, bsz, seq, layer, first, last):
    tb = TIME_BLOCK
    n_tok = N_META + seq
    nblk = pl.cdiv(n_tok, tb)
    nsteps = bsz * nblk

    def in_block(n):
        return jnp.minimum(n, nsteps - 1)

    def out_block(n):
        return jnp.maximum(n - PIPELINE_DEPTH, 0)
    if first:
        in_specs = [pl.BlockSpec((None, tb, D_MODEL), lambda n: (in_block(n) // nblk, in_block(n) % nblk, 0)),
                    _const_spec((N_META, D_MODEL))]
    else:
        in_specs = [pl.BlockSpec((tb, D_MODEL), lambda n: (in_block(n), 0))]
    if last:
        out_spec = pl.BlockSpec((None, tb, D_MODEL), lambda n: (out_block(n) // nblk, out_block(n) % nblk, 0))
        out_shape = jax.ShapeDtypeStruct((bsz, seq, D_MODEL), F32)
    else:
        out_spec = pl.BlockSpec((tb, D_MODEL), lambda n: (out_block(n), 0))
        out_shape = jax.ShapeDtypeStruct((nsteps * tb, D_MODEL), F32)
    return pl.pallas_call(
        functools.partial(_layer_kernel, nblk, nsteps, first, last, n_tok),
        grid=(nsteps + PIPELINE_DEPTH,),
        in_specs=in_specs + [_layer_spec(c.shape, layer) if c.ndim == 3 else _const_spec(c.shape) for c in consts],
        out_specs=out_spec,
        out_shape=out_shape,
        scratch_shapes=[pltpu.VMEM(shape, dtype) for _, shape, dtype in _SCRATCH],
        compiler_params=pltpu.CompilerParams(
            dimension_semantics=("arbitrary",), vmem_limit_bytes=VMEM_LIMIT_BYTES),
    )(*inputs, *consts)


def _block_diag(blocks):
    layers, n, a, b = blocks.shape
    eye = jnp.eye(n, dtype=blocks.dtype)
    return jnp.einsum('lnab,nm->lnamb', blocks, eye).reshape(layers, n * a, n * b)


def _hgrn2_level_map():
    t = np.arange(HG_CHUNK, dtype=np.int32)[:, None]
    s = np.arange(HG_CHUNK, dtype=np.int32)[None, :]
    x = t ^ s
    lv = np.full((HG_CHUNK, HG_CHUNK), -1, np.int32)
    for level in range(HG_LEVELS):
        n = 1 << level
        lv = np.where((x >= n) & (x < 2 * n) & (t > s), level, lv)
    lv = np.where(t == s, HG_LEVELS, lv)
    return jnp.asarray(np.tile(lv, (1, HG_HEADS)).astype(np.int32))


def _prepare_layers(lb, s5_lam_re, s5_lam_im, s5_b_re, s5_b_im, s5_c_re, s5_c_im, s5_d,
                    s5_log_dt, s5_glu_w, s5_glu_b, sc_conv_w, hg_gnorm, lru_conv_w, lru_conv_b,
                    lru_wa, lru_ba, lru_wx, lru_bx, lru_a_param):
    layers = lb.shape[0]
    lam_r, lam_i = s5_lam_re.astype(F32), s5_lam_im.astype(F32)
    dt = jnp.exp(s5_log_dt.astype(F32))[..., None]
    arg_r, arg_i = lam_r * dt, lam_i * dt

    def lam_bar_pow(ar, ai, k):
        mag = jnp.exp(ar * k)
        return mag * jnp.cos(ai * k), mag * jnp.sin(ai * k)
    bar_r, bar_i = lam_bar_pow(arg_r, arg_i, 1.0)
    num_r, num_i = bar_r - 1.0, bar_i
    den = lam_r * lam_r + lam_i * lam_i
    coef_r = ((num_r * lam_r + num_i * lam_i) / den)[..., None]
    coef_i = ((num_i * lam_r - num_r * lam_i) / den)[..., None]
    b_r, b_i = s5_b_re.astype(F32), s5_b_im.astype(F32)
    bbar_r = jnp.swapaxes(coef_r * b_r - coef_i * b_i, -1, -2)
    bbar_i = jnp.swapaxes(coef_r * b_i + coef_i * b_r, -1, -2)
    bmat = jnp.concatenate([_block_diag(bbar_r), _block_diag(bbar_i)], axis=-1)
    c_r = jnp.swapaxes(s5_c_re.astype(F32), -1, -2)
    c_i = jnp.swapaxes(s5_c_im.astype(F32), -1, -2)
    cmat = jnp.concatenate([_block_diag(c_r), -_block_diag(c_i)], axis=-2)

    def flat(zr, zi):
        return jnp.concatenate([zr.reshape(layers, -1), zi.reshape(layers, -1)], axis=-1)
    lam_rows = jnp.stack([flat(bar_r, bar_i), flat(*lam_bar_pow(arg_r, arg_i, float(SEG)))], axis=1)
    steps = jnp.arange(1, SEG + 1, dtype=F32)[None, :, None, None]
    pw_r, pw_i = lam_bar_pow(arg_r[:, None], arg_i[:, None], steps)
    ptab = jnp.concatenate([pw_r.reshape(layers, SEG, -1), pw_i.reshape(layers, SEG, -1)], axis=-1)

    rows = [None] * N_VEC
    rows[V_S5_D] = s5_d
    rows[V_GLU_B] = s5_glu_b
    rows[V_SC_W0], rows[V_SC_W1], rows[V_SC_W2] = sc_conv_w[:, 0], sc_conv_w[:, 1], sc_conv_w[:, 2]
    rows[V_HG_LOGLB] = jnp.maximum(jnp.log(lb), -1e30)
    rows[V_HG_LOG1MLB] = jnp.log1p(-lb)
    rows[V_HG_1MLB] = 1.0 - lb
    rows[V_HG_GNORM] = hg_gnorm
    for i, v in enumerate((V_LRU_W0, V_LRU_W1, V_LRU_W2, V_LRU_W3)):
        rows[v] = lru_conv_w[:, i]
    rows[V_LRU_CB] = lru_conv_b
    rows[V_LRU_CA] = -LRU_C * jax.nn.softplus(-lru_a_param.astype(F32))
    rows[V_LRU_BA] = lru_ba
    rows[V_LRU_BX] = lru_bx
    zero = jnp.zeros((layers, W_GROUP), F32)
    vec = jnp.stack([zero if r is None else r.astype(F32) for r in rows], axis=1)
    wab = jnp.concatenate([_block_diag(lru_wa.astype(F32)), _block_diag(lru_wx.astype(F32))], axis=-1)
    return (vec, bmat.astype(BF16), cmat.astype(BF16), lam_rows, ptab, s5_glu_w.astype(BF16), wab.astype(BF16))


def kernel(x, meta_tokens, hg_lb_raw, w_in, w_out, s5_lam_re, s5_lam_im, s5_b_re, s5_b_im, s5_c_re, s5_c_im, s5_d, s5_log_dt, s5_glu_w, s5_glu_b, sc_conv_w, hg_gnorm, lru_conv_w, lru_conv_b, lru_wa, lru_ba, lru_wx, lru_bx, lru_a_param, ln1_g, ln1_b, w_ffn_in, w_ffn_out, ln2_g, ln2_b):
    bsz, seq, d = x.shape
    assert d == D_MODEL
    inputs = (x.astype(F32), meta_tokens.astype(F32))

    lb_all = jnp.cumsum(jax.nn.softmax(hg_lb_raw.astype(F32), axis=0), axis=0)
    lb_all = lb_all - lb_all[0:1]

    head = np.arange(W_GROUP) // HG_HEAD_DIM
    same_head = (head[:, None] == head[None, :]).astype(np.float32)
    bdmask = jnp.asarray(same_head)
    headmask = jnp.asarray(same_head, dtype=BF16)
    ones = jnp.asarray(same_head / HG_HEAD_DIM, dtype=BF16)
    lv = _hgrn2_level_map()

    w_in_bf, w_out_bf = w_in.astype(BF16), w_out.astype(BF16)
    w_ffn_in_bf, w_ffn_out_bf = w_ffn_in.astype(BF16), w_ffn_out.astype(BF16)
    prep = _prepare_layers(lb_all, s5_lam_re, s5_lam_im, s5_b_re, s5_b_im, s5_c_re, s5_c_im,
                           s5_d, s5_log_dt, s5_glu_w, s5_glu_b, sc_conv_w, hg_gnorm, lru_conv_w,
                           lru_conv_b, lru_wa, lru_ba, lru_wx, lru_bx, lru_a_param)
    ln = jnp.stack([ln1_g, ln1_b, ln2_g, ln2_b], axis=1).astype(F32)
    consts = (w_in_bf,) + prep + (headmask, bdmask, ones, lv, w_out_bf, ln, w_ffn_in_bf, w_ffn_out_bf)
    for l in range(DEPTH):
        inputs = (_layer_call(inputs, consts, bsz, seq, l, first=l == 0, last=l == DEPTH - 1),)
    return inputs[0]
```

```python
import functools
import math
import types

import jax
import jax.numpy as jnp
import numpy as np
from jax import lax
from jax.experimental import pallas as pl
from jax.experimental.pallas import tpu as pltpu

F32 = jnp.float32
BF16 = jnp.bfloat16

D_MODEL = 1024
DEPTH = 2
N_META = 16
W_GROUP = 256
N_IN = 10 * W_GROUP
S5_GROUP = 16
S5_NGROUPS = 16
S5_STATE = 64
S5_W = S5_NGROUPS * S5_STATE
HG_HEADS = 4
HG_HEAD_DIM = 64
LRU_HEADS = 4
LRU_C = 8.0
D_FF = 2816
ALPHA = (2 * DEPTH) ** 0.25
EPS = 1e-5

SUBLANES = 8
LANES = 128
MXU_N = 256
NSLAB = W_GROUP // LANES
TIME_BLOCK = 320
SEG = TIME_BLOCK // SUBLANES
SCAN_PIECE = 8
PIPELINE_DEPTH = 2
HG_CHUNK = 64
HG_LEVELS = 6
VMEM_LIMIT_BYTES = 60 * 1024 * 1024

(V_S5_D, V_GLU_B, V_SC_W0, V_SC_W1, V_SC_W2, V_HG_LOGLB, V_HG_LOG1MLB, V_HG_1MLB, V_HG_GNORM,
 V_LRU_W0, V_LRU_W1, V_LRU_W2, V_LRU_W3, V_LRU_CB, V_LRU_CA, V_LRU_BA, V_LRU_BX) = range(17)
N_VEC = 24


def _sigmoid(x):
    return 0.5 + 0.5 * jnp.tanh(0.5 * x)


def _silu(x):
    h = 0.5 * x
    return h + h * jnp.tanh(h)


def _gelu_tanh(x):
    c = math.sqrt(2.0 / math.pi)
    return x * (0.5 * (1.0 + jnp.tanh(c * (x + 0.044715 * (x * x * x)))))


def _dot(a, b):
    return jnp.dot(a, b, preferred_element_type=F32)


def _dot_nt(a, b):
    return lax.dot_general(a, b, (((1,), (1,)), ((), ())), preferred_element_type=F32)


def _dot_tn(a, b):
    return lax.dot_general(a, b, (((0,), (0,)), ((), ())), preferred_element_type=F32)


def _tile(r):
    return slice(r * SUBLANES, (r + 1) * SUBLANES)


def _slab(j, first=0):
    return slice(first + j * LANES, first + (j + 1) * LANES)


def _cols(j, first=0):
    return slice(first + j * MXU_N, first + (j + 1) * MXU_N)


def _seg_rows(r):
    return pl.ds(r, SUBLANES, stride=SEG)


def _vrow(vec_ref, i):
    return vec_ref[i:i + 1, :]


def _layer_norm(x, g, b):
    mu = jnp.mean(x, axis=-1, keepdims=True)
    xc = x - mu
    var = jnp.mean(xc * xc, axis=-1, keepdims=True)
    return xc * lax.rsqrt(var + EPS) * g + b


def _sibling(x, n):
    c, w = x.shape
    if n < SUBLANES:
        x3 = x.reshape(c // SUBLANES, SUBLANES, w)
        fwd = pltpu.roll(x3, n, 1)
        if 2 * n == SUBLANES:
            return fwd.reshape(c, w)
        bwd = pltpu.roll(x3, SUBLANES - n, 1)
        row = lax.broadcasted_iota(jnp.int32, x3.shape, 1)
        return jnp.where((row & n) != 0, fwd, bwd).reshape(c, w)
    x4 = x.reshape(c // (2 * n), 2, n, w)
    return jnp.concatenate([x4[:, 1:2], x4[:, 0:1]], axis=1).reshape(c, w)


class _Streams:
    def __init__(self):
        self.lanes = {}

    def add(self, lane, name, cost, fn, deps=()):
        self.lanes.setdefault(lane, []).append((name, cost, tuple(deps), fn))

    def emit(self):
        total = {k: sum(p[1] for p in v) for k, v in self.lanes.items()}
        pos = dict.fromkeys(self.lanes, 0)
        spent = dict.fromkeys(self.lanes, 0)
        done = set()

        def ready(k):
            return pos[k] < len(self.lanes[k]) and all(d in done for d in self.lanes[k][pos[k]][2])
        while any(pos[k] < len(v) for k, v in self.lanes.items()):
            order = sorted(self.lanes, key=lambda k: spent[k] / total[k] * LANE_LAG.get(k, 1.0))
            pick = next((k for k in order if ready(k)), None)
            assert pick is not None, "piece dependencies cannot be met"
            name, cost, _, fn = self.lanes[pick][pos[pick]]
            fn()
            done.add(name)
            pos[pick] += 1
            spent[pick] += cost


MXU, PROJ, VPU, S5, HG, LRU, TAIL = "mxu", "proj", "misc", "s5", "hg", "lru", "tail"
LANE_LAG = {MXU: 0.7, PROJ: 0.3}


def _matmul_cost(k, n):
    rows_per_push, cycles_per_push, units = 16, 8, 2
    return TIME_BLOCK // rows_per_push * cycles_per_push * (k // MXU_N) * (n // MXU_N) // units


OUT_PIECE = MXU_N
N_OUT_PIECES = D_MODEL // OUT_PIECE
LAST_OUT_PROJ = f"out_proj{N_OUT_PIECES - 1}"
LAST_FFN_OUT = f"ffn_out{N_OUT_PIECES - 1}"


def _out_cols(j):
    return slice(j * OUT_PIECE, (j + 1) * OUT_PIECE)


def _out_proj_pieces(st, r):
    for j in range(N_OUT_PIECES):
        def out_proj(j=j):
            r.res[:, _out_cols(j)] = (ALPHA * r.hprev[:, _out_cols(j)]
                                      + _dot(r.mix[...], r.w_out[:, _out_cols(j)]))
        st.add(MXU, f"out_proj{j}", _matmul_cost(4 * W_GROUP, OUT_PIECE), out_proj)


def _ln1_piece(st, r):
    def ln1():
        hn = _layer_norm(r.res[...], r.ln[0:1, :], r.ln[1:2, :])
        r.hn[...] = hn
        r.hnb[...] = hn.astype(BF16)
    st.add(VPU, "ln1", 800, ln1, [LAST_OUT_PROJ])


def _ffn_pieces(st, r):
    nff = D_FF // MXU_N
    for j in range(nff):
        def ffn_in(j=j):
            hb = r.hnb[...]
            gate = _dot(hb, r.w_ffn_in[:, _cols(j)])
            up = _dot(hb, r.w_ffn_in[:, _cols(j, D_FF)])
            r.act[:, _cols(j)] = (_silu(gate) * up).astype(BF16)
        st.add(MXU, f"ffn_in{j}", 2 * _matmul_cost(D_MODEL, MXU_N), ffn_in, ["ln1"])

    for j in range(N_OUT_PIECES):
        def ffn_out(j=j):
            r.ybuf[:, _out_cols(j)] = (ALPHA * r.hn[:, _out_cols(j)]
                                       + _dot(r.act[...], r.w_ffn_out[:, _out_cols(j)]))
        st.add(MXU, f"ffn_out{j}", _matmul_cost(D_FF, OUT_PIECE), ffn_out, [f"ffn_in{nff - 1}", "ln2"])


def _ln2_pieces(st, r, shift_rows, with_tail):
    def norm(y):
        return _layer_norm(y, r.ln[2:3, :], r.ln[3:4, :])

    def ln2():
        if shift_rows:
            r.out[0:TIME_BLOCK - shift_rows, :] = norm(r.ybuf[shift_rows:TIME_BLOCK, :])
        else:
            r.out[...] = norm(r.ybuf[...])
    st.add(VPU, "ln2", 800, ln2)

    def ln2_tail():
        r.out[TIME_BLOCK - shift_rows:TIME_BLOCK, :] = norm(r.ybuf[0:shift_rows, :])
    if shift_rows and with_tail:
        st.add(TAIL, "ln2_tail", 50, ln2_tail, [LAST_FFN_OUT])


def _first_layer_block(r, block, n_tok):
    keep = TIME_BLOCK - N_META
    row = lax.broadcasted_iota(jnp.int32, (keep, D_MODEL), 0) + (block * TIME_BLOCK + N_META)
    return jnp.concatenate([r.xcarry[...], jnp.where(row < n_tok, r.x[0:keep, :], 0.0)], axis=0)


def _in_proj_pieces(st, r):
    def cast():
        r.hb[...] = r.read_h().astype(BF16)
    st.add(VPU, "cast", 100, cast)

    def proj(j):
        return _dot(r.hb[...], r.w_in[:, _cols(j)])

    def p_s5():
        p = proj(0)
        for j in range(NSLAB):
            r.segin[j] = p[:, _slab(j)]
    st.add(PROJ, "p_s5", _matmul_cost(D_MODEL, MXU_N), p_s5, ["cast"])
    return proj


def _in_proj_rest(st, r, proj):
    for j in range(4):
        def p_hg(j=j):
            r.hgp[:, _cols(j)] = proj(4 + j)
        st.add(PROJ, f"p_hg{j}", _matmul_cost(D_MODEL, MXU_N), p_hg)

    def p_lru_x():
        r.lru_xbuf[SUBLANES:SUBLANES + TIME_BLOCK, :] = proj(8)
    st.add(PROJ, "p_lru_x", _matmul_cost(D_MODEL, MXU_N), p_lru_x)

    def p_lru_y():
        p = proj(9)
        for j in range(NSLAB):
            r.segin[NSLAB + j] = p[:, _slab(j)]
    st.add(PROJ, "p_lru_y", _matmul_cost(D_MODEL, MXU_N), p_lru_y)
    for j in range(3):
        def p_sc(j=j):
            r.scp[:, _cols(j)] = proj(1 + j)
        st.add(PROJ, f"p_sc{j}", _matmul_cost(D_MODEL, MXU_N), p_sc)


def _s5_head(st, r):
    def gather():
        for t in range(SEG):
            for j in range(NSLAB):
                r.uperm[_tile(t), _slab(j)] = r.segin[j, _seg_rows(t), :]
        r.bu[...] = _dot(r.uperm[...].astype(BF16), r.bmat[...])
    st.add(S5, "s5_gather", 300, gather, ["p_s5"])


def _s5_pieces(st, r):
    n = S5_W
    env = types.SimpleNamespace()

    def scan(first):
        lam_r = jnp.broadcast_to(r.lam[0:1, 0:n], (SUBLANES, n))
        lam_i = jnp.broadcast_to(r.lam[0:1, n:2 * n], (SUBLANES, n))
        for t in range(first, first + SCAN_PIECE):
            br, bi = r.bu[_tile(t), 0:n], r.bu[_tile(t), n:2 * n]
            if t == 0:
                env.xr, env.xi = br, bi
            else:
                env.xr, env.xi = (lam_r * env.xr - lam_i * env.xi + br,
                                  lam_r * env.xi + lam_i * env.xr + bi)
                r.bu[_tile(t), 0:n] = env.xr
                r.bu[_tile(t), n:2 * n] = env.xi
    for first in range(0, SEG, SCAN_PIECE):
        st.add(S5, f"s5_scan{first}", 22 * SCAN_PIECE, functools.partial(scan, first))

    def carry():
        r.s5_e[:, 0:n] = env.xr
        r.s5_e[:, n:2 * n] = env.xi
        pr, pi_ = r.lam[1:2, 0:n], r.lam[1:2, n:2 * n]
        cr, ci = r.s5_state[0:1, 0:n], r.s5_state[0:1, n:2 * n]
        for s in range(SUBLANES):
            r.s5_cin[s:s + 1, 0:n] = cr
            r.s5_cin[s:s + 1, n:2 * n] = ci
            sr, si = r.s5_e[s:s + 1, 0:n], r.s5_e[s:s + 1, n:2 * n]
            cr, ci = sr + (pr * cr - pi_ * ci), si + (pr * ci + pi_ * cr)
        r.s5_state[0:1, 0:n] = cr
        r.s5_state[0:1, n:2 * n] = ci
    st.add(S5, "s5_carry", 100, carry)

    def fix(first):
        cin_r = r.s5_cin[:, 0:n]
        cin_i = r.s5_cin[:, n:2 * n]
        for tt in range(first // 2, (first + SCAN_PIECE) // 2):
            parts_r, parts_i = [], []
            for t in (2 * tt, 2 * tt + 1):
                qr = jnp.broadcast_to(r.ptab[t:t + 1, 0:n], (SUBLANES, n))
                qi = jnp.broadcast_to(r.ptab[t:t + 1, n:2 * n], (SUBLANES, n))
                parts_r.append(r.bu[_tile(t), 0:n] + (qr * cin_r - qi * cin_i))
                parts_i.append(r.bu[_tile(t), n:2 * n] + (qr * cin_i + qi * cin_r))
            rows2 = slice(tt * 2 * SUBLANES, (tt + 1) * 2 * SUBLANES)
            r.xbf[rows2, 0:n] = jnp.concatenate(parts_r, axis=0).astype(BF16)
            r.xbf[rows2, n:2 * n] = jnp.concatenate(parts_i, axis=0).astype(BF16)
    for first in range(0, SEG, SCAN_PIECE):
        st.add(S5, f"s5_fix{first}", 32 * SCAN_PIECE, functools.partial(fix, first))

    def post():
        y = _dot(r.xbf[...], r.cmat[...])
        y = y + _vrow(r.vec, V_S5_D) * r.uperm[...]
        g = _gelu_tanh(y)
        gate = _sigmoid(_dot(g.astype(BF16), r.glu_w[...]) + _vrow(r.vec, V_GLU_B))
        r.uperm[...] = g * gate
        for t in range(SEG):
            for j in range(NSLAB):
                r.segout[j, _seg_rows(t), :] = r.uperm[_tile(t), _slab(j)]
    st.add(S5, "s5_post", 500, post)


def _short_conv_pieces(st, r):
    def conv():
        tb = TIME_BLOCK
        h, gb, gc = r.scp[:, 0:W_GROUP], r.scp[:, W_GROUP:2 * W_GROUP], r.scp[:, 2 * W_GROUP:3 * W_GROUP]
        r.sc_buf[SUBLANES:SUBLANES + tb, :] = gc * h
        out = (_vrow(r.vec, V_SC_W0) * r.sc_buf[SUBLANES - 2:SUBLANES - 2 + tb, :]
               + _vrow(r.vec, V_SC_W1) * r.sc_buf[SUBLANES - 1:SUBLANES - 1 + tb, :]
               + _vrow(r.vec, V_SC_W2) * r.sc_buf[SUBLANES:SUBLANES + tb, :])
        r.mix[:, W_GROUP:2 * W_GROUP] = (gb * out).astype(BF16)
        r.sc_buf[0:SUBLANES, :] = r.sc_buf[tb:tb + SUBLANES, :]
    st.add(VPU, "sc", 400, conv, ["p_sc2", LAST_OUT_PROJ])


def _hgrn2_pieces(st, r):
    c = HG_CHUNK
    env = types.SimpleNamespace()

    def stack_heads(x_bf):
        return jnp.concatenate([x_bf] * HG_HEADS, axis=0) * r.headmask[...]

    def gates():
        q_in = r.hgp[:, 0:W_GROUP]
        z = r.hgp[:, W_GROUP:2 * W_GROUP]
        q = _silu(q_in) * (HG_HEAD_DIM ** -0.5)
        log_sig = jnp.minimum(z, 0.0) - jnp.log(1.0 + jnp.exp(-jnp.abs(z)))
        cpl = _vrow(r.vec, V_HG_LOG1MLB) + log_sig
        loglb = _vrow(r.vec, V_HG_LOGLB)
        g = jnp.maximum(loglb, cpl) + jnp.log(1.0 + jnp.exp(-jnp.abs(loglb - cpl)))
        k = _vrow(r.vec, V_HG_1MLB) * _sigmoid(-z)
        r.hg_qt[HG_LEVELS + 1] = q.astype(BF16)
        r.hg_kt[HG_LEVELS + 1] = k.astype(BF16)
        env.q, env.k, env.cin, env.sfx = q, k, g, jnp.zeros_like(g)
    st.add(HG, "hg_gates", 600, gates, ["p_hg1"])

    def level(lvl):
        r.hg_qt[lvl] = (env.q * jnp.exp(env.cin)).astype(BF16)
        r.hg_kt[lvl] = (env.k * jnp.exp(env.sfx)).astype(BF16)
        if lvl == HG_LEVELS:
            r.hg_cin[...] = env.cin
            return
        n = 1 << lvl
        row = lax.broadcasted_iota(jnp.int32, (TIME_BLOCK, W_GROUP), 0)
        sib = _sibling(env.cin + env.sfx, n)
        right = (row & n) != 0
        env.cin = env.cin + jnp.where(right, sib, 0.0)
        env.sfx = env.sfx + jnp.where(right, 0.0, sib)
    for lvl in range(HG_LEVELS + 1):
        st.add(HG, f"hg_level{lvl}", 350, functools.partial(level, lvl))

    nchunk = TIME_BLOCK // c

    def chunk_rows(ci):
        return slice(ci * c, (ci + 1) * c)

    def intra(ci):
        rows = chunk_rows(ci)
        lv = r.lv[...]
        v = r.hgp[rows, 2 * W_GROUP:3 * W_GROUP].astype(BF16)
        scores = None
        for lvl in (HG_LEVELS + 1,) + tuple(range(HG_LEVELS)):
            s_n = _dot_nt(r.hg_qt[lvl, rows, :], stack_heads(r.hg_kt[lvl, rows, :]))
            code = HG_LEVELS if lvl == HG_LEVELS + 1 else lvl
            scores = jnp.where(lv == code, s_n, 0.0 if scores is None else scores)
        r.hg_o[rows, :] = _dot(scores.astype(BF16), stack_heads(v))
        r.hg_upd[ci] = _dot_tn(v, r.hg_kt[HG_LEVELS, rows, :]) * r.bdmask[...]
    for ci in range(nchunk):
        st.add(HG, f"hg_intra{ci}", 400, functools.partial(intra, ci), ["p_hg3"])

    def inter(ci):
        rows = chunk_rows(ci)
        state = r.hg_state[...]
        r.hg_o[rows, :] = r.hg_o[rows, :] + _dot_nt(r.hg_qt[HG_LEVELS, rows, :], state.astype(BF16))
        decay = jnp.exp(r.hg_cin[(ci + 1) * c - 1:(ci + 1) * c, :])
        r.hg_state[...] = state * decay + r.hg_upd[ci]
    for ci in range(nchunk):
        st.add(HG, f"hg_inter{ci}", 150, functools.partial(inter, ci))

    def norm():
        o = r.hg_o[...]
        ms = _dot((o * o).astype(BF16), r.ones[...])
        o = o * lax.rsqrt(ms + EPS) * _vrow(r.vec, V_HG_GNORM)
        g_in = r.hgp[:, 3 * W_GROUP:4 * W_GROUP]
        r.mix[:, 2 * W_GROUP:3 * W_GROUP] = (o * _silu(g_in)).astype(BF16)
    st.add(HG, "hg_norm", 300, norm, [LAST_OUT_PROJ])


def _rglru_pieces(st, r):
    tb = TIME_BLOCK
    env = types.SimpleNamespace()

    def seg_tile(ref, first, t):
        return jnp.concatenate([ref[first + j, _seg_rows(t), :] for j in range(NSLAB)], axis=1)

    def pre():
        xc = _vrow(r.vec, V_LRU_CB) + (
            _vrow(r.vec, V_LRU_W0) * r.lru_xbuf[SUBLANES - 3:SUBLANES - 3 + tb, :]
            + _vrow(r.vec, V_LRU_W1) * r.lru_xbuf[SUBLANES - 2:SUBLANES - 2 + tb, :]
            + _vrow(r.vec, V_LRU_W2) * r.lru_xbuf[SUBLANES - 1:SUBLANES - 1 + tb, :]
            + _vrow(r.vec, V_LRU_W3) * r.lru_xbuf[SUBLANES:SUBLANES + tb, :])
        r.lru_xbuf[0:SUBLANES, :] = r.lru_xbuf[tb:tb + SUBLANES, :]
        gates = _dot(xc.astype(BF16), r.wab[...])
        gate_a = _sigmoid(gates[:, 0:W_GROUP] + _vrow(r.vec, V_LRU_BA))
        gate_x = _sigmoid(gates[:, W_GROUP:2 * W_GROUP] + _vrow(r.vec, V_LRU_BX))
        log_a = _vrow(r.vec, V_LRU_CA) * gate_a
        a = jnp.exp(log_a)
        b = xc * gate_x * jnp.sqrt(-jnp.tanh(log_a) * (a * a + 1.0))
        for j in range(NSLAB):
            r.lru_a[j] = a[:, _slab(j)]
            r.lru_b[j] = b[:, _slab(j)]
    st.add(LRU, "lru_pre", 500, pre, ["p_lru_x"])

    def scan(first):
        for t in range(first, first + SCAN_PIECE):
            a_t, b_t = seg_tile(r.lru_a, 0, t), seg_tile(r.lru_b, 0, t)
            env.h, env.p = (b_t, a_t) if t == 0 else (a_t * env.h + b_t, a_t * env.p)
            r.lru_h[_tile(t), :] = env.h
            r.lru_p[_tile(t), :] = env.p
    for first in range(0, SEG, SCAN_PIECE):
        st.add(LRU, f"lru_scan{first}", 18 * SCAN_PIECE, functools.partial(scan, first))

    def carry():
        r.lru_e[0:SUBLANES, :] = env.h
        r.lru_e[SUBLANES:2 * SUBLANES, :] = env.p
        cur = r.lru_state[0:1, :]
        for s in range(SUBLANES):
            r.lru_cin[s:s + 1, :] = cur
            cur = r.lru_e[s:s + 1, :] + r.lru_e[SUBLANES + s:SUBLANES + s + 1, :] * cur
        r.lru_state[0:1, :] = cur
    st.add(LRU, "lru_carry", 50, carry)

    def fix(first):
        cin = r.lru_cin[...]
        for t in range(first, first + SCAN_PIECE):
            res = (r.lru_h[_tile(t), :] + r.lru_p[_tile(t), :] * cin) * _gelu_tanh(seg_tile(r.segin, NSLAB, t))
            for j in range(NSLAB):
                r.segout[NSLAB + j, _seg_rows(t), :] = res[:, _slab(j)]
    for first in range(0, SEG, SCAN_PIECE):
        st.add(LRU, f"lru_fix{first}", 38 * SCAN_PIECE, functools.partial(fix, first), ["p_lru_y"])


_SCRATCH = (
    ("hprev", (TIME_BLOCK, D_MODEL), F32),
    ("mix", (TIME_BLOCK, 4 * W_GROUP), BF16),
    ("res", (TIME_BLOCK, D_MODEL), F32),
    ("ybuf", (TIME_BLOCK, D_MODEL), F32),
    ("xcarry", (N_META, D_MODEL), F32),
    ("hn", (TIME_BLOCK, D_MODEL), F32),
    ("hnb", (TIME_BLOCK, D_MODEL), BF16),
    ("act", (TIME_BLOCK, D_FF), BF16),
    ("hb", (TIME_BLOCK, D_MODEL), BF16),
    ("hgp", (TIME_BLOCK, 4 * W_GROUP), F32),
    ("scp", (TIME_BLOCK, 3 * W_GROUP), F32),
    ("segin", (2 * NSLAB, TIME_BLOCK, LANES), F32),
    ("segout", (2 * NSLAB, TIME_BLOCK, LANES), F32),
    ("uperm", (TIME_BLOCK, W_GROUP), F32),
    ("bu", (TIME_BLOCK, 2 * S5_W), F32),
    ("xbf", (TIME_BLOCK, 2 * S5_W), BF16),
    ("s5_e", (SUBLANES, 2 * S5_W), F32),
    ("s5_cin", (SUBLANES, 2 * S5_W), F32),
    ("s5_state", (SUBLANES, 2 * S5_W), F32),
    ("sc_buf", (TIME_BLOCK + SUBLANES, W_GROUP), F32),
    ("hg_qt", (HG_LEVELS + 2, TIME_BLOCK, W_GROUP), BF16),
    ("hg_kt", (HG_LEVELS + 2, TIME_BLOCK, W_GROUP), BF16),
    ("hg_cin", (TIME_BLOCK, W_GROUP), F32),
    ("hg_state", (W_GROUP, W_GROUP), F32),
    ("hg_upd", (TIME_BLOCK // HG_CHUNK, W_GROUP, W_GROUP), F32),
    ("hg_o", (TIME_BLOCK, W_GROUP), F32),
    ("lru_xbuf", (TIME_BLOCK + SUBLANES, W_GROUP), F32),
    ("lru_a", (NSLAB, TIME_BLOCK, LANES), F32),
    ("lru_b", (NSLAB, TIME_BLOCK, LANES), F32),
    ("lru_h", (TIME_BLOCK, W_GROUP), F32),
    ("lru_p", (TIME_BLOCK, W_GROUP), F32),
    ("lru_e", (2 * SUBLANES, W_GROUP), F32),
    ("lru_cin", (SUBLANES, W_GROUP), F32),
    ("lru_state", (SUBLANES, W_GROUP), F32),
)
_CONSTS = ("w_in", "vec", "bmat", "cmat", "lam", "ptab", "glu_w", "wab", "headmask", "bdmask", "ones",
           "lv", "w_out", "ln", "w_ffn_in", "w_ffn_out")


def _layer_kernel(nblk, nsteps, first, last, n_tok, *refs):
    names = (("x", "meta") if first else ("h",)) + _CONSTS + ("out",) + tuple(s[0] for s in _SCRATCH)
    r = types.SimpleNamespace(**dict(zip(names, refs, strict=True)))
    step = pl.program_id(0)
    block = jnp.minimum(step, nsteps - 1) % nblk

    @pl.when(step == 0)
    def _():
        r.hprev[...] = jnp.zeros_like(r.hprev)
        r.mix[...] = jnp.zeros_like(r.mix)
        r.ybuf[...] = jnp.zeros_like(r.ybuf)

    if first:
        r.read_h = functools.partial(_first_layer_block, r, block, n_tok)

        @pl.when(block == 0)
        def _():
            r.xcarry[...] = r.meta[...]
    else:
        r.read_h = lambda: r.h[...]

    @pl.when(step % nblk == 0)
    def _():
        r.s5_state[...] = jnp.zeros_like(r.s5_state)
        r.sc_buf[0:SUBLANES, :] = jnp.zeros((SUBLANES, W_GROUP), F32)
        r.hg_state[...] = jnp.zeros_like(r.hg_state)
        r.lru_xbuf[0:SUBLANES, :] = jnp.zeros((SUBLANES, W_GROUP), F32)
        r.lru_state[...] = jnp.zeros_like(r.lru_state)

    def finish():
        for j in range(NSLAB):
            r.mix[:, _slab(j)] = r.segout[j].astype(BF16)
            r.mix[:, _slab(j, 3 * W_GROUP)] = r.segout[NSLAB + j].astype(BF16)
        r.hprev[...] = r.read_h()
        if first:
            r.xcarry[...] = r.x[TIME_BLOCK - N_META:TIME_BLOCK, :]

    def body(mixers, channel):
        st = _Streams()
        if mixers:
            proj = _in_proj_pieces(st, r)
        _ln2_pieces(st, r, N_META if last else 0, with_tail=channel)
        if channel:
            _out_proj_pieces(st, r)
        if mixers:
            _in_proj_rest(st, r, proj)
        if channel:
            _ffn_pieces(st, r)
        if mixers:
            _s5_head(st, r)
        if channel:
            _ln1_piece(st, r)
        if mixers:
            _s5_pieces(st, r)
            _hgrn2_pieces(st, r)
            _rglru_pieces(st, r)
            _short_conv_pieces(st, r)
            st.add(VPU, "finish", 200, finish,
                   [LAST_OUT_PROJ, "s5_post", f"lru_fix{SEG - SCAN_PIECE}"])
        st.emit()

    pl.when(step < nsteps)(functools.partial(body, True, True))
    pl.when(step == nsteps)(functools.partial(body, False, True))
    pl.when(step == nsteps + 1)(functools.partial(body, False, False))


def _const_spec(shape):
    return pl.BlockSpec(shape, lambda *_: (0,) * len(shape), pipeline_mode=pl.Buffered(1))


def _layer_spec(shape, layer):
    return pl.BlockSpec((None,) + shape[1:], lambda *_: (layer, 0, 0), pipeline_mode=pl.Buffered(1))


def _layer_call(inputs, consts, bsz, seq, layer, first, last):
    tb = TIME_BLOCK
    n_tok = N_META + seq
    nblk = pl.cdiv(n_tok, tb)
    nsteps = bsz * nblk

    def in_block(n):
        return jnp.minimum(n, nsteps - 1)

    def out_block(n):
        return jnp.maximum(n - PIPELINE_DEPTH, 0)
    if first:
        in_specs = [pl.BlockSpec((None, tb, D_MODEL), lambda n: (in_block(n) // nblk, in_block(n) % nblk, 0)),
                    _const_spec((N_META, D_MODEL))]
    else:
        in_specs = [pl.BlockSpec((tb, D_MODEL), lambda n: (in_block(n), 0))]
    if last:
        out_spec = pl.BlockSpec((None, tb, D_MODEL), lambda n: (out_block(n) // nblk, out_block(n) % nblk, 0))
        out_shape = jax.ShapeDtypeStruct((bsz, seq, D_MODEL), F32)
    else:
        out_spec = pl.BlockSpec((tb, D_MODEL), lambda n: (out_block(n), 0))
        out_shape = jax.ShapeDtypeStruct((nsteps * tb, D_MODEL), F32)
    return pl.pallas_call(
        functools.partial(_layer_kernel, nblk, nsteps, first, last, n_tok),
        grid=(nsteps + PIPELINE_DEPTH,),
        in_specs=in_specs + [_layer_spec(c.shape, layer) if c.ndim == 3 else _const_spec(c.shape) for c in consts],
        out_specs=out_spec,
        out_shape=out_shape,
        scratch_shapes=[pltpu.VMEM(shape, dtype) for _, shape, dtype in _SCRATCH],
        compiler_params=pltpu.CompilerParams(
            dimension_semantics=("arbitrary",), vmem_limit_bytes=VMEM_LIMIT_BYTES),
    )(*inputs, *consts)


def _block_diag(blocks):
    layers, n, a, b = blocks.shape
    eye = jnp.eye(n, dtype=blocks.dtype)
    return jnp.einsum('lnab,nm->lnamb', blocks, eye).reshape(layers, n * a, n * b)


def _hgrn2_level_map():
    t = np.arange(HG_CHUNK, dtype=np.int32)[:, None]
    s = np.arange(HG_CHUNK, dtype=np.int32)[None, :]
    x = t ^ s
    lv = np.full((HG_CHUNK, HG_CHUNK), -1, np.int32)
    for level in range(HG_LEVELS):
        n = 1 << level
        lv = np.where((x >= n) & (x < 2 * n) & (t > s), level, lv)
    lv = np.where(t == s, HG_LEVELS, lv)
    return jnp.asarray(np.tile(lv, (1, HG_HEADS)).astype(np.int32))


def _prepare_layers(lb, s5_lam_re, s5_lam_im, s5_b_re, s5_b_im, s5_c_re, s5_c_im, s5_d,
                    s5_log_dt, s5_glu_w, s5_glu_b, sc_conv_w, hg_gnorm, lru_conv_w, lru_conv_b,
                    lru_wa, lru_ba, lru_wx, lru_bx, lru_a_param):
    layers = lb.shape[0]
    lam_r, lam_i = s5_lam_re.astype(F32), s5_lam_im.astype(F32)
    dt = jnp.exp(s5_log_dt.astype(F32))[..., None]
    arg_r, arg_i = lam_r * dt, lam_i * dt

    def lam_bar_pow(ar, ai, k):
        mag = jnp.exp(ar * k)
        return mag * jnp.cos(ai * k), mag * jnp.sin(ai * k)
    bar_r, bar_i = lam_bar_pow(arg_r, arg_i, 1.0)
    num_r, num_i = bar_r - 1.0, bar_i
    den = lam_r * lam_r + lam_i * lam_i
    coef_r = ((num_r * lam_r + num_i * lam_i) / den)[..., None]
    coef_i = ((num_i * lam_r - num_r * lam_i) / den)[..., None]
    b_r, b_i = s5_b_re.astype(F32), s5_b_im.astype(F32)
    bbar_r = jnp.swapaxes(coef_r * b_r - coef_i * b_i, -1, -2)
    bbar_i = jnp.swapaxes(coef_r * b_i + coef_i * b_r, -1, -2)
    bmat = jnp.concatenate([_block_diag(bbar_r), _block_diag(bbar_i)], axis=-1)
    c_r = jnp.swapaxes(s5_c_re.astype(F32), -1, -2)
    c_i = jnp.swapaxes(s5_c_im.astype(F32), -1, -2)
    cmat = jnp.concatenate([_block_diag(c_r), -_block_diag(c_i)], axis=-2)

    def flat(zr, zi):
        return jnp.concatenate([zr.reshape(layers, -1), zi.reshape(layers, -1)], axis=-1)
    lam_rows = jnp.stack([flat(bar_r, bar_i), flat(*lam_bar_pow(arg_r, arg_i, float(SEG)))], axis=1)
    steps = jnp.arange(1, SEG + 1, dtype=F32)[None, :, None, None]
    pw_r, pw_i = lam_bar_pow(arg_r[:, None], arg_i[:, None], steps)
    ptab = jnp.concatenate([pw_r.reshape(layers, SEG, -1), pw_i.reshape(layers, SEG, -1)], axis=-1)

    rows = [None] * N_VEC
    rows[V_S5_D] = s5_d
    rows[V_GLU_B] = s5_glu_b
    rows[V_SC_W0], rows[V_SC_W1], rows[V_SC_W2] = sc_conv_w[:, 0], sc_conv_w[:, 1], sc_conv_w[:, 2]
    rows[V_HG_LOGLB] = jnp.maximum(jnp.log(lb), -1e30)
    rows[V_HG_LOG1MLB] = jnp.log1p(-lb)
    rows[V_HG_1MLB] = 1.0 - lb
    rows[V_HG_GNORM] = hg_gnorm
    for i, v in enumerate((V_LRU_W0, V_LRU_W1, V_LRU_W2, V_LRU_W3)):
        rows[v] = lru_conv_w[:, i]
    rows[V_LRU_CB] = lru_conv_b
    rows[V_LRU_CA] = -LRU_C * jax.nn.softplus(-lru_a_param.astype(F32))
    rows[V_LRU_BA] = lru_ba
    rows[V_LRU_BX] = lru_bx
    zero = jnp.zeros((layers, W_GROUP), F32)
    vec = jnp.stack([zero if r is None else r.astype(F32) for r in rows], axis=1)
    wab = jnp.concatenate([_block_diag(lru_wa.astype(F32)), _block_diag(lru_wx.astype(F32))], axis=-1)
    return (vec, bmat.astype(BF16), cmat.astype(BF16), lam_rows, ptab, s5_glu_w.astype(BF16), wab.astype(BF16))


def kernel(x, meta_tokens, hg_lb_raw, w_in, w_out, s5_lam_re, s5_lam_im, s5_b_re, s5_b_im, s5_c_re, s5_c_im, s5_d, s5_log_dt, s5_glu_w, s5_glu_b, sc_conv_w, hg_gnorm, lru_conv_w, lru_conv_b, lru_wa, lru_ba, lru_wx, lru_bx, lru_a_param, ln1_g, ln1_b, w_ffn_in, w_ffn_out, ln2_g, ln2_b):
    bsz, seq, d = x.shape
    assert d == D_MODEL
    inputs = (x.astype(F32), meta_tokens.astype(F32))

    lb_all = jnp.cumsum(jax.nn.softmax(hg_lb_raw.astype(F32), axis=0), axis=0)
    lb_all = lb_all - lb_all[0:1]

    head = np.arange(W_GROUP) // HG_HEAD_DIM
    same_head = (head[:, None] == head[None, :]).astype(np.float32)
    bdmask = jnp.asarray(same_head)
    headmask = jnp.asarray(same_head, dtype=BF16)
    ones = jnp.asarray(same_head / HG_HEAD_DIM, dtype=BF16)
    lv = _hgrn2_level_map()

    w_in_bf, w_out_bf = w_in.astype(BF16), w_out.astype(BF16)
    w_ffn_in_bf, w_ffn_out_bf = w_ffn_in.astype(BF16), w_ffn_out.astype(BF16)
    prep = _prepare_layers(lb_all, s5_lam_re, s5_lam_im, s5_b_re, s5_b_im, s5_c_re, s5_c_im,
                           s5_d, s5_log_dt, s5_glu_w, s5_glu_b, sc_conv_w, hg_gnorm, lru_conv_w,
                           lru_conv_b, lru_wa, lru_ba, lru_wx, lru_bx, lru_a_param)
    ln = jnp.stack([ln1_g, ln1_b, ln2_g, ln2_b], axis=1).astype(F32)
    consts = (w_in_bf,) + prep + (headmask, bdmask, ones, lv, w_out_bf, ln, w_ffn_in_bf, w_ffn_out_bf)
    for l in range(DEPTH):
        inputs = (_layer_call(inputs, consts, bsz, seq, l, first=l == 0, last=l == DEPTH - 1),)
    return inputs[0]
```

```python
import functools
import math
import types

import jax
import jax.numpy as jnp
import numpy as np
from jax import lax
from jax.experimental import pallas as pl
from jax.experimental.pallas import tpu as pltpu

F32 = jnp.float32
BF16 = jnp.bfloat16

D_MODEL = 1024
DEPTH = 2
N_META = 16
W_GROUP = 256
N_IN = 10 * W_GROUP
S5_GROUP = 16
S5_NGROUPS = 16
S5_STATE = 64
S5_W = S5_NGROUPS * S5_STATE
HG_HEADS = 4
HG_HEAD_DIM = 64
LRU_HEADS = 4
LRU_C = 8.0
D_FF = 2816
ALPHA = (2 * DEPTH) ** 0.25
EPS = 1e-5

SUBLANES = 8
LANES = 128
MXU_N = 256
NSLAB = W_GROUP // LANES
TIME_BLOCK = 320
SEG = TIME_BLOCK // SUBLANES
SCAN_PIECE = 8
PIPELINE_DEPTH = 2
SUBSTEPS = 2
HG_CHUNK = 64
HG_LEVELS = 6
VMEM_LIMIT_BYTES = 60 * 1024 * 1024

(V_S5_D, V_GLU_B, V_SC_W0, V_SC_W1, V_SC_W2, V_HG_LOGLB, V_HG_LOG1MLB, V_HG_1MLB, V_HG_GNORM,
 V_LRU_W0, V_LRU_W1, V_LRU_W2, V_LRU_W3, V_LRU_CB, V_LRU_CA, V_LRU_BA, V_LRU_BX) = range(17)
N_VEC = 24


def _sigmoid(x):
    return 0.5 + 0.5 * jnp.tanh(0.5 * x)


def _silu(x):
    h = 0.5 * x
    return h + h * jnp.tanh(h)


def _gelu_tanh(x):
    c = math.sqrt(2.0 / math.pi)
    return x * (0.5 * (1.0 + jnp.tanh(c * (x + 0.044715 * (x * x * x)))))


def _dot(a, b):
    return jnp.dot(a, b, preferred_element_type=F32)


def _dot_nt(a, b):
    return lax.dot_general(a, b, (((1,), (1,)), ((), ())), preferred_element_type=F32)


def _dot_tn(a, b):
    return lax.dot_general(a, b, (((0,), (0,)), ((), ())), preferred_element_type=F32)


def _tile(r):
    return slice(r * SUBLANES, (r + 1) * SUBLANES)


def _slab(j, first=0):
    return slice(first + j * LANES, first + (j + 1) * LANES)


def _cols(j, first=0):
    return slice(first + j * MXU_N, first + (j + 1) * MXU_N)


def _seg_rows(r):
    return pl.ds(r, SUBLANES, stride=SEG)


def _vrow(vec_ref, i):
    return vec_ref[i:i + 1, :]


def _layer_norm(x, g, b):
    mu = jnp.mean(x, axis=-1, keepdims=True)
    xc = x - mu
    var = jnp.mean(xc * xc, axis=-1, keepdims=True)
    return xc * lax.rsqrt(var + EPS) * g + b


def _sibling(x, n):
    c, w = x.shape
    if n < SUBLANES:
        x3 = x.reshape(c // SUBLANES, SUBLANES, w)
        fwd = pltpu.roll(x3, n, 1)
        if 2 * n == SUBLANES:
            return fwd.reshape(c, w)
        bwd = pltpu.roll(x3, SUBLANES - n, 1)
        row = lax.broadcasted_iota(jnp.int32, x3.shape, 1)
        return jnp.where((row & n) != 0, fwd, bwd).reshape(c, w)
    x4 = x.reshape(c // (2 * n), 2, n, w)
    return jnp.concatenate([x4[:, 1:2], x4[:, 0:1]], axis=1).reshape(c, w)


class _Streams:
    def __init__(self):
        self.lanes = {}

    def add(self, lane, name, cost, fn, deps=()):
        self.lanes.setdefault(lane, []).append((name, cost, tuple(deps), fn))

    def emit(self):
        total = {k: sum(p[1] for p in v) for k, v in self.lanes.items()}
        pos = dict.fromkeys(self.lanes, 0)
        spent = dict.fromkeys(self.lanes, 0)
        done = set()

        def ready(k):
            return pos[k] < len(self.lanes[k]) and all(d in done for d in self.lanes[k][pos[k]][2])
        while any(pos[k] < len(v) for k, v in self.lanes.items()):
            order = sorted(self.lanes, key=lambda k: spent[k] / total[k] * LANE_LAG.get(k, 1.0))
            pick = next((k for k in order if ready(k)), None)
            assert pick is not None, "piece dependencies cannot be met"
            name, cost, _, fn = self.lanes[pick][pos[pick]]
            fn()
            done.add(name)
            pos[pick] += 1
            spent[pick] += cost


MXU, PROJ, VPU, S5, HG, LRU, TAIL = "mxu", "proj", "misc", "s5", "hg", "lru", "tail"
LANE_LAG = {MXU: 0.7, PROJ: 0.3}


def _matmul_cost(k, n):
    rows_per_push, cycles_per_push, units = 16, 8, 2
    return TIME_BLOCK // rows_per_push * cycles_per_push * (k // MXU_N) * (n // MXU_N) // units


OUT_PIECE = MXU_N
N_OUT_PIECES = D_MODEL // OUT_PIECE
LAST_OUT_PROJ = f"out_proj{N_OUT_PIECES - 1}"
LAST_FFN_OUT = f"ffn_out{N_OUT_PIECES - 1}"


def _out_cols(j):
    return slice(j * OUT_PIECE, (j + 1) * OUT_PIECE)


def _out_proj_pieces(st, r):
    for j in range(N_OUT_PIECES):
        def out_proj(j=j):
            r.res[:, _out_cols(j)] = (ALPHA * r.hprev[:, _out_cols(j)]
                                      + _dot(r.mix[...], r.w_out[:, _out_cols(j)]))
        st.add(MXU, f"out_proj{j}", _matmul_cost(4 * W_GROUP, OUT_PIECE), out_proj)


def _ln1_piece(st, r):
    def ln1():
        hn = _layer_norm(r.res[...], r.ln[0:1, :], r.ln[1:2, :])
        r.hn[...] = hn
        r.hnb[...] = hn.astype(BF16)
    st.add(VPU, "ln1", 800, ln1, [LAST_OUT_PROJ])


def _ffn_pieces(st, r):
    nff = D_FF // MXU_N
    for j in range(nff):
        def ffn_in(j=j):
            hb = r.hnb[...]
            gate = _dot(hb, r.w_ffn_in[:, _cols(j)])
            up = _dot(hb, r.w_ffn_in[:, _cols(j, D_FF)])
            r.act[:, _cols(j)] = (_silu(gate) * up).astype(BF16)
        st.add(MXU, f"ffn_in{j}", 2 * _matmul_cost(D_MODEL, MXU_N), ffn_in, ["ln1"])

    for j in range(N_OUT_PIECES):
        def ffn_out(j=j):
            r.ybuf[:, _out_cols(j)] = (ALPHA * r.hn[:, _out_cols(j)]
                                       + _dot(r.act[...], r.w_ffn_out[:, _out_cols(j)]))
        st.add(MXU, f"ffn_out{j}", _matmul_cost(D_FF, OUT_PIECE), ffn_out, [f"ffn_in{nff - 1}", "ln2"])


def _ln2_pieces(st, r, shift_rows, with_tail):
    def norm(y):
        return _layer_norm(y, r.ln[2:3, :], r.ln[3:4, :])

    def ln2():
        if shift_rows:
            r.out[0:TIME_BLOCK - shift_rows, :] = norm(r.ybuf[shift_rows:TIME_BLOCK, :])
        else:
            r.out[...] = norm(r.ybuf[...])
    st.add(VPU, "ln2", 800, ln2)

    def ln2_tail():
        r.out[TIME_BLOCK - shift_rows:TIME_BLOCK, :] = norm(r.ybuf[0:shift_rows, :])
    if shift_rows and with_tail:
        st.add(TAIL, "ln2_tail", 50, ln2_tail, [LAST_FFN_OUT])


def _first_layer_block(r, block, n_tok):
    keep = TIME_BLOCK - N_META
    row = lax.broadcasted_iota(jnp.int32, (keep, D_MODEL), 0) + (block * TIME_BLOCK + N_META)
    return jnp.concatenate([r.xcarry[...], jnp.where(row < n_tok, r.x[0:keep, :], 0.0)], axis=0)


def _in_proj_pieces(st, r):
    def cast():
        r.hb[...] = r.read_h().astype(BF16)
    st.add(VPU, "cast", 100, cast)

    def proj(j):
        return _dot(r.hb[...], r.w_in[:, _cols(j)])

    def p_s5():
        p = proj(0)
        for j in range(NSLAB):
            r.segin[j] = p[:, _slab(j)]
    st.add(PROJ, "p_s5", _matmul_cost(D_MODEL, MXU_N), p_s5, ["cast"])
    return proj


def _in_proj_rest(st, r, proj):
    for j in range(4):
        def p_hg(j=j):
            r.hgp[:, _cols(j)] = proj(4 + j)
        st.add(PROJ, f"p_hg{j}", _matmul_cost(D_MODEL, MXU_N), p_hg)

    def p_lru_x():
        r.lru_xbuf[SUBLANES:SUBLANES + TIME_BLOCK, :] = proj(8)
    st.add(PROJ, "p_lru_x", _matmul_cost(D_MODEL, MXU_N), p_lru_x)

    def p_lru_y():
        p = proj(9)
        for j in range(NSLAB):
            r.segin[NSLAB + j] = p[:, _slab(j)]
    st.add(PROJ, "p_lru_y", _matmul_cost(D_MODEL, MXU_N), p_lru_y)
    for j in range(3):
        def p_sc(j=j):
            r.scp[:, _cols(j)] = proj(1 + j)
        st.add(PROJ, f"p_sc{j}", _matmul_cost(D_MODEL, MXU_N), p_sc)


def _s5_head(st, r):
    def gather():
        for t in range(SEG):
            for j in range(NSLAB):
                r.uperm[_tile(t), _slab(j)] = r.segin[j, _seg_rows(t), :]
        r.bu[...] = _dot(r.uperm[...].astype(BF16), r.bmat[...])
    st.add(S5, "s5_gather", 300, gather, ["p_s5"])


def _s5_pieces(st, r):
    n = S5_W
    env = types.SimpleNamespace()

    def scan(first):
        lam_r = jnp.broadcast_to(r.lam[0:1, 0:n], (SUBLANES, n))
        lam_i = jnp.broadcast_to(r.lam[0:1, n:2 * n], (SUBLANES, n))
        for t in range(first, first + SCAN_PIECE):
            br, bi = r.bu[_tile(t), 0:n], r.bu[_tile(t), n:2 * n]
            if t == 0:
                env.xr, env.xi = br, bi
            else:
                env.xr, env.xi = (lam_r * env.xr - lam_i * env.xi + br,
                                  lam_r * env.xi + lam_i * env.xr + bi)
                r.bu[_tile(t), 0:n] = env.xr
                r.bu[_tile(t), n:2 * n] = env.xi
    for first in range(0, SEG, SCAN_PIECE):
        st.add(S5, f"s5_scan{first}", 22 * SCAN_PIECE, functools.partial(scan, first))

    def carry():
        r.s5_e[:, 0:n] = env.xr
        r.s5_e[:, n:2 * n] = env.xi
        pr, pi_ = r.lam[1:2, 0:n], r.lam[1:2, n:2 * n]
        cr, ci = r.s5_state[0:1, 0:n], r.s5_state[0:1, n:2 * n]
        for s in range(SUBLANES):
            r.s5_cin[s:s + 1, 0:n] = cr
            r.s5_cin[s:s + 1, n:2 * n] = ci
            sr, si = r.s5_e[s:s + 1, 0:n], r.s5_e[s:s + 1, n:2 * n]
            cr, ci = sr + (pr * cr - pi_ * ci), si + (pr * ci + pi_ * cr)
        r.s5_state[0:1, 0:n] = cr
        r.s5_state[0:1, n:2 * n] = ci
    st.add(S5, "s5_carry", 100, carry)

    def fix(first):
        cin_r = r.s5_cin[:, 0:n]
        cin_i = r.s5_cin[:, n:2 * n]
        for tt in range(first // 2, (first + SCAN_PIECE) // 2):
            parts_r, parts_i = [], []
            for t in (2 * tt, 2 * tt + 1):
                qr = jnp.broadcast_to(r.ptab[t:t + 1, 0:n], (SUBLANES, n))
                qi = jnp.broadcast_to(r.ptab[t:t + 1, n:2 * n], (SUBLANES, n))
                parts_r.append(r.bu[_tile(t), 0:n] + (qr * cin_r - qi * cin_i))
                parts_i.append(r.bu[_tile(t), n:2 * n] + (qr * cin_i + qi * cin_r))
            rows2 = slice(tt * 2 * SUBLANES, (tt + 1) * 2 * SUBLANES)
            r.xbf[rows2, 0:n] = jnp.concatenate(parts_r, axis=0).astype(BF16)
            r.xbf[rows2, n:2 * n] = jnp.concatenate(parts_i, axis=0).astype(BF16)
    for first in range(0, SEG, SCAN_PIECE):
        st.add(S5, f"s5_fix{first}", 32 * SCAN_PIECE, functools.partial(fix, first))

    def post():
        y = _dot(r.xbf[...], r.cmat[...])
        y = y + _vrow(r.vec, V_S5_D) * r.uperm[...]
        g = _gelu_tanh(y)
        gate = _sigmoid(_dot(g.astype(BF16), r.glu_w[...]) + _vrow(r.vec, V_GLU_B))
        r.uperm[...] = g * gate
        for t in range(SEG):
            for j in range(NSLAB):
                r.segout[j, _seg_rows(t), :] = r.uperm[_tile(t), _slab(j)]
    st.add(S5, "s5_post", 500, post)


def _short_conv_pieces(st, r):
    def conv():
        tb = TIME_BLOCK
        h, gb, gc = r.scp[:, 0:W_GROUP], r.scp[:, W_GROUP:2 * W_GROUP], r.scp[:, 2 * W_GROUP:3 * W_GROUP]
        r.sc_buf[SUBLANES:SUBLANES + tb, :] = gc * h
        out = (_vrow(r.vec, V_SC_W0) * r.sc_buf[SUBLANES - 2:SUBLANES - 2 + tb, :]
               + _vrow(r.vec, V_SC_W1) * r.sc_buf[SUBLANES - 1:SUBLANES - 1 + tb, :]
               + _vrow(r.vec, V_SC_W2) * r.sc_buf[SUBLANES:SUBLANES + tb, :])
        r.mix[:, W_GROUP:2 * W_GROUP] = (gb * out).astype(BF16)
        r.sc_buf[0:SUBLANES, :] = r.sc_buf[tb:tb + SUBLANES, :]
    st.add(VPU, "sc", 400, conv, ["p_sc2", LAST_OUT_PROJ])


def _hgrn2_pieces(st, r):
    c = HG_CHUNK
    env = types.SimpleNamespace()

    def stack_heads(x_bf):
        return jnp.concatenate([x_bf] * HG_HEADS, axis=0) * r.headmask[...]

    def gates():
        q_in = r.hgp[:, 0:W_GROUP]
        z = r.hgp[:, W_GROUP:2 * W_GROUP]
        q = _silu(q_in) * (HG_HEAD_DIM ** -0.5)
        log_sig = jnp.minimum(z, 0.0) - jnp.log(1.0 + jnp.exp(-jnp.abs(z)))
        cpl = _vrow(r.vec, V_HG_LOG1MLB) + log_sig
        loglb = _vrow(r.vec, V_HG_LOGLB)
        g = jnp.maximum(loglb, cpl) + jnp.log(1.0 + jnp.exp(-jnp.abs(loglb - cpl)))
        k = _vrow(r.vec, V_HG_1MLB) * _sigmoid(-z)
        r.hg_qt[HG_LEVELS + 1] = q.astype(BF16)
        r.hg_kt[HG_LEVELS + 1] = k.astype(BF16)
        env.q, env.k, env.cin, env.sfx = q, k, g, jnp.zeros_like(g)
    st.add(HG, "hg_gates", 600, gates, ["p_hg1"])

    def level(lvl):
        r.hg_qt[lvl] = (env.q * jnp.exp(env.cin)).astype(BF16)
        r.hg_kt[lvl] = (env.k * jnp.exp(env.sfx)).astype(BF16)
        if lvl == HG_LEVELS:
            r.hg_cin[...] = env.cin
            return
        n = 1 << lvl
        row = lax.broadcasted_iota(jnp.int32, (TIME_BLOCK, W_GROUP), 0)
        sib = _sibling(env.cin + env.sfx, n)
        right = (row & n) != 0
        env.cin = env.cin + jnp.where(right, sib, 0.0)
        env.sfx = env.sfx + jnp.where(right, 0.0, sib)
    for lvl in range(HG_LEVELS + 1):
        st.add(HG, f"hg_level{lvl}", 350, functools.partial(level, lvl))

    nchunk = TIME_BLOCK // c

    def chunk_rows(ci):
        return slice(ci * c, (ci + 1) * c)

    def intra(ci):
        rows = chunk_rows(ci)
        lv = r.lv[...]
        v = r.hgp[rows, 2 * W_GROUP:3 * W_GROUP].astype(BF16)
        scores = None
        for lvl in (HG_LEVELS + 1,) + tuple(range(HG_LEVELS)):
            s_n = _dot_nt(r.hg_qt[lvl, rows, :], stack_heads(r.hg_kt[lvl, rows, :]))
            code = HG_LEVELS if lvl == HG_LEVELS + 1 else lvl
            scores = jnp.where(lv == code, s_n, 0.0 if scores is None else scores)
        r.hg_o[rows, :] = _dot(scores.astype(BF16), stack_heads(v))
        r.hg_upd[ci] = _dot_tn(v, r.hg_kt[HG_LEVELS, rows, :]) * r.bdmask[...]
    for ci in range(nchunk):
        st.add(HG, f"hg_intra{ci}", 400, functools.partial(intra, ci), ["p_hg3"])

    def inter(ci):
        rows = chunk_rows(ci)
        state = r.hg_state[...]
        r.hg_o[rows, :] = r.hg_o[rows, :] + _dot_nt(r.hg_qt[HG_LEVELS, rows, :], state.astype(BF16))
        decay = jnp.exp(r.hg_cin[(ci + 1) * c - 1:(ci + 1) * c, :])
        r.hg_state[...] = state * decay + r.hg_upd[ci]
    for ci in range(nchunk):
        st.add(HG, f"hg_inter{ci}", 150, functools.partial(inter, ci))

    def norm():
        o = r.hg_o[...]
        ms = _dot((o * o).astype(BF16), r.ones[...])
        o = o * lax.rsqrt(ms + EPS) * _vrow(r.vec, V_HG_GNORM)
        g_in = r.hgp[:, 3 * W_GROUP:4 * W_GROUP]
        r.mix[:, 2 * W_GROUP:3 * W_GROUP] = (o * _silu(g_in)).astype(BF16)
    st.add(HG, "hg_norm", 300, norm, [LAST_OUT_PROJ])


def _rglru_pieces(st, r):
    tb = TIME_BLOCK
    env = types.SimpleNamespace()

    def seg_tile(ref, first, t):
        return jnp.concatenate([ref[first + j, _seg_rows(t), :] for j in range(NSLAB)], axis=1)

    def pre():
        xc = _vrow(r.vec, V_LRU_CB) + (
            _vrow(r.vec, V_LRU_W0) * r.lru_xbuf[SUBLANES - 3:SUBLANES - 3 + tb, :]
            + _vrow(r.vec, V_LRU_W1) * r.lru_xbuf[SUBLANES - 2:SUBLANES - 2 + tb, :]
            + _vrow(r.vec, V_LRU_W2) * r.lru_xbuf[SUBLANES - 1:SUBLANES - 1 + tb, :]
            + _vrow(r.vec, V_LRU_W3) * r.lru_xbuf[SUBLANES:SUBLANES + tb, :])
        r.lru_xbuf[0:SUBLANES, :] = r.lru_xbuf[tb:tb + SUBLANES, :]
        gates = _dot(xc.astype(BF16), r.wab[...])
        gate_a = _sigmoid(gates[:, 0:W_GROUP] + _vrow(r.vec, V_LRU_BA))
        gate_x = _sigmoid(gates[:, W_GROUP:2 * W_GROUP] + _vrow(r.vec, V_LRU_BX))
        log_a = _vrow(r.vec, V_LRU_CA) * gate_a
        a = jnp.exp(log_a)
        b = xc * gate_x * jnp.sqrt(-jnp.tanh(log_a) * (a * a + 1.0))
        for j in range(NSLAB):
            r.lru_a[j] = a[:, _slab(j)]
            r.lru_b[j] = b[:, _slab(j)]
    st.add(LRU, "lru_pre", 500, pre, ["p_lru_x"])

    def scan(first):
        for t in range(first, first + SCAN_PIECE):
            a_t, b_t = seg_tile(r.lru_a, 0, t), seg_tile(r.lru_b, 0, t)
            env.h, env.p = (b_t, a_t) if t == 0 else (a_t * env.h + b_t, a_t * env.p)
            r.lru_h[_tile(t), :] = env.h
            r.lru_p[_tile(t), :] = env.p
    for first in range(0, SEG, SCAN_PIECE):
        st.add(LRU, f"lru_scan{first}", 18 * SCAN_PIECE, functools.partial(scan, first))

    def carry():
        r.lru_e[0:SUBLANES, :] = env.h
        r.lru_e[SUBLANES:2 * SUBLANES, :] = env.p
        cur = r.lru_state[0:1, :]
        for s in range(SUBLANES):
            r.lru_cin[s:s + 1, :] = cur
            cur = r.lru_e[s:s + 1, :] + r.lru_e[SUBLANES + s:SUBLANES + s + 1, :] * cur
        r.lru_state[0:1, :] = cur
    st.add(LRU, "lru_carry", 50, carry)

    def fix(first):
        cin = r.lru_cin[...]
        for t in range(first, first + SCAN_PIECE):
            res = (r.lru_h[_tile(t), :] + r.lru_p[_tile(t), :] * cin) * _gelu_tanh(seg_tile(r.segin, NSLAB, t))
            for j in range(NSLAB):
                r.segout[NSLAB + j, _seg_rows(t), :] = res[:, _slab(j)]
    for first in range(0, SEG, SCAN_PIECE):
        st.add(LRU, f"lru_fix{first}", 38 * SCAN_PIECE, functools.partial(fix, first), ["p_lru_y"])


_SCRATCH = (
    ("hprev", (TIME_BLOCK, D_MODEL), F32),
    ("mix", (TIME_BLOCK, 4 * W_GROUP), BF16),
    ("res", (TIME_BLOCK, D_MODEL), F32),
    ("ybuf", (TIME_BLOCK, D_MODEL), F32),
    ("xcarry", (N_META, D_MODEL), F32),
    ("hn", (TIME_BLOCK, D_MODEL), F32),
    ("hnb", (TIME_BLOCK, D_MODEL), BF16),
    ("act", (TIME_BLOCK, D_FF), BF16),
    ("hb", (TIME_BLOCK, D_MODEL), BF16),
    ("hgp", (TIME_BLOCK, 4 * W_GROUP), F32),
    ("scp", (TIME_BLOCK, 3 * W_GROUP), F32),
    ("segin", (2 * NSLAB, TIME_BLOCK, LANES), F32),
    ("segout", (2 * NSLAB, TIME_BLOCK, LANES), F32),
    ("uperm", (TIME_BLOCK, W_GROUP), F32),
    ("bu", (TIME_BLOCK, 2 * S5_W), F32),
    ("xbf", (TIME_BLOCK, 2 * S5_W), BF16),
    ("s5_e", (SUBLANES, 2 * S5_W), F32),
    ("s5_cin", (SUBLANES, 2 * S5_W), F32),
    ("s5_state", (SUBLANES, 2 * S5_W), F32),
    ("sc_buf", (TIME_BLOCK + SUBLANES, W_GROUP), F32),
    ("hg_qt", (HG_LEVELS + 2, TIME_BLOCK, W_GROUP), BF16),
    ("hg_kt", (HG_LEVELS + 2, TIME_BLOCK, W_GROUP), BF16),
    ("hg_cin", (TIME_BLOCK, W_GROUP), F32),
    ("hg_state", (W_GROUP, W_GROUP), F32),
    ("hg_upd", (TIME_BLOCK // HG_CHUNK, W_GROUP, W_GROUP), F32),
    ("hg_o", (TIME_BLOCK, W_GROUP), F32),
    ("lru_xbuf", (TIME_BLOCK + SUBLANES, W_GROUP), F32),
    ("lru_a", (NSLAB, TIME_BLOCK, LANES), F32),
    ("lru_b", (NSLAB, TIME_BLOCK, LANES), F32),
    ("lru_h", (TIME_BLOCK, W_GROUP), F32),
    ("lru_p", (TIME_BLOCK, W_GROUP), F32),
    ("lru_e", (2 * SUBLANES, W_GROUP), F32),
    ("lru_cin", (SUBLANES, W_GROUP), F32),
    ("lru_state", (SUBLANES, W_GROUP), F32),
)
_CONSTS = ("w_in", "vec", "bmat", "cmat", "lam", "ptab", "glu_w", "wab", "headmask", "bdmask", "ones",
           "lv", "w_out", "ln", "w_ffn_in", "w_ffn_out")


def _layer_kernel(nblk, nsteps, first, last, n_tok, *refs):
    names = (("x", "meta") if first else ("h",)) + _CONSTS + ("out",) + tuple(s[0] for s in _SCRATCH)
    base = types.SimpleNamespace(**dict(zip(names, refs, strict=True)))
    gstep = pl.program_id(0)
    gsteps = nsteps // SUBSTEPS
    first_block = (gstep * SUBSTEPS) % nblk

    @pl.when(gstep == 0)
    def _():
        base.hprev[...] = jnp.zeros_like(base.hprev)
        base.mix[...] = jnp.zeros_like(base.mix)
        base.ybuf[...] = jnp.zeros_like(base.ybuf)

    @pl.when(first_block == 0)
    def _():
        if first:
            base.xcarry[...] = base.meta[...]
        base.s5_state[...] = jnp.zeros_like(base.s5_state)
        base.sc_buf[0:SUBLANES, :] = jnp.zeros((SUBLANES, W_GROUP), F32)
        base.hg_state[...] = jnp.zeros_like(base.hg_state)
        base.lru_xbuf[0:SUBLANES, :] = jnp.zeros((SUBLANES, W_GROUP), F32)
        base.lru_state[...] = jnp.zeros_like(base.lru_state)

    def body(sub, mixers, channel):
        r = types.SimpleNamespace(**vars(base))
        rows = slice(sub * TIME_BLOCK, (sub + 1) * TIME_BLOCK)
        r.out = base.out.at[rows]
        block = jnp.minimum(gstep * SUBSTEPS + sub, nsteps - 1) % nblk
        if first:
            r.x = base.x.at[rows]
            r.read_h = functools.partial(_first_layer_block, r, block, n_tok)
        else:
            r.h = base.h.at[rows]
            r.read_h = lambda: r.h[...]

        def finish():
            for j in range(NSLAB):
                r.mix[:, _slab(j)] = r.segout[j].astype(BF16)
                r.mix[:, _slab(j, 3 * W_GROUP)] = r.segout[NSLAB + j].astype(BF16)
            r.hprev[...] = r.read_h()
            if first:
                r.xcarry[...] = r.x[TIME_BLOCK - N_META:TIME_BLOCK, :]

        st = _Streams()
        if mixers:
            proj = _in_proj_pieces(st, r)
        _ln2_pieces(st, r, N_META if last else 0, with_tail=channel)
        if channel:
            _out_proj_pieces(st, r)
        if mixers:
            _in_proj_rest(st, r, proj)
        if channel:
            _ffn_pieces(st, r)
        if mixers:
            _s5_head(st, r)
        if channel:
            _ln1_piece(st, r)
        if mixers:
            _s5_pieces(st, r)
            _hgrn2_pieces(st, r)
            _rglru_pieces(st, r)
            _short_conv_pieces(st, r)
            st.add(VPU, "finish", 200, finish,
                   [LAST_OUT_PROJ, "s5_post", f"lru_fix{SEG - SCAN_PIECE}"])
        st.emit()

    @pl.when(gstep < gsteps)
    def _():
        for sub in range(SUBSTEPS):
            body(sub, True, True)

    @pl.when(gstep == gsteps)
    def _():
        body(0, False, True)
        body(1, False, False)


def _const_spec(shape):
    return pl.BlockSpec(shape, lambda *_: (0,) * len(shape), pipeline_mode=pl.Buffered(1))


def _layer_spec(shape, layer):
    return pl.BlockSpec((None,) + shape[1:], lambda *_: (layer, 0, 0), pipeline_mode=pl.Buffered(1))


def _layer_call(inputs, consts, bsz, seq, layer, first, last):
    tb = TIME_BLOCK
    n_tok = N_META + seq
    nblk = pl.cdiv(pl.cdiv(n_tok, tb), SUBSTEPS) * SUBSTEPS
    nsteps = bsz * nblk
    assert PIPELINE_DEPTH == SUBSTEPS
    win, nwin, gsteps = SUBSTEPS * tb, nblk // SUBSTEPS, nsteps // SUBSTEPS

    def in_block(n):
        return jnp.minimum(n, gsteps - 1)

    def out_block(n):
        return jnp.maximum(n - 1, 0)
    if first:
        in_specs = [pl.BlockSpec((None, win, D_MODEL), lambda n: (in_block(n) // nwin, in_block(n) % nwin, 0)),
                    _const_spec((N_META, D_MODEL))]
    else:
        in_specs = [pl.BlockSpec((win, D_MODEL), lambda n: (in_block(n), 0))]
    if last:
        out_spec = pl.BlockSpec((None, win, D_MODEL), lambda n: (out_block(n) // nwin, out_block(n) % nwin, 0))
        out_shape = jax.ShapeDtypeStruct((bsz, seq, D_MODEL), F32)
    else:
        out_spec = pl.BlockSpec((win, D_MODEL), lambda n: (out_block(n), 0))
        out_shape = jax.ShapeDtypeStruct((nsteps * tb, D_MODEL), F32)
    return pl.pallas_call(
        functools.partial(_layer_kernel, nblk, nsteps, first, last, n_tok),
        grid=(gsteps + 1,),
        in_specs=in_specs + [_layer_spec(c.shape, layer) if c.ndim == 3 else _const_spec(c.shape) for c in consts],
        out_specs=out_spec,
        out_shape=out_shape,
        scratch_shapes=[pltpu.VMEM(shape, dtype) for _, shape, dtype in _SCRATCH],
        compiler_params=pltpu.CompilerParams(
            dimension_semantics=("arbitrary",), vmem_limit_bytes=VMEM_LIMIT_BYTES),
    )(*inputs, *consts)


def _block_diag(blocks):
    layers, n, a, b = blocks.shape
    eye = jnp.eye(n, dtype=blocks.dtype)
    return jnp.einsum('lnab,nm->lnamb', blocks, eye).reshape(layers, n * a, n * b)


def _hgrn2_level_map():
    t = np.arange(HG_CHUNK, dtype=np.int32)[:, None]
    s = np.arange(HG_CHUNK, dtype=np.int32)[None, :]
    x = t ^ s
    lv = np.full((HG_CHUNK, HG_CHUNK), -1, np.int32)
    for level in range(HG_LEVELS):
        n = 1 << level
        lv = np.where((x >= n) & (x < 2 * n) & (t > s), level, lv)
    lv = np.where(t == s, HG_LEVELS, lv)
    return jnp.asarray(np.tile(lv, (1, HG_HEADS)).astype(np.int32))


def _prepare_layers(lb, s5_lam_re, s5_lam_im, s5_b_re, s5_b_im, s5_c_re, s5_c_im, s5_d,
                    s5_log_dt, s5_glu_w, s5_glu_b, sc_conv_w, hg_gnorm, lru_conv_w, lru_conv_b,
                    lru_wa, lru_ba, lru_wx, lru_bx, lru_a_param):
    layers = lb.shape[0]
    lam_r, lam_i = s5_lam_re.astype(F32), s5_lam_im.astype(F32)
    dt = jnp.exp(s5_log_dt.astype(F32))[..., None]
    arg_r, arg_i = lam_r * dt, lam_i * dt

    def lam_bar_pow(ar, ai, k):
        mag = jnp.exp(ar * k)
        return mag * jnp.cos(ai * k), mag * jnp.sin(ai * k)
    bar_r, bar_i = lam_bar_pow(arg_r, arg_i, 1.0)
    num_r, num_i = bar_r - 1.0, bar_i
    den = lam_r * lam_r + lam_i * lam_i
    coef_r = ((num_r * lam_r + num_i * lam_i) / den)[..., None]
    coef_i = ((num_i * lam_r - num_r * lam_i) / den)[..., None]
    b_r, b_i = s5_b_re.astype(F32), s5_b_im.astype(F32)
    bbar_r = jnp.swapaxes(coef_r * b_r - coef_i * b_i, -1, -2)
    bbar_i = jnp.swapaxes(coef_r * b_i + coef_i * b_r, -1, -2)
    bmat = jnp.concatenate([_block_diag(bbar_r), _block_diag(bbar_i)], axis=-1)
    c_r = jnp.swapaxes(s5_c_re.astype(F32), -1, -2)
    c_i = jnp.swapaxes(s5_c_im.astype(F32), -1, -2)
    cmat = jnp.concatenate([_block_diag(c_r), -_block_diag(c_i)], axis=-2)

    def flat(zr, zi):
        return jnp.concatenate([zr.reshape(layers, -1), zi.reshape(layers, -1)], axis=-1)
    lam_rows = jnp.stack([flat(bar_r, bar_i), flat(*lam_bar_pow(arg_r, arg_i, float(SEG)))], axis=1)
    steps = jnp.arange(1, SEG + 1, dtype=F32)[None, :, None, None]
    pw_r, pw_i = lam_bar_pow(arg_r[:, None], arg_i[:, None], steps)
    ptab = jnp.concatenate([pw_r.reshape(layers, SEG, -1), pw_i.reshape(layers, SEG, -1)], axis=-1)

    rows = [None] * N_VEC
    rows[V_S5_D] = s5_d
    rows[V_GLU_B] = s5_glu_b
    rows[V_SC_W0], rows[V_SC_W1], rows[V_SC_W2] = sc_conv_w[:, 0], sc_conv_w[:, 1], sc_conv_w[:, 2]
    rows[V_HG_LOGLB] = jnp.maximum(jnp.log(lb), -1e30)
    rows[V_HG_LOG1MLB] = jnp.log1p(-lb)
    rows[V_HG_1MLB] = 1.0 - lb
    rows[V_HG_GNORM] = hg_gnorm
    for i, v in enumerate((V_LRU_W0, V_LRU_W1, V_LRU_W2, V_LRU_W3)):
        rows[v] = lru_conv_w[:, i]
    rows[V_LRU_CB] = lru_conv_b
    rows[V_LRU_CA] = -LRU_C * jax.nn.softplus(-lru_a_param.astype(F32))
    rows[V_LRU_BA] = lru_ba
    rows[V_LRU_BX] = lru_bx
    zero = jnp.zeros((layers, W_GROUP), F32)
    vec = jnp.stack([zero if r is None else r.astype(F32) for r in rows], axis=1)
    wab = jnp.concatenate([_block_diag(lru_wa.astype(F32)), _block_diag(lru_wx.astype(F32))], axis=-1)
    return (vec, bmat.astype(BF16), cmat.astype(BF16), lam_rows, ptab, s5_glu_w.astype(BF16), wab.astype(BF16))


def kernel(x, meta_tokens, hg_lb_raw, w_in, w_out, s5_lam_re, s5_lam_im, s5_b_re, s5_b_im, s5_c_re, s5_c_im, s5_d, s5_log_dt, s5_glu_w, s5_glu_b, sc_conv_w, hg_gnorm, lru_conv_w, lru_conv_b, lru_wa, lru_ba, lru_wx, lru_bx, lru_a_param, ln1_g, ln1_b, w_ffn_in, w_ffn_out, ln2_g, ln2_b):
    bsz, seq, d = x.shape
    assert d == D_MODEL
    inputs = (x.astype(F32), meta_tokens.astype(F32))

    lb_all = jnp.cumsum(jax.nn.softmax(hg_lb_raw.astype(F32), axis=0), axis=0)
    lb_all = lb_all - lb_all[0:1]

    head = np.arange(W_GROUP) // HG_HEAD_DIM
    same_head = (head[:, None] == head[None, :]).astype(np.float32)
    bdmask = jnp.asarray(same_head)
    headmask = jnp.asarray(same_head, dtype=BF16)
    ones = jnp.asarray(same_head / HG_HEAD_DIM, dtype=BF16)
    lv = _hgrn2_level_map()

    w_in_bf, w_out_bf = w_in.astype(BF16), w_out.astype(BF16)
    w_ffn_in_bf, w_ffn_out_bf = w_ffn_in.astype(BF16), w_ffn_out.astype(BF16)
    prep = _prepare_layers(lb_all, s5_lam_re, s5_lam_im, s5_b_re, s5_b_im, s5_c_re, s5_c_im,
                           s5_d, s5_log_dt, s5_glu_w, s5_glu_b, sc_conv_w, hg_gnorm, lru_conv_w,
                           lru_conv_b, lru_wa, lru_ba, lru_wx, lru_bx, lru_a_param)
    ln = jnp.stack([ln1_g, ln1_b, ln2_g, ln2_b], axis=1).astype(F32)
    consts = (w_in_bf,) + prep + (headmask, bdmask, ones, lv, w_out_bf, ln, w_ffn_in_bf, w_ffn_out_bf)
    for l in range(DEPTH):
        inputs = (_layer_call(inputs, consts, bsz, seq, l, first=l == 0, last=l == DEPTH - 1),)
    return inputs[0]
```
